```python
import math
import jax, jax.numpy as jnp
from jax import lax
import numpy as np

D_MODEL = 1024
BATCH = 4
SEQ = 4096
DEPTH = 1

CONV_WIDTH = 512
CONV_KERNEL = 31
N_HEADS = 8
HEAD_DIM = 64
ATTN_WIDTH = N_HEADS * HEAD_DIM
IDX_HEADS = 8
IDX_DIM = 64
TOPK_MAX = 256
Q_BLOCK = 128
ROPE_THETA = 10000.0
N_GROUPS = 4
EXPERTS_PER_GROUP = 4
N_EXPERTS = N_GROUPS * EXPERTS_PER_GROUP
TOP_K_EXPERTS = 2
D_EXPERT = 256
EPS = 1e-6
IN_SIZES = (2 * CONV_WIDTH, ATTN_WIDTH, ATTN_WIDTH, ATTN_WIDTH,
            IDX_HEADS * IDX_DIM, IDX_DIM, IDX_HEADS, D_MODEL, D_MODEL)
IN_COLS = sum(IN_SIZES)

kernel_name = "hybrid_conv_dsa_hmoe_block"


def rms_norm(x, g):
    xf = x.astype(jnp.float32)
    y = xf * lax.rsqrt(jnp.mean(xf * xf, axis=-1, keepdims=True) + EPS)
    return (y * g.astype(jnp.float32)).astype(x.dtype)


def rope_tables(positions, dim, dtype):
    inv = 1.0 / (ROPE_THETA ** (jnp.arange(0, dim, 2, dtype=jnp.float32) / dim))
    ang = positions.astype(jnp.float32)[..., None] * inv
    return jnp.cos(ang).astype(dtype), jnp.sin(ang).astype(dtype)


def apply_rope(x, cos, sin):
    x1, x2 = jnp.split(x, 2, axis=-1)
    return jnp.concatenate([x1 * cos - x2 * sin, x2 * cos + x1 * sin], axis=-1)


def conformer_conv(u, w_dw, b_dw, ln_g, ln_b, w_out):
    a, b = jnp.split(u, 2, axis=-1)
    g = a * jax.nn.sigmoid(b)
    c = lax.conv_general_dilated(g, w_dw.astype(g.dtype), window_strides=(1,),
                                 padding=[(CONV_KERNEL - 1, 0)],
                                 dimension_numbers=('NWC', 'WIO', 'NWC'),
                                 feature_group_count=CONV_WIDTH) + b_dw
    cf = c.astype(jnp.float32)
    mu = jnp.mean(cf, axis=-1, keepdims=True)
    var = jnp.mean(jnp.square(cf - mu), axis=-1, keepdims=True)
    n = ((cf - mu) * lax.rsqrt(var + EPS) * ln_g + ln_b).astype(u.dtype)
    return jax.nn.silu(n) @ w_out


def dsa_attention(q, k, v, qi, ki, wi):
    B, S = q.shape[0], q.shape[1]
    n_blocks = S // Q_BLOCK
    topk = min(TOPK_MAX, S // 4)
    key_pos = jnp.arange(S)
    scale = HEAD_DIM ** -0.5

    def block(i):
        start = i * Q_BLOCK
        qb = lax.dynamic_slice_in_dim(q, start, Q_BLOCK, axis=1)
        qib = lax.dynamic_slice_in_dim(qi, start, Q_BLOCK, axis=1)
        wib = lax.dynamic_slice_in_dim(wi, start, Q_BLOCK, axis=1)
        t = start + jnp.arange(Q_BLOCK)
        causal = key_pos[None, :] <= t[:, None]
        rel = jax.nn.relu(jnp.einsum('bqhd,bsd->bqhs', qib, ki))
        scores = jnp.einsum('bqhs,bqh->bqs', rel, wib).astype(jnp.float32)
        scores = jnp.where(causal[None], scores, -jnp.inf)
        _, idx = lax.top_k(scores, topk)
        valid = idx <= t[None, :, None]
        k_sel = jax.vmap(lambda kb, ib: kb[ib])(k, idx)
        v_sel = jax.vmap(lambda vb, ib: vb[ib])(v, idx)
        logits = jnp.einsum('bqhd,bqkhd->bqhk', qb, k_sel).astype(jnp.float32) * scale
        logits = jnp.where(valid[:, :, None, :], logits, -jnp.inf)
        p = jax.nn.softmax(logits, axis=-1).astype(v.dtype)
        return jnp.einsum('bqhk,bqkhd->bqhd', p, v_sel)

    out = lax.map(block, jnp.arange(n_blocks))
    return jnp.transpose(out, (1, 0, 2, 3, 4)).reshape(B, S, N_HEADS * HEAD_DIM)


def hier_moe(h, w_rg, b_rg, w_re, b_re, w_gate, w_up, w_down):
    B, S, D = h.shape
    hf = h.reshape(-1, D)
    T = hf.shape[0]
    g_prob = jax.nn.softmax((hf @ w_rg + b_rg).astype(jnp.float32), axis=-1)
    p_g, gi = lax.top_k(g_prob, 1)
    e_logits = (jnp.einsum('td,dge->tge', hf, w_re) + b_re).astype(jnp.float32)
    e_sel = jnp.take_along_axis(e_logits, gi[:, :, None], axis=1)[:, 0]
    p_e, ei = lax.top_k(jax.nn.softmax(e_sel, axis=-1), TOP_K_EXPERTS)
    p_e = p_e / jnp.sum(p_e, axis=-1, keepdims=True)
    comb = p_g * p_e
    eid = gi * EXPERTS_PER_GROUP + ei
    gate_w = jnp.zeros((T, N_EXPERTS), jnp.float32).at[jnp.arange(T)[:, None], eid].add(comb)
    gate_w = gate_w.astype(h.dtype)
    hid = jax.nn.silu(jnp.einsum('td,edf->tef', hf, w_gate)) * jnp.einsum('td,edf->tef', hf, w_up)
    hid = hid * gate_w[:, :, None]
    y = jnp.einsum('tef,efd->td', hid, w_down)
    return y.reshape(B, S, D)


def setup_inputs(seed: int = 0) -> dict:
    key = jax.random.key(seed)
    ks = jax.random.split(key, 24)
    f32 = jnp.float32
    nrm = lambda k, shape, fan_in: jax.random.normal(k, shape, f32) * (fan_in ** -0.5)
    L = DEPTH
    return {
        "x": jax.random.normal(ks[0], (BATCH, SEQ, D_MODEL), f32),
        "positions": jnp.broadcast_to(jnp.arange(SEQ, dtype=jnp.int32), (BATCH, SEQ)),
        "g_mix": 1.0 + 0.02 * jax.random.normal(ks[1], (L, D_MODEL), f32),
        "w_in": nrm(ks[2], (L, D_MODEL, IN_COLS), D_MODEL),
        "w_dw": nrm(ks[3], (L, CONV_KERNEL, 1, CONV_WIDTH), CONV_KERNEL),
        "b_dw": 0.02 * jax.random.normal(ks[4], (L, CONV_WIDTH), f32),
        "ln_g": 1.0 + 0.02 * jax.random.normal(ks[5], (L, CONV_WIDTH), f32),
        "ln_b": 0.02 * jax.random.normal(ks[6], (L, CONV_WIDTH), f32),
        "w_conv_out": nrm(ks[7], (L, CONV_WIDTH, D_MODEL), CONV_WIDTH),
        "w_attn_out": nrm(ks[8], (L, ATTN_WIDTH, D_MODEL), ATTN_WIDTH),
        "w_o": nrm(ks[9], (L, D_MODEL, D_MODEL), D_MODEL),
        "g_ffn": 1.0 + 0.02 * jax.random.normal(ks[10], (L, D_MODEL), f32),
        "w_rg": nrm(ks[11], (L, D_MODEL, N_GROUPS), D_MODEL),
        "b_rg": 0.01 * jax.random.normal(ks[12], (L, N_GROUPS), f32),
        "w_re": nrm(ks[13], (L, D_MODEL, N_GROUPS, EXPERTS_PER_GROUP), D_MODEL),
        "b_re": 0.01 * jax.random.normal(ks[14], (L, N_GROUPS, EXPERTS_PER_GROUP), f32),
        "w_gate": nrm(ks[15], (L, N_EXPERTS, D_MODEL, D_EXPERT), D_MODEL),
        "w_up": nrm(ks[16], (L, N_EXPERTS, D_MODEL, D_EXPERT), D_MODEL),
        "w_down": nrm(ks[17], (L, N_EXPERTS, D_EXPERT, D_MODEL), D_EXPERT),
        "g_final": 1.0 + 0.02 * jax.random.normal(ks[18], (D_MODEL,), f32),
    }


def reference(x, positions, g_mix, w_in, w_dw, b_dw, ln_g, ln_b, w_conv_out, w_attn_out,
              w_o, g_ffn, w_rg, b_rg, w_re, b_re, w_gate, w_up, w_down, g_final):
    B, S, _ = x.shape
    cos, sin = rope_tables(positions, HEAD_DIM, x.dtype)
    cos_h, sin_h = cos[:, :, None, :], sin[:, :, None, :]
    idx_scale = (IDX_HEADS ** -0.5) * (IDX_DIM ** -0.5)
    split_pts = np.cumsum(IN_SIZES)[:-1].tolist()
    for l in range(DEPTH):
        h = rms_norm(x, g_mix[l])
        z = h @ w_in[l]
        u_conv, q, k, v, qi, ki, wi, gc, ga = jnp.split(z, split_pts, axis=-1)
        q = apply_rope(q.reshape(B, S, N_HEADS, HEAD_DIM), cos_h, sin_h)
        k = apply_rope(k.reshape(B, S, N_HEADS, HEAD_DIM), cos_h, sin_h)
        v = v.reshape(B, S, N_HEADS, HEAD_DIM)
        qi = apply_rope(qi.reshape(B, S, IDX_HEADS, IDX_DIM), cos_h, sin_h)
        ki = apply_rope(ki, cos, sin)
        wi = wi * idx_scale
        y_conv = conformer_conv(u_conv, w_dw[l], b_dw[l], ln_g[l], ln_b[l], w_conv_out[l])
        y_attn = dsa_attention(q, k, v, qi, ki, wi) @ w_attn_out[l]
        m = jax.nn.sigmoid(gc) * y_conv + jax.nn.sigmoid(ga) * y_attn
        x = x + m @ w_o[l]
        h2 = rms_norm(x, g_ffn[l])
        x = x + hier_moe(h2, w_rg[l], b_rg[l], w_re[l], b_re[l], w_gate[l], w_up[l], w_down[l])
    return rms_norm(x, g_final)
```

```python
import functools

import jax
import jax.numpy as jnp
from jax import lax
from jax.experimental import pallas as pl
from jax.experimental.pallas import tpu as pltpu

F32 = jnp.float32
BF16 = jnp.bfloat16

D_MODEL = 1024
CONV_WIDTH = 512
CONV_KERNEL = 31
N_HEADS = 8
HEAD_DIM = 64
ATTN_WIDTH = N_HEADS * HEAD_DIM
IDX_HEADS = 8
IDX_DIM = 64
TOPK_MAX = 256
Q_BLOCK = 128
ROPE_THETA = 10000.0
N_GROUPS = 4
EXPERTS_PER_GROUP = 4
N_EXPERTS = N_GROUPS * EXPERTS_PER_GROUP
D_EXPERT = 256
EPS = 1e-6
IN_SIZES = (2 * CONV_WIDTH, ATTN_WIDTH, ATTN_WIDTH, ATTN_WIDTH,
            IDX_HEADS * IDX_DIM, IDX_DIM, IDX_HEADS, D_MODEL, D_MODEL)

LANES = 128
SUBLANES = 8
VMEM_LIMIT_BYTES = 56 * 1024 * 1024

PROJ_ROWS = 256
CONV_ROWS = 512
CONV_HALO = 32
CONV_SUB = 64
KEY_CHUNK = 512
MIX_ROWS = 256
MOE_ROWS = 512
ROUTER_LANES = 128
MASK_VALUE = -1e30
INT_MIN = -2 ** 31


def _rms_rows(x, g):
    ms = jnp.mean(x * x, axis=-1, keepdims=True)
    return x * lax.rsqrt(ms + EPS) * g


def _sigmoid(x):
    return jax.nn.sigmoid(x)


def _rope128(z, cos, sin, first_half):
    rot = jnp.where(first_half, pltpu.roll(z, LANES - HEAD_DIM // 2, 1),
                    pltpu.roll(z, HEAD_DIM // 2, 1))
    return z * cos + rot * sin


def _inproj_kernel(x_ref, g_ref, cos_ref, sin_ref, wu_ref, wq_ref, wk_ref, wv_ref, wqi_ref,
                   wkk_ref, wwi_ref, wgc_ref, wga_ref,
                   u_ref, q_ref, k_ref, vt_ref, qi_ref, kk_ref, wi_ref, gc_ref, ga_ref):
    h = _rms_rows(x_ref[0], g_ref[...]).astype(BF16)
    cos = cos_ref[0]
    sin = sin_ref[0]
    lane = lax.broadcasted_iota(jnp.int32, cos.shape, 1)
    first_half = (lane % HEAD_DIM) < (HEAD_DIM // 2)

    def proj(w_ref):
        return jnp.dot(h, w_ref[...], preferred_element_type=F32)

    u_ref[0] = proj(wu_ref)
    gc_ref[0] = proj(wgc_ref)
    ga_ref[0] = proj(wga_ref)
    for w_ref, o_ref in ((wq_ref, q_ref), (wk_ref, k_ref), (wqi_ref, qi_ref)):
        z = proj(w_ref)
        for c in range(ATTN_WIDTH // LANES):
            sl = slice(c * LANES, (c + 1) * LANES)
            o_ref[0, :, sl] = _rope128(z[:, sl], cos, sin, first_half).astype(BF16)
    kk_ref[0] = _rope128(proj(wkk_ref), cos, sin, first_half).astype(BF16)
    idx_scale = (IDX_HEADS ** -0.5) * (IDX_DIM ** -0.5)
    wi_ref[0] = proj(wwi_ref) * idx_scale
    vt_ref[0] = proj(wv_ref).T.astype(BF16)


def _inproj(x, g_mix, cos_t, sin_t, ws):
    B, S, D = x.shape
    rows = PROJ_ROWS
    grid = (B, S // rows)
    row_spec = lambda n: pl.BlockSpec((1, rows, n), lambda b, i: (b, i, 0))
    w_spec = lambda w: pl.BlockSpec(w.shape, lambda b, i: (0, 0))
    out_shape = (
        jax.ShapeDtypeStruct((B, S, 2 * CONV_WIDTH), F32),
        jax.ShapeDtypeStruct((B, S, ATTN_WIDTH), BF16),
        jax.ShapeDtypeStruct((B, S, ATTN_WIDTH), BF16),
        jax.ShapeDtypeStruct((B, ATTN_WIDTH, S), BF16),
        jax.ShapeDtypeStruct((B, S, ATTN_WIDTH), BF16),
        jax.ShapeDtypeStruct((B, S, LANES), BF16),
        jax.ShapeDtypeStruct((B, S, LANES), F32),
        jax.ShapeDtypeStruct((B, S, D_MODEL), F32),
        jax.ShapeDtypeStruct((B, S, D_MODEL), F32),
    )
    out_specs = (
        row_spec(2 * CONV_WIDTH), row_spec(ATTN_WIDTH), row_spec(ATTN_WIDTH),
        pl.BlockSpec((1, ATTN_WIDTH, rows), lambda b, i: (b, 0, i)),
        row_spec(ATTN_WIDTH), row_spec(LANES), row_spec(LANES), row_spec(D_MODEL), row_spec(D_MODEL),
    )
    in_specs = [row_spec(D), pl.BlockSpec((1, D), lambda b, i: (0, 0)), row_spec(LANES), row_spec(LANES)]
    in_specs += [w_spec(w) for w in ws]
    return pl.pallas_call(
        _inproj_kernel, grid=grid, in_specs=in_specs, out_specs=out_specs, out_shape=out_shape,
        compiler_params=pltpu.CompilerParams(
            dimension_semantics=("arbitrary", "arbitrary"), vmem_limit_bytes=VMEM_LIMIT_BYTES),
        name="inproj",
    )(x, g_mix, cos_t, sin_t, *ws)


def _conv_kernel(u_ref, uh_ref, gc_ref, wdw_ref, bdw_ref, lng_ref, lnb_ref, wout_ref,
                 o_ref, g_buf, s_buf):
    i = pl.program_id(1)
    rows = u_ref.shape[1]
    uh = uh_ref[0]
    gh = uh[:, :CONV_WIDTH] * _sigmoid(uh[:, CONV_WIDTH:])
    g_buf[0:CONV_HALO, :] = jnp.where(i > 0, gh, 0.0)
    um = u_ref[0]
    g_buf[CONV_HALO:CONV_HALO + rows, :] = um[:, :CONV_WIDTH] * _sigmoid(um[:, CONV_WIDTH:])
    first = CONV_HALO - (CONV_KERNEL - 1)
    for r in range(rows // CONV_SUB):
        acc = jnp.zeros((CONV_SUB, CONV_WIDTH), F32)
        for j in range(CONV_KERNEL):
            acc = acc + wdw_ref[j:j + 1, :] * g_buf[pl.ds(r * CONV_SUB + first + j, CONV_SUB), :]
        c = acc + bdw_ref[...]
        mu = jnp.mean(c, axis=-1, keepdims=True)
        d = c - mu
        var = jnp.mean(d * d, axis=-1, keepdims=True)
        n = d * lax.rsqrt(var + EPS) * lng_ref[...] + lnb_ref[...]
        s_buf[r * CONV_SUB:(r + 1) * CONV_SUB, :] = (n * _sigmoid(n)).astype(BF16)
    y = jnp.dot(s_buf[...], wout_ref[...], preferred_element_type=F32)
    o_ref[0] = _sigmoid(gc_ref[0]) * y


def _conv_branch(u, gc, w_dw, b_dw, ln_g, ln_b, w_out):
    B, S, _ = u.shape
    rows = CONV_ROWS
    halo_per_tile = rows // CONV_HALO
    grid = (B, S // rows)
    vec = lambda n: pl.BlockSpec((1, n), lambda b, i: (0, 0))
    return pl.pallas_call(
        _conv_kernel, grid=grid,
        in_specs=[
            pl.BlockSpec((1, rows, 2 * CONV_WIDTH), lambda b, i: (b, i, 0)),
            pl.BlockSpec((1, CONV_HALO, 2 * CONV_WIDTH),
                         lambda b, i: (b, jnp.maximum(i * halo_per_tile - 1, 0), 0)),
            pl.BlockSpec((1, rows, D_MODEL), lambda b, i: (b, i, 0)),
            pl.BlockSpec((CONV_KERNEL, CONV_WIDTH), lambda b, i: (0, 0)),
            vec(CONV_WIDTH), vec(CONV_WIDTH), vec(CONV_WIDTH),
            pl.BlockSpec((CONV_WIDTH, D_MODEL), lambda b, i: (0, 0)),
        ],
        out_specs=pl.BlockSpec((1, rows, D_MODEL), lambda b, i: (b, i, 0)),
        out_shape=jax.ShapeDtypeStruct((B, S, D_MODEL), F32),
        scratch_shapes=[pltpu.VMEM((CONV_HALO + rows, CONV_WIDTH), F32),
                        pltpu.VMEM((rows, CONV_WIDTH), BF16)],
        compiler_params=pltpu.CompilerParams(
            dimension_semantics=("arbitrary", "arbitrary"), vmem_limit_bytes=VMEM_LIMIT_BYTES),
        name="conv_branch",
    )(u, u, gc, w_dw, b_dw, ln_g, ln_b, w_out)


def _pair_rhs(x_bf16, scale=None):
    x = x_bf16.astype(F32)
    if scale is not None:
        x = x * scale
    lane = lax.broadcasted_iota(jnp.int32, x.shape, 1)
    lo = jnp.where(lane < HEAD_DIM, x, 0.0)
    hi = jnp.where(lane >= HEAD_DIM, x, 0.0)
    return jnp.concatenate([lo, hi], axis=0).astype(BF16)


def _nt_dot(a, b):
    return lax.dot_general(a, b, (((1,), (1,)), ((), ())), preferred_element_type=F32)


def _key_to_f32(key):
    bits = jnp.where(key >= 0, key, key ^ jnp.int32(0x7FFFFFFF))
    return lax.bitcast_convert_type(bits, F32)


def _attn_kernel(qi_ref, q_ref, wi_ref, kk_ref, k_ref, vt_ref, o_ref,
                 sc_ref, acc_ref, m_ref, l_ref):
    i = pl.program_id(1)
    kc = KEY_CHUNK
    n_chunks = i // (kc // Q_BLOCK) + 1
    topk = float(TOPK_MAX)
    t_idx = i * Q_BLOCK + lax.broadcasted_iota(jnp.int32, (1, Q_BLOCK), 1)
    s_iota = lax.broadcasted_iota(jnp.int32, (kc, Q_BLOCK), 0)
    n_pairs = N_HEADS // 2

    qi = qi_ref[0]
    idx_rhs = [_pair_rhs(qi[:, p * LANES:(p + 1) * LANES]) for p in range(IDX_HEADS // 2)]
    wi_t = wi_ref[0].T

    def score_body(c, carry):
        start = pl.multiple_of(c * kc, kc)
        kk = kk_ref[0, pl.ds(start, kc), :]
        s = jnp.zeros((kc, Q_BLOCK), F32)
        for p in range(IDX_HEADS // 2):
            r = _nt_dot(kk, idx_rhs[p])
            s = s + wi_t[2 * p:2 * p + 1, :] * jnp.maximum(r[:, :Q_BLOCK], 0.0)
            s = s + wi_t[2 * p + 1:2 * p + 2, :] * jnp.maximum(r[:, Q_BLOCK:], 0.0)
        causal = (start + s_iota) <= t_idx
        sc_ref[pl.ds(start, kc), :] = jnp.where(causal, s, -jnp.inf)
        return carry

    lax.fori_loop(0, n_chunks, score_body, 0)

    def count_ge(cand):
        def body(c, acc):
            start = pl.multiple_of(c * kc, kc)
            hit = jnp.where(sc_ref[pl.ds(start, kc), :] >= cand, 1.0, 0.0)
            return acc + jnp.sum(hit.reshape(kc // SUBLANES, SUBLANES, Q_BLOCK), axis=0)
        acc = lax.fori_loop(0, n_chunks, body, jnp.zeros((SUBLANES, Q_BLOCK), F32))
        return jnp.sum(acc, axis=0, keepdims=True)

    c0 = count_ge(jnp.zeros((1, Q_BLOCK), F32))
    key0 = jnp.where(c0 >= topk, jnp.int32(0), jnp.int32(INT_MIN))

    def bit_body(j, key):
        cand = key + lax.shift_left(jnp.int32(1), 30 - j)
        cnt = count_ge(_key_to_f32(cand))
        return jnp.where(cnt >= topk, cand, key)

    key = lax.fori_loop(0, 31, bit_body, key0)
    thr = _key_to_f32(key)
    thr = jnp.where(thr != thr, -jnp.inf, thr)

    def count_both(c, carry):
        ge, gt = carry
        start = pl.multiple_of(c * kc, kc)
        s = sc_ref[pl.ds(start, kc), :]
        ge = ge + jnp.sum(jnp.where(s >= thr, 1.0, 0.0).reshape(kc // SUBLANES, SUBLANES, Q_BLOCK), axis=0)
        gt = gt + jnp.sum(jnp.where(s > thr, 1.0, 0.0).reshape(kc // SUBLANES, SUBLANES, Q_BLOCK), axis=0)
        return ge, gt

    z8 = jnp.zeros((SUBLANES, Q_BLOCK), F32)
    ge8, gt8 = lax.fori_loop(0, n_chunks, count_both, (z8, z8))
    cnt_ge = jnp.sum(ge8, axis=0, keepdims=True)
    cnt_gt = jnp.sum(gt8, axis=0, keepdims=True)
    need = topk - cnt_gt
    excess = jnp.where((cnt_ge > topk) & (thr > -jnp.inf), 1.0, 0.0)
    has_excess = jnp.sum(excess) > 0.0

    @pl.when(jnp.logical_not(has_excess))
    def _():
        def body(c, carry):
            start = pl.multiple_of(c * kc, kc)
            s = sc_ref[pl.ds(start, kc), :]
            keep = (s >= thr) & ((start + s_iota) <= t_idx)
            sc_ref[pl.ds(start, kc), :] = jnp.where(keep, 0.0, MASK_VALUE)
            return carry
        lax.fori_loop(0, n_chunks, body, 0)

    @pl.when(has_excess)
    def _():
        r_i = lax.broadcasted_iota(jnp.int32, (kc, kc), 0)
        c_i = lax.broadcasted_iota(jnp.int32, (kc, kc), 1)
        strict_lower = jnp.where(c_i < r_i, 1.0, 0.0).astype(BF16)

        def body(c, before):
            start = pl.multiple_of(c * kc, kc)
            s = sc_ref[pl.ds(start, kc), :]
            tie = jnp.where(s == thr, 1.0, 0.0)
            rank = jnp.dot(strict_lower, tie.astype(BF16), preferred_element_type=F32) + before
            keep = (s > thr) | ((s == thr) & (rank < need))
            keep = keep & ((start + s_iota) <= t_idx)
            sc_ref[pl.ds(start, kc), :] = jnp.where(keep, 0.0, MASK_VALUE)
            return before + jnp.sum(tie, axis=0, keepdims=True)
        lax.fori_loop(0, n_chunks, body, jnp.zeros((1, Q_BLOCK), F32))

    q = q_ref[0]
    att_rhs = [_pair_rhs(q[:, p * LANES:(p + 1) * LANES], HEAD_DIM ** -0.5) for p in range(n_pairs)]
    m_ref[...] = jnp.full(m_ref.shape, -jnp.inf, F32)
    l_ref[...] = jnp.zeros(l_ref.shape, F32)
    acc_ref[...] = jnp.zeros(acc_ref.shape, F32)

    def att_body(c, carry):
        start = pl.multiple_of(c * kc, kc)
        bias = sc_ref[pl.ds(start, kc), :]
        for p in range(n_pairs):
            kp = k_ref[0, pl.ds(start, kc), p * LANES:(p + 1) * LANES]
            lg = _nt_dot(kp, att_rhs[p])
            for hh in range(2):
                h = 2 * p + hh
                x = lg[:, hh * Q_BLOCK:(hh + 1) * Q_BLOCK] + bias
                m_old = m_ref[h:h + 1, :]
                m_new = jnp.maximum(m_old, jnp.max(x, axis=0, keepdims=True))
                alpha = jnp.exp(m_old - m_new)
                pm = jnp.exp(x - m_new)
                l_ref[h:h + 1, :] = alpha * l_ref[h:h + 1, :] + jnp.sum(pm, axis=0, keepdims=True)
                vt = vt_ref[0, h * HEAD_DIM:(h + 1) * HEAD_DIM, pl.ds(start, kc)]
                pv = jnp.dot(vt, pm.astype(BF16), preferred_element_type=F32)
                rows = slice(h * HEAD_DIM, (h + 1) * HEAD_DIM)
                acc_ref[rows, :] = alpha * acc_ref[rows, :] + pv
                m_ref[h:h + 1, :] = m_new
        return carry

    lax.fori_loop(0, n_chunks, att_body, 0)

    for h in range(N_HEADS):
        rows = slice(h * HEAD_DIM, (h + 1) * HEAD_DIM)
        acc_ref[rows, :] = acc_ref[rows, :] / l_ref[h:h + 1, :]
    o_ref[0] = acc_ref[...].T.astype(BF16)


def _attention(qi, q, wi, kk, k, vt):
    B, S, _ = q.shape
    grid = (B, S // Q_BLOCK)
    blk = lambda n: pl.BlockSpec((1, Q_BLOCK, n), lambda b, i: (b, i, 0))
    full = lambda n: pl.BlockSpec((1, S, n), lambda b, i: (b, 0, 0))
    return pl.pallas_call(
        _attn_kernel, grid=grid,
        in_specs=[blk(ATTN_WIDTH), blk(ATTN_WIDTH), blk(LANES), full(LANES), full(ATTN_WIDTH),
                  pl.BlockSpec((1, ATTN_WIDTH, S), lambda b, i: (b, 0, 0))],
        out_specs=blk(ATTN_WIDTH),
        out_shape=jax.ShapeDtypeStruct((B, S, ATTN_WIDTH), BF16),
        scratch_shapes=[pltpu.VMEM((S, Q_BLOCK), F32),
                        pltpu.VMEM((ATTN_WIDTH, Q_BLOCK), F32),
                        pltpu.VMEM((N_HEADS, Q_BLOCK), F32),
                        pltpu.VMEM((N_HEADS, Q_BLOCK), F32)],
        compiler_params=pltpu.CompilerParams(
            dimension_semantics=("arbitrary", "arbitrary"), vmem_limit_bytes=VMEM_LIMIT_BYTES),
        name="sparse_attn",
    )(qi, q, wi, kk, k, vt)


def _lane_first(mask, lane):
    return jnp.min(jnp.where(mask, lane, ROUTER_LANES), axis=-1, keepdims=True)


def _router(logits):
    lane = lax.broadcasted_iota(jnp.int32, logits.shape, 1)
    neg = -jnp.inf
    gl = jnp.where(lane < N_GROUPS, logits, neg)
    gmax = jnp.max(gl, axis=-1, keepdims=True)
    gsum = jnp.sum(jnp.exp(gl - gmax), axis=-1, keepdims=True)
    p_g = 1.0 / gsum
    gi = _lane_first(gl == gmax, lane)
    e_lo = N_GROUPS + gi * EXPERTS_PER_GROUP
    in_group = (lane >= e_lo) & (lane < e_lo + EXPERTS_PER_GROUP)
    el = jnp.where(in_group, logits, neg)
    emax = jnp.max(el, axis=-1, keepdims=True)
    ee = jnp.exp(el - emax)
    pe = ee / jnp.sum(ee, axis=-1, keepdims=True)
    pe = jnp.where(in_group, pe, -1.0)
    p1 = jnp.max(pe, axis=-1, keepdims=True)
    l1 = _lane_first(pe == p1, lane)
    pe2 = jnp.where(lane == l1, -1.0, pe)
    p2 = jnp.max(pe2, axis=-1, keepdims=True)
    l2 = _lane_first(pe2 == p2, lane)
    tot = p1 + p2
    c1 = p_g * (p1 / tot)
    c2 = p_g * (p2 / tot)
    return jnp.where(lane == l1, c1, 0.0) + jnp.where(lane == l2, c2, 0.0)


def _mix_kernel(x_ref, mconv_ref, attn_ref, ga_ref, wao_ref, wo_ref, gffn_ref, wr_ref, br_ref,
                x1_ref, h2_ref, gw_ref):
    ya = jnp.dot(attn_ref[...], wao_ref[...], preferred_element_type=F32)
    m = mconv_ref[...] + _sigmoid(ga_ref[...]) * ya
    x1 = x_ref[...] + jnp.dot(m.astype(BF16), wo_ref[...], preferred_element_type=F32)
    x1_ref[...] = x1
    h2 = _rms_rows(x1, gffn_ref[...]).astype(BF16)
    h2_ref[...] = h2
    logits = jnp.dot(h2, wr_ref[...], preferred_element_type=F32) + br_ref[...]
    gw_ref[...] = _router(logits)


def _mix(x, mconv, attn, ga, w_ao, w_o, g_ffn, w_r, b_r):
    T = x.shape[0]
    rows = MIX_ROWS
    row = lambda n: pl.BlockSpec((rows, n), lambda i: (i, 0))
    const = lambda a: pl.BlockSpec(a.shape, lambda i: (0, 0))
    return pl.pallas_call(
        _mix_kernel, grid=(T // rows,),
        in_specs=[row(D_MODEL), row(D_MODEL), row(ATTN_WIDTH), row(D_MODEL),
                  const(w_ao), const(w_o), const(g_ffn), const(w_r), const(b_r)],
        out_specs=(row(D_MODEL), row(D_MODEL), row(ROUTER_LANES)),
        out_shape=(jax.ShapeDtypeStruct((T, D_MODEL), F32),
                   jax.ShapeDtypeStruct((T, D_MODEL), BF16),
                   jax.ShapeDtypeStruct((T, ROUTER_LANES), F32)),
        compiler_params=pltpu.CompilerParams(
            dimension_semantics=("arbitrary",), vmem_limit_bytes=VMEM_LIMIT_BYTES),
        name="mix_router",
    )(x, mconv, attn, ga, w_ao, w_o, g_ffn, w_r, b_r)


def _moe_kernel(h2_ref, gw_ref, x1_ref, wg_ref, wu_ref, wd_ref, gfin_ref, o_ref, acc_ref):
    e = pl.program_id(1)

    @pl.when(e == 0)
    def _():
        acc_ref[...] = jnp.zeros(acc_ref.shape, F32)

    h2 = h2_ref[...]
    gate = jnp.dot(h2, wg_ref[0], preferred_element_type=F32)
    up = jnp.dot(h2, wu_ref[0], preferred_element_type=F32)
    gw = gw_ref[...]
    lane = lax.broadcasted_iota(jnp.int32, gw.shape, 1)
    gwe = jnp.sum(jnp.where(lane == e + N_GROUPS, gw, 0.0), axis=-1, keepdims=True)
    hid = (gate * _sigmoid(gate)) * up * gwe
    acc_ref[...] += jnp.dot(hid.astype(BF16), wd_ref[0], preferred_element_type=F32)

    @pl.when(e == N_EXPERTS - 1)
    def _():
        o_ref[...] = _rms_rows(x1_ref[...] + acc_ref[...], gfin_ref[...])


def _moe(h2, gw, x1, w_gate, w_up, w_down, g_final):
    T = h2.shape[0]
    rows = MOE_ROWS
    row = lambda n: pl.BlockSpec((rows, n), lambda i, e: (i, 0))
    return pl.pallas_call(
        _moe_kernel, grid=(T // rows, N_EXPERTS),
        in_specs=[row(D_MODEL), row(ROUTER_LANES), row(D_MODEL),
                  pl.BlockSpec((1, D_MODEL, D_EXPERT), lambda i, e: (e, 0, 0)),
                  pl.BlockSpec((1, D_MODEL, D_EXPERT), lambda i, e: (e, 0, 0)),
                  pl.BlockSpec((1, D_EXPERT, D_MODEL), lambda i, e: (e, 0, 0)),
                  pl.BlockSpec((1, D_MODEL), lambda i, e: (0, 0))],
        out_specs=row(D_MODEL),
        out_shape=jax.ShapeDtypeStruct((T, D_MODEL), F32),
        scratch_shapes=[pltpu.VMEM((rows, D_MODEL), F32)],
        compiler_params=pltpu.CompilerParams(
            dimension_semantics=("arbitrary", "arbitrary"), vmem_limit_bytes=VMEM_LIMIT_BYTES),
        name="moe_final",
    )(h2, gw, x1, w_gate, w_up, w_down, g_final)


def _rope_tables(positions):
    half = HEAD_DIM // 2
    inv = 1.0 / (ROPE_THETA ** (jnp.arange(0, HEAD_DIM, 2, dtype=F32) / HEAD_DIM))
    ang = positions.astype(F32)[..., None] * inv
    cos, sin = jnp.cos(ang), jnp.sin(ang)
    reps = LANES // HEAD_DIM
    cos_t = jnp.tile(jnp.concatenate([cos, cos], axis=-1), (1, 1, reps))
    sin_t = jnp.tile(jnp.concatenate([-sin, sin], axis=-1), (1, 1, reps))
    del half
    return cos_t, sin_t


def _layer(x, cos_t, sin_t, g_mix, w_in, w_dw, b_dw, ln_g, ln_b, w_conv_out, w_attn_out, w_o,
           g_ffn, w_rg, b_rg, w_re, b_re, w_gate, w_up, w_down, g_final):
    B, S, D = x.shape
    T = B * S
    offs = [0]
    for n in IN_SIZES:
        offs.append(offs[-1] + n)
    col = lambda j: w_in[:, offs[j]:offs[j + 1]]
    w_ki, w_wi = col(5), col(6)
    ws = (
        col(0), col(1), col(2), col(3), col(4),
        jnp.concatenate([w_ki, w_ki], axis=1),
        jnp.pad(w_wi, ((0, 0), (0, LANES - IDX_HEADS))),
        col(7), col(8),
    )
    ws = tuple(w.astype(BF16) for w in ws)
    u, q, k, vt, qi, kk, wi, gc, ga = _inproj(x, g_mix.reshape(1, D), cos_t, sin_t, ws)

    mconv = _conv_branch(u, gc, w_dw.reshape(CONV_KERNEL, CONV_WIDTH), b_dw.reshape(1, -1),
                         ln_g.reshape(1, -1), ln_b.reshape(1, -1), w_conv_out.astype(BF16))
    attn = _attention(qi, q, wi, kk, k, vt)

    n_r = N_GROUPS + N_EXPERTS
    w_r = jnp.concatenate([w_rg, w_re.reshape(D, N_EXPERTS)], axis=1)
    w_r = jnp.pad(w_r, ((0, 0), (0, ROUTER_LANES - n_r))).astype(BF16)
    b_r = jnp.pad(jnp.concatenate([b_rg, b_re.reshape(N_EXPERTS)]), (0, ROUTER_LANES - n_r)).reshape(1, -1)
    x1, h2, gw = _mix(x.reshape(T, D), mconv.reshape(T, D), attn.reshape(T, ATTN_WIDTH), ga.reshape(T, D),
                      w_attn_out.astype(BF16), w_o.astype(BF16), g_ffn.reshape(1, D), w_r, b_r)
    out = _moe(h2, gw, x1, w_gate.astype(BF16), w_up.astype(BF16), w_down.astype(BF16),
               g_final.reshape(1, D))
    return out.reshape(B, S, D)


def kernel(x, positions, g_mix, w_in, w_dw, b_dw, ln_g, ln_b, w_conv_out, w_attn_out, w_o, g_ffn,
           w_rg, b_rg, w_re, b_re, w_gate, w_up, w_down, g_final):
    depth = g_mix.shape[0]
    assert depth == 1, "final norm is fused into the single layer's MoE call"
    cos_t, sin_t = _rope_tables(positions)
    return _layer(x, cos_t, sin_t, g_mix[0], w_in[0], w_dw[0], b_dw[0], ln_g[0], ln_b[0],
                  w_conv_out[0], w_attn_out[0], w_o[0], g_ffn[0], w_rg[0], b_rg[0], w_re[0],
                  b_re[0], w_gate[0], w_up[0], w_down[0], g_final)
```

```python
import functools

import jax
import jax.numpy as jnp
from jax import lax
from jax.experimental import pallas as pl
from jax.experimental.pallas import tpu as pltpu

F32 = jnp.float32
BF16 = jnp.bfloat16

D_MODEL = 1024
CONV_WIDTH = 512
CONV_KERNEL = 31
N_HEADS = 8
HEAD_DIM = 64
ATTN_WIDTH = N_HEADS * HEAD_DIM
IDX_HEADS = 8
IDX_DIM = 64
TOPK_MAX = 256
Q_BLOCK = 128
ROPE_THETA = 10000.0
N_GROUPS = 4
EXPERTS_PER_GROUP = 4
N_EXPERTS = N_GROUPS * EXPERTS_PER_GROUP
D_EXPERT = 256
EPS = 1e-6
IN_SIZES = (2 * CONV_WIDTH, ATTN_WIDTH, ATTN_WIDTH, ATTN_WIDTH,
            IDX_HEADS * IDX_DIM, IDX_DIM, IDX_HEADS, D_MODEL, D_MODEL)

LANES = 128
SUBLANES = 8
VMEM_LIMIT_BYTES = 56 * 1024 * 1024

PROJ_ROWS = 256
CONV_ROWS = 512
CONV_HALO = 32
CONV_SUB = 64
KEY_CHUNK = 512
ACC_ROWS = 64
HEAD_ACC_ROWS = 16
MIX_ROWS = 256
MOE_ROWS = 512
ROUTER_LANES = 128
MASK_VALUE = -1e30
INT_MIN = -2 ** 31


def _rms_rows(x, g):
    ms = jnp.mean(x * x, axis=-1, keepdims=True)
    return x * lax.rsqrt(ms + EPS) * g


def _sigmoid(x):
    return jax.nn.sigmoid(x)


def _rope128(z, cos, sin, first_half):
    rot = jnp.where(first_half, pltpu.roll(z, LANES - HEAD_DIM // 2, 1),
                    pltpu.roll(z, HEAD_DIM // 2, 1))
    return z * cos + rot * sin


def _inproj_kernel(x_ref, g_ref, cos_ref, sin_ref, wu_ref, wq_ref, wk_ref, wv_ref, wqi_ref,
                   wkk_ref, wwi_ref, wgc_ref, wga_ref,
                   u_ref, q_ref, k_ref, vt_ref, qi_ref, kk_ref, wi_ref, gc_ref, ga_ref):
    h = _rms_rows(x_ref[0], g_ref[...]).astype(BF16)
    cos = cos_ref[0]
    sin = sin_ref[0]
    lane = lax.broadcasted_iota(jnp.int32, cos.shape, 1)
    first_half = (lane % HEAD_DIM) < (HEAD_DIM // 2)

    def proj(w_ref):
        return jnp.dot(h, w_ref[...], preferred_element_type=F32)

    u_ref[0] = proj(wu_ref)
    gc_ref[0] = proj(wgc_ref)
    ga_ref[0] = proj(wga_ref)
    for w_ref, o_ref in ((wq_ref, q_ref), (wk_ref, k_ref), (wqi_ref, qi_ref)):
        z = proj(w_ref)
        for c in range(ATTN_WIDTH // LANES):
            sl = slice(c * LANES, (c + 1) * LANES)
            o_ref[0, :, sl] = _rope128(z[:, sl], cos, sin, first_half).astype(BF16)
    kk_ref[0] = _rope128(proj(wkk_ref), cos, sin, first_half).astype(BF16)
    idx_scale = (IDX_HEADS ** -0.5) * (IDX_DIM ** -0.5)
    wi_ref[0] = proj(wwi_ref) * idx_scale
    vt_ref[0] = proj(wv_ref).T.astype(BF16)


def _inproj(x, g_mix, cos_t, sin_t, ws):
    B, S, D = x.shape
    rows = PROJ_ROWS
    grid = (B, S // rows)
    row_spec = lambda n: pl.BlockSpec((1, rows, n), lambda b, i: (b, i, 0))
    w_spec = lambda w: pl.BlockSpec(w.shape, lambda b, i: (0, 0))
    out_shape = (
        jax.ShapeDtypeStruct((B, S, 2 * CONV_WIDTH), F32),
        jax.ShapeDtypeStruct((B, S, ATTN_WIDTH), BF16),
        jax.ShapeDtypeStruct((B, S, ATTN_WIDTH), BF16),
        jax.ShapeDtypeStruct((B, ATTN_WIDTH, S), BF16),
        jax.ShapeDtypeStruct((B, S, ATTN_WIDTH), BF16),
        jax.ShapeDtypeStruct((B, S, LANES), BF16),
        jax.ShapeDtypeStruct((B, S, LANES), F32),
        jax.ShapeDtypeStruct((B, S, D_MODEL), F32),
        jax.ShapeDtypeStruct((B, S, D_MODEL), F32),
    )
    out_specs = (
        row_spec(2 * CONV_WIDTH), row_spec(ATTN_WIDTH), row_spec(ATTN_WIDTH),
        pl.BlockSpec((1, ATTN_WIDTH, rows), lambda b, i: (b, 0, i)),
        row_spec(ATTN_WIDTH), row_spec(LANES), row_spec(LANES), row_spec(D_MODEL), row_spec(D_MODEL),
    )
    in_specs = [row_spec(D), pl.BlockSpec((1, D), lambda b, i: (0, 0)), row_spec(LANES), row_spec(LANES)]
    in_specs += [w_spec(w) for w in ws]
    return pl.pallas_call(
        _inproj_kernel, grid=grid, in_specs=in_specs, out_specs=out_specs, out_shape=out_shape,
        compiler_params=pltpu.CompilerParams(
            dimension_semantics=("arbitrary", "arbitrary"), vmem_limit_bytes=VMEM_LIMIT_BYTES),
        name="inproj",
    )(x, g_mix, cos_t, sin_t, *ws)


def _conv_kernel(u_ref, uh_ref, gc_ref, wdw_ref, bdw_ref, lng_ref, lnb_ref, wout_ref,
                 o_ref, g_buf, s_buf):
    i = pl.program_id(1)
    rows = u_ref.shape[1]
    uh = uh_ref[0]
    gh = uh[:, :CONV_WIDTH] * _sigmoid(uh[:, CONV_WIDTH:])
    g_buf[0:CONV_HALO, :] = jnp.where(i > 0, gh, 0.0)
    um = u_ref[0]
    g_buf[CONV_HALO:CONV_HALO + rows, :] = um[:, :CONV_WIDTH] * _sigmoid(um[:, CONV_WIDTH:])
    first = CONV_HALO - (CONV_KERNEL - 1)
    for r in range(rows // CONV_SUB):
        acc = jnp.zeros((CONV_SUB, CONV_WIDTH), F32)
        for j in range(CONV_KERNEL):
            acc = acc + wdw_ref[j:j + 1, :] * g_buf[pl.ds(r * CONV_SUB + first + j, CONV_SUB), :]
        c = acc + bdw_ref[...]
        mu = jnp.mean(c, axis=-1, keepdims=True)
        d = c - mu
        var = jnp.mean(d * d, axis=-1, keepdims=True)
        n = d * lax.rsqrt(var + EPS) * lng_ref[...] + lnb_ref[...]
        s_buf[r * CONV_SUB:(r + 1) * CONV_SUB, :] = (n * _sigmoid(n)).astype(BF16)
    y = jnp.dot(s_buf[...], wout_ref[...], preferred_element_type=F32)
    o_ref[0] = _sigmoid(gc_ref[0]) * y


def _conv_branch(u, gc, w_dw, b_dw, ln_g, ln_b, w_out):
    B, S, _ = u.shape
    rows = CONV_ROWS
    halo_per_tile = rows // CONV_HALO
    grid = (B, S // rows)
    vec = lambda n: pl.BlockSpec((1, n), lambda b, i: (0, 0))
    return pl.pallas_call(
        _conv_kernel, grid=grid,
        in_specs=[
            pl.BlockSpec((1, rows, 2 * CONV_WIDTH), lambda b, i: (b, i, 0)),
            pl.BlockSpec((1, CONV_HALO, 2 * CONV_WIDTH),
                         lambda b, i: (b, jnp.maximum(i * halo_per_tile - 1, 0), 0)),
            pl.BlockSpec((1, rows, D_MODEL), lambda b, i: (b, i, 0)),
            pl.BlockSpec((CONV_KERNEL, CONV_WIDTH), lambda b, i: (0, 0)),
            vec(CONV_WIDTH), vec(CONV_WIDTH), vec(CONV_WIDTH),
            pl.BlockSpec((CONV_WIDTH, D_MODEL), lambda b, i: (0, 0)),
        ],
        out_specs=pl.BlockSpec((1, rows, D_MODEL), lambda b, i: (b, i, 0)),
        out_shape=jax.ShapeDtypeStruct((B, S, D_MODEL), F32),
        scratch_shapes=[pltpu.VMEM((CONV_HALO + rows, CONV_WIDTH), F32),
                        pltpu.VMEM((rows, CONV_WIDTH), BF16)],
        compiler_params=pltpu.CompilerParams(
            dimension_semantics=("arbitrary", "arbitrary"), vmem_limit_bytes=VMEM_LIMIT_BYTES),
        name="conv_branch",
    )(u, u, gc, w_dw, b_dw, ln_g, ln_b, w_out)


def _pair_rhs(x_bf16, scale=None):
    x = x_bf16.astype(F32)
    if scale is not None:
        x = x * scale
    lane = lax.broadcasted_iota(jnp.int32, x.shape, 1)
    lo = jnp.where(lane < HEAD_DIM, x, 0.0)
    hi = jnp.where(lane >= HEAD_DIM, x, 0.0)
    return jnp.concatenate([lo, hi], axis=0).astype(BF16)


def _fold_rows(x, op, rows=ACC_ROWS):
    n = x.shape[0] // rows
    return op(x.reshape(n, rows, x.shape[1]), axis=0)


def _finish_rows(x, op):
    y = op(x.reshape(x.shape[0] // SUBLANES, SUBLANES, x.shape[1]), axis=0)
    return op(y, axis=0, keepdims=True)


def _nt_dot(a, b):
    return lax.dot_general(a, b, (((1,), (1,)), ((), ())), preferred_element_type=F32)


def _key_to_f32(key):
    bits = jnp.where(key >= 0, key, key ^ jnp.int32(0x7FFFFFFF))
    return lax.bitcast_convert_type(bits, F32)


def _attn_kernel(qi_ref, q_ref, wi_ref, kk_ref, k_ref, vt_ref, o_ref,
                 sc_ref, xs_ref, acc_ref):
    i = pl.program_id(1)
    kc = KEY_CHUNK
    n_chunks = i // (kc // Q_BLOCK) + 1
    topk = float(TOPK_MAX)
    t_idx = i * Q_BLOCK + lax.broadcasted_iota(jnp.int32, (1, Q_BLOCK), 1)
    s_iota = lax.broadcasted_iota(jnp.int32, (kc, Q_BLOCK), 0)
    n_pairs = N_HEADS // 2

    qi = qi_ref[0]
    idx_rhs = [_pair_rhs(qi[:, p * LANES:(p + 1) * LANES]) for p in range(IDX_HEADS // 2)]
    wi_t = wi_ref[0].T

    def score_body(c, carry):
        start = pl.multiple_of(c * kc, kc)
        kk = kk_ref[0, pl.ds(start, kc), :]
        s = jnp.zeros((kc, Q_BLOCK), F32)
        for p in range(IDX_HEADS // 2):
            r = _nt_dot(kk, idx_rhs[p])
            s = s + wi_t[2 * p:2 * p + 1, :] * jnp.maximum(r[:, :Q_BLOCK], 0.0)
            s = s + wi_t[2 * p + 1:2 * p + 2, :] * jnp.maximum(r[:, Q_BLOCK:], 0.0)
        causal = (start + s_iota) <= t_idx
        sc_ref[pl.ds(start, kc), :] = jnp.where(causal, s, -jnp.inf)
        return carry

    lax.fori_loop(0, n_chunks, score_body, 0)

    def count_ge(cand):
        def body(c, acc):
            start = pl.multiple_of(c * kc, kc)
            hit = jnp.where(sc_ref[pl.ds(start, kc), :] >= cand, 1.0, 0.0)
            return acc + _fold_rows(hit, jnp.sum)
        acc = lax.fori_loop(0, n_chunks, body, jnp.zeros((ACC_ROWS, Q_BLOCK), F32))
        return _finish_rows(acc, jnp.sum)

    c0 = count_ge(jnp.zeros((1, Q_BLOCK), F32))
    key0 = jnp.where(c0 >= topk, jnp.int32(0), jnp.int32(INT_MIN))

    def bit_body(j, key):
        cand = key + lax.shift_left(jnp.int32(1), 30 - j)
        cnt = count_ge(_key_to_f32(cand))
        return jnp.where(cnt >= topk, cand, key)

    key = lax.fori_loop(0, 31, bit_body, key0)
    thr = _key_to_f32(key)
    thr = jnp.where(thr != thr, -jnp.inf, thr)

    def count_both(c, carry):
        ge, gt = carry
        start = pl.multiple_of(c * kc, kc)
        s = sc_ref[pl.ds(start, kc), :]
        ge = ge + _fold_rows(jnp.where(s >= thr, 1.0, 0.0), jnp.sum)
        gt = gt + _fold_rows(jnp.where(s > thr, 1.0, 0.0), jnp.sum)
        return ge, gt

    z_acc = jnp.zeros((ACC_ROWS, Q_BLOCK), F32)
    ge_acc, gt_acc = lax.fori_loop(0, n_chunks, count_both, (z_acc, z_acc))
    cnt_ge = _finish_rows(ge_acc, jnp.sum)
    cnt_gt = _finish_rows(gt_acc, jnp.sum)
    need = topk - cnt_gt
    excess = jnp.where((cnt_ge > topk) & (thr > -jnp.inf), 1.0, 0.0)
    has_excess = jnp.sum(excess) > 0.0

    @pl.when(jnp.logical_not(has_excess))
    def _():
        def body(c, carry):
            start = pl.multiple_of(c * kc, kc)
            s = sc_ref[pl.ds(start, kc), :]
            keep = (s >= thr) & ((start + s_iota) <= t_idx)
            sc_ref[pl.ds(start, kc), :] = jnp.where(keep, 0.0, MASK_VALUE)
            return carry
        lax.fori_loop(0, n_chunks, body, 0)

    @pl.when(has_excess)
    def _():
        r_i = lax.broadcasted_iota(jnp.int32, (kc, kc), 0)
        c_i = lax.broadcasted_iota(jnp.int32, (kc, kc), 1)
        strict_lower = jnp.where(c_i < r_i, 1.0, 0.0).astype(BF16)

        def body(c, before):
            start = pl.multiple_of(c * kc, kc)
            s = sc_ref[pl.ds(start, kc), :]
            tie = jnp.where(s == thr, 1.0, 0.0)
            rank = jnp.dot(strict_lower, tie.astype(BF16), preferred_element_type=F32) + before
            keep = (s > thr) | ((s == thr) & (rank < need))
            keep = keep & ((start + s_iota) <= t_idx)
            sc_ref[pl.ds(start, kc), :] = jnp.where(keep, 0.0, MASK_VALUE)
            return before + jnp.sum(tie, axis=0, keepdims=True)
        lax.fori_loop(0, n_chunks, body, jnp.zeros((1, Q_BLOCK), F32))

    q = q_ref[0]
    att_rhs = [_pair_rhs(q[:, p * LANES:(p + 1) * LANES], HEAD_DIM ** -0.5) for p in range(n_pairs)]

    def qk_body(c, mx):
        start = pl.multiple_of(c * kc, kc)
        bias = sc_ref[pl.ds(start, kc), :]
        out = []
        for p in range(n_pairs):
            kp = k_ref[0, pl.ds(start, kc), p * LANES:(p + 1) * LANES]
            lg = _nt_dot(kp, att_rhs[p])
            for hh in range(2):
                h = 2 * p + hh
                x = lg[:, hh * Q_BLOCK:(hh + 1) * Q_BLOCK] + bias
                xs_ref[pl.ds(start, kc), h * Q_BLOCK:(h + 1) * Q_BLOCK] = x
                out.append(jnp.maximum(mx[h], _fold_rows(x, jnp.max, HEAD_ACC_ROWS)))
        return tuple(out)

    neg = jnp.full((HEAD_ACC_ROWS, Q_BLOCK), -jnp.inf, F32)
    mx = lax.fori_loop(0, n_chunks, qk_body, (neg,) * N_HEADS)
    m = [_finish_rows(mx[h], jnp.max) for h in range(N_HEADS)]
    acc_ref[...] = jnp.zeros(acc_ref.shape, F32)

    def pv_body(c, ls):
        start = pl.multiple_of(c * kc, kc)
        out = []
        for h in range(N_HEADS):
            pm = jnp.exp(xs_ref[pl.ds(start, kc), h * Q_BLOCK:(h + 1) * Q_BLOCK] - m[h])
            rows = slice(h * HEAD_DIM, (h + 1) * HEAD_DIM)
            acc_ref[rows, :] += jnp.dot(vt_ref[0, rows, pl.ds(start, kc)], pm.astype(BF16),
                                        preferred_element_type=F32)
            out.append(ls[h] + _fold_rows(pm, jnp.sum, HEAD_ACC_ROWS))
        return tuple(out)

    zl = jnp.zeros((HEAD_ACC_ROWS, Q_BLOCK), F32)
    ls = lax.fori_loop(0, n_chunks, pv_body, (zl,) * N_HEADS)
    for h in range(N_HEADS):
        rows = slice(h * HEAD_DIM, (h + 1) * HEAD_DIM)
        acc_ref[rows, :] = acc_ref[rows, :] / _finish_rows(ls[h], jnp.sum)
    o_ref[0] = acc_ref[...].T.astype(BF16)


def _attention(qi, q, wi, kk, k, vt):
    B, S, _ = q.shape
    grid = (B, S // Q_BLOCK)
    blk = lambda n: pl.BlockSpec((1, Q_BLOCK, n), lambda b, i: (b, i, 0))
    full = lambda n: pl.BlockSpec((1, S, n), lambda b, i: (b, 0, 0))
    return pl.pallas_call(
        _attn_kernel, grid=grid,
        in_specs=[blk(ATTN_WIDTH), blk(ATTN_WIDTH), blk(LANES), full(LANES), full(ATTN_WIDTH),
                  pl.BlockSpec((1, ATTN_WIDTH, S), lambda b, i: (b, 0, 0))],
        out_specs=blk(ATTN_WIDTH),
        out_shape=jax.ShapeDtypeStruct((B, S, ATTN_WIDTH), BF16),
        scratch_shapes=[pltpu.VMEM((S, Q_BLOCK), F32),
                        pltpu.VMEM((S, N_HEADS * Q_BLOCK), F32),
                        pltpu.VMEM((ATTN_WIDTH, Q_BLOCK), F32)],
        compiler_params=pltpu.CompilerParams(
            dimension_semantics=("arbitrary", "arbitrary"), vmem_limit_bytes=VMEM_LIMIT_BYTES),
        name="sparse_attn",
    )(qi, q, wi, kk, k, vt)


def _lane_first(mask, lane):
    return jnp.min(jnp.where(mask, lane, ROUTER_LANES), axis=-1, keepdims=True)


def _router(logits):
    lane = lax.broadcasted_iota(jnp.int32, logits.shape, 1)
    neg = -jnp.inf
    gl = jnp.where(lane < N_GROUPS, logits, neg)
    gmax = jnp.max(gl, axis=-1, keepdims=True)
    gsum = jnp.sum(jnp.exp(gl - gmax), axis=-1, keepdims=True)
    p_g = 1.0 / gsum
    gi = _lane_first(gl == gmax, lane)
    e_lo = N_GROUPS + gi * EXPERTS_PER_GROUP
    in_group = (lane >= e_lo) & (lane < e_lo + EXPERTS_PER_GROUP)
    el = jnp.where(in_group, logits, neg)
    emax = jnp.max(el, axis=-1, keepdims=True)
    ee = jnp.exp(el - emax)
    pe = ee / jnp.sum(ee, axis=-1, keepdims=True)
    pe = jnp.where(in_group, pe, -1.0)
    p1 = jnp.max(pe, axis=-1, keepdims=True)
    l1 = _lane_first(pe == p1, lane)
    pe2 = jnp.where(lane == l1, -1.0, pe)
    p2 = jnp.max(pe2, axis=-1, keepdims=True)
    l2 = _lane_first(pe2 == p2, lane)
    tot = p1 + p2
    c1 = p_g * (p1 / tot)
    c2 = p_g * (p2 / tot)
    return jnp.where(lane == l1, c1, 0.0) + jnp.where(lane == l2, c2, 0.0)


def _mix_kernel(x_ref, mconv_ref, attn_ref, ga_ref, wao_ref, wo_ref, gffn_ref, wr_ref, br_ref,
                x1_ref, h2_ref, gw_ref):
    ya = jnp.dot(attn_ref[...], wao_ref[...], preferred_element_type=F32)
    m = mconv_ref[...] + _sigmoid(ga_ref[...]) * ya
    x1 = x_ref[...] + jnp.dot(m.astype(BF16), wo_ref[...], preferred_element_type=F32)
    x1_ref[...] = x1
    h2 = _rms_rows(x1, gffn_ref[...]).astype(BF16)
    h2_ref[...] = h2
    logits = jnp.dot(h2, wr_ref[...], preferred_element_type=F32) + br_ref[...]
    gw_ref[...] = _router(logits)


def _mix(x, mconv, attn, ga, w_ao, w_o, g_ffn, w_r, b_r):
    T = x.shape[0]
    rows = MIX_ROWS
    row = lambda n: pl.BlockSpec((rows, n), lambda i: (i, 0))
    const = lambda a: pl.BlockSpec(a.shape, lambda i: (0, 0))
    return pl.pallas_call(
        _mix_kernel, grid=(T // rows,),
        in_specs=[row(D_MODEL), row(D_MODEL), row(ATTN_WIDTH), row(D_MODEL),
                  const(w_ao), const(w_o), const(g_ffn), const(w_r), const(b_r)],
        out_specs=(row(D_MODEL), row(D_MODEL), row(ROUTER_LANES)),
        out_shape=(jax.ShapeDtypeStruct((T, D_MODEL), F32),
                   jax.ShapeDtypeStruct((T, D_MODEL), BF16),
                   jax.ShapeDtypeStruct((T, ROUTER_LANES), F32)),
        compiler_params=pltpu.CompilerParams(
            dimension_semantics=("arbitrary",), vmem_limit_bytes=VMEM_LIMIT_BYTES),
        name="mix_router",
    )(x, mconv, attn, ga, w_ao, w_o, g_ffn, w_r, b_r)


def _moe_kernel(h2_ref, gw_ref, x1_ref, wg_ref, wu_ref, wd_ref, gfin_ref, o_ref, acc_ref):
    e = pl.program_id(1)

    @pl.when(e == 0)
    def _():
        acc_ref[...] = jnp.zeros(acc_ref.shape, F32)

    h2 = h2_ref[...]
    gate = jnp.dot(h2, wg_ref[0], preferred_element_type=F32)
    up = jnp.dot(h2, wu_ref[0], preferred_element_type=F32)
    gw = gw_ref[...]
    lane = lax.broadcasted_iota(jnp.int32, gw.shape, 1)
    gwe = jnp.sum(jnp.where(lane == e + N_GROUPS, gw, 0.0), axis=-1, keepdims=True)
    hid = (gate * _sigmoid(gate)) * up * gwe
    acc_ref[...] += jnp.dot(hid.astype(BF16), wd_ref[0], preferred_element_type=F32)

    @pl.when(e == N_EXPERTS - 1)
    def _():
        o_ref[...] = _rms_rows(x1_ref[...] + acc_ref[...], gfin_ref[...])


def _moe(h2, gw, x1, w_gate, w_up, w_down, g_final):
    T = h2.shape[0]
    rows = MOE_ROWS
    row = lambda n: pl.BlockSpec((rows, n), lambda i, e: (i, 0))
    return pl.pallas_call(
        _moe_kernel, grid=(T // rows, N_EXPERTS),
        in_specs=[row(D_MODEL), row(ROUTER_LANES), row(D_MODEL),
                  pl.BlockSpec((1, D_MODEL, D_EXPERT), lambda i, e: (e, 0, 0)),
                  pl.BlockSpec((1, D_MODEL, D_EXPERT), lambda i, e: (e, 0, 0)),
                  pl.BlockSpec((1, D_EXPERT, D_MODEL), lambda i, e: (e, 0, 0)),
                  pl.BlockSpec((1, D_MODEL), lambda i, e: (0, 0))],
        out_specs=row(D_MODEL),
        out_shape=jax.ShapeDtypeStruct((T, D_MODEL), F32),
        scratch_shapes=[pltpu.VMEM((rows, D_MODEL), F32)],
        compiler_params=pltpu.CompilerParams(
            dimension_semantics=("arbitrary", "arbitrary"), vmem_limit_bytes=VMEM_LIMIT_BYTES),
        name="moe_final",
    )(h2, gw, x1, w_gate, w_up, w_down, g_final)


def _rope_tables(positions):
    half = HEAD_DIM // 2
    inv = 1.0 / (ROPE_THETA ** (jnp.arange(0, HEAD_DIM, 2, dtype=F32) / HEAD_DIM))
    ang = positions.astype(F32)[..., None] * inv
    cos, sin = jnp.cos(ang), jnp.sin(ang)
    reps = LANES // HEAD_DIM
    cos_t = jnp.tile(jnp.concatenate([cos, cos], axis=-1), (1, 1, reps))
    sin_t = jnp.tile(jnp.concatenate([-sin, sin], axis=-1), (1, 1, reps))
    del half
    return cos_t, sin_t


def _layer(x, cos_t, sin_t, g_mix, w_in, w_dw, b_dw, ln_g, ln_b, w_conv_out, w_attn_out, w_o,
           g_ffn, w_rg, b_rg, w_re, b_re, w_gate, w_up, w_down, g_final):
    B, S, D = x.shape
    T = B * S
    offs = [0]
    for n in IN_SIZES:
        offs.append(offs[-1] + n)
    col = lambda j: w_in[:, offs[j]:offs[j + 1]]
    w_ki, w_wi = col(5), col(6)
    ws = (
        col(0), col(1), col(2), col(3), col(4),
        jnp.concatenate([w_ki, w_ki], axis=1),
        jnp.pad(w_wi, ((0, 0), (0, LANES - IDX_HEADS))),
        col(7), col(8),
    )
    ws = tuple(w.astype(BF16) for w in ws)
    u, q, k, vt, qi, kk, wi, gc, ga = _inproj(x, g_mix.reshape(1, D), cos_t, sin_t, ws)

    mconv = _conv_branch(u, gc, w_dw.reshape(CONV_KERNEL, CONV_WIDTH), b_dw.reshape(1, -1),
                         ln_g.reshape(1, -1), ln_b.reshape(1, -1), w_conv_out.astype(BF16))
    attn = _attention(qi, q, wi, kk, k, vt)

    n_r = N_GROUPS + N_EXPERTS
    w_r = jnp.concatenate([w_rg, w_re.reshape(D, N_EXPERTS)], axis=1)
    w_r = jnp.pad(w_r, ((0, 0), (0, ROUTER_LANES - n_r))).astype(BF16)
    b_r = jnp.pad(jnp.concatenate([b_rg, b_re.reshape(N_EXPERTS)]), (0, ROUTER_LANES - n_r)).reshape(1, -1)
    x1, h2, gw = _mix(x.reshape(T, D), mconv.reshape(T, D), attn.reshape(T, ATTN_WIDTH), ga.reshape(T, D),
                      w_attn_out.astype(BF16), w_o.astype(BF16), g_ffn.reshape(1, D), w_r, b_r)
    out = _moe(h2, gw, x1, w_gate.astype(BF16), w_up.astype(BF16), w_down.astype(BF16),
               g_final.reshape(1, D))
    return out.reshape(B, S, D)


def kernel(x, positions, g_mix, w_in, w_dw, b_dw, ln_g, ln_b, w_conv_out, w_attn_out, w_o, g_ffn,
           w_rg, b_rg, w_re, b_re, w_gate, w_up, w_down, g_final):
    depth = g_mix.shape[0]
    assert depth == 1, "final norm is fused into the single layer's MoE call"
    cos_t, sin_t = _rope_tables(positions)
    return _layer(x, cos_t, sin_t, g_mix[0], w_in[0], w_dw[0], b_dw[0], ln_g[0], ln_b[0],
                  w_conv_out[0], w_attn_out[0], w_o[0], g_ffn[0], w_rg[0], b_rg[0], w_re[0],
                  b_re[0], w_gate[0], w_up[0], w_down[0], g_final)
```

```python
import functools

import jax
import jax.numpy as jnp
from jax import lax
from jax.experimental import pallas as pl
from jax.experimental.pallas import tpu as pltpu

F32 = jnp.float32
BF16 = jnp.bfloat16

D_MODEL = 1024
CONV_WIDTH = 512
CONV_KERNEL = 31
N_HEADS = 8
HEAD_DIM = 64
ATTN_WIDTH = N_HEADS * HEAD_DIM
IDX_HEADS = 8
IDX_DIM = 64
TOPK_MAX = 256
ROPE_THETA = 10000.0
N_GROUPS = 4
EXPERTS_PER_GROUP = 4
N_EXPERTS = N_GROUPS * EXPERTS_PER_GROUP
D_EXPERT = 256
EPS = 1e-6
IN_SIZES = (2 * CONV_WIDTH, ATTN_WIDTH, ATTN_WIDTH, ATTN_WIDTH,
            IDX_HEADS * IDX_DIM, IDX_DIM, IDX_HEADS, D_MODEL, D_MODEL)

LANES = 128
SUBLANES = 8
VMEM_LIMIT_BYTES = 56 * 1024 * 1024

PROJ_ROWS = 256
CONV_ROWS = 512
CONV_HALO = 32
CONV_SUB = 64
ATTN_Q = 256
HEAD_GROUP = 4
KEY_CHUNK = 512
COUNT_ROWS = 256
COUNT_ACC_ROWS = 32
ACC_ROWS = 64
HEAD_ACC_ROWS = 8
MIX_ROWS = 256
MOE_ROWS = 1024
ROUTER_LANES = 128
MASK_VALUE = -1e30
INT_MIN = -2 ** 31


def _rms_rows(x, g):
    ms = jnp.mean(x * x, axis=-1, keepdims=True)
    return x * lax.rsqrt(ms + EPS) * g


def _sigmoid(x):
    return jax.nn.sigmoid(x)


def _rope128(z, cos, sin, first_half):
    rot = jnp.where(first_half, pltpu.roll(z, LANES - HEAD_DIM // 2, 1),
                    pltpu.roll(z, HEAD_DIM // 2, 1))
    return z * cos + rot * sin


def _inproj_kernel(x_ref, g_ref, cos_ref, sin_ref, wu_ref, wq_ref, wk_ref, wv_ref, wqi_ref,
                   wkk_ref, wwi_ref, wgc_ref, wga_ref,
                   u_ref, q_ref, k_ref, vt_ref, qi_ref, kk_ref, wi_ref, gc_ref, ga_ref):
    h = _rms_rows(x_ref[0], g_ref[...]).astype(BF16)
    cos = cos_ref[0]
    sin = sin_ref[0]
    lane = lax.broadcasted_iota(jnp.int32, cos.shape, 1)
    first_half = (lane % HEAD_DIM) < (HEAD_DIM // 2)

    def proj(w_ref):
        return jnp.dot(h, w_ref[...], preferred_element_type=F32)

    u_ref[0] = proj(wu_ref)
    gc_ref[0] = proj(wgc_ref)
    ga_ref[0] = proj(wga_ref)
    for w_ref, o_ref in ((wq_ref, q_ref), (wk_ref, k_ref), (wqi_ref, qi_ref)):
        z = proj(w_ref)
        for c in range(ATTN_WIDTH // LANES):
            sl = slice(c * LANES, (c + 1) * LANES)
            o_ref[0, :, sl] = _rope128(z[:, sl], cos, sin, first_half).astype(BF16)
    kk_ref[0] = _rope128(proj(wkk_ref), cos, sin, first_half).astype(BF16)
    idx_scale = (IDX_HEADS ** -0.5) * (IDX_DIM ** -0.5)
    wi_ref[0] = proj(wwi_ref) * idx_scale
    vt_ref[0] = proj(wv_ref).T.astype(BF16)


def _inproj(x, g_mix, cos_t, sin_t, ws):
    B, S, D = x.shape
    rows = PROJ_ROWS
    grid = (B, S // rows)
    row_spec = lambda n: pl.BlockSpec((1, rows, n), lambda b, i: (b, i, 0))
    w_spec = lambda w: pl.BlockSpec(w.shape, lambda b, i: (0, 0))
    out_shape = (
        jax.ShapeDtypeStruct((B, S, 2 * CONV_WIDTH), F32),
        jax.ShapeDtypeStruct((B, S, ATTN_WIDTH), BF16),
        jax.ShapeDtypeStruct((B, S, ATTN_WIDTH), BF16),
        jax.ShapeDtypeStruct((B, ATTN_WIDTH, S), BF16),
        jax.ShapeDtypeStruct((B, S, ATTN_WIDTH), BF16),
        jax.ShapeDtypeStruct((B, S, LANES), BF16),
        jax.ShapeDtypeStruct((B, S, LANES), F32),
        jax.ShapeDtypeStruct((B, S, D_MODEL), F32),
        jax.ShapeDtypeStruct((B, S, D_MODEL), F32),
    )
    out_specs = (
        row_spec(2 * CONV_WIDTH), row_spec(ATTN_WIDTH), row_spec(ATTN_WIDTH),
        pl.BlockSpec((1, ATTN_WIDTH, rows), lambda b, i: (b, 0, i)),
        row_spec(ATTN_WIDTH), row_spec(LANES), row_spec(LANES), row_spec(D_MODEL), row_spec(D_MODEL),
    )
    in_specs = [row_spec(D), pl.BlockSpec((1, D), lambda b, i: (0, 0)), row_spec(LANES), row_spec(LANES)]
    in_specs += [w_spec(w) for w in ws]
    return pl.pallas_call(
        _inproj_kernel, grid=grid, in_specs=in_specs, out_specs=out_specs, out_shape=out_shape,
        compiler_params=pltpu.CompilerParams(
            dimension_semantics=("arbitrary", "arbitrary"), vmem_limit_bytes=VMEM_LIMIT_BYTES),
        name="inproj",
    )(x, g_mix, cos_t, sin_t, *ws)


def _conv_kernel(u_ref, uh_ref, gc_ref, wdw_ref, bdw_ref, lng_ref, lnb_ref, wout_ref,
                 o_ref, g_buf, s_buf):
    i = pl.program_id(1)
    rows = u_ref.shape[1]
    uh = uh_ref[0]
    gh = uh[:, :CONV_WIDTH] * _sigmoid(uh[:, CONV_WIDTH:])
    g_buf[0, 0:CONV_HALO, :] = jnp.where(i > 0, gh, 0.0)
    um = u_ref[0]
    g_buf[0, CONV_HALO:CONV_HALO + rows, :] = um[:, :CONV_WIDTH] * _sigmoid(um[:, CONV_WIDTH:])
    span = CONV_HALO + rows - SUBLANES
    for r in range(1, SUBLANES):
        for base in range(0, span, CONV_SUB):
            n = min(CONV_SUB, span - base)
            g_buf[r, base:base + n, :] = g_buf[0, pl.ds(base + r, n), :]
    first = CONV_HALO - (CONV_KERNEL - 1)
    for rr in range(rows // CONV_SUB):
        acc = jnp.zeros((CONV_SUB, CONV_WIDTH), F32)
        for j in range(CONV_KERNEL):
            shift = (first + j) % SUBLANES
            row0 = rr * CONV_SUB + (first + j) - shift
            acc = acc + wdw_ref[j:j + 1, :] * g_buf[shift, row0:row0 + CONV_SUB, :]
        c = acc + bdw_ref[...]
        mu = jnp.mean(c, axis=-1, keepdims=True)
        d = c - mu
        var = jnp.mean(d * d, axis=-1, keepdims=True)
        n = d * lax.rsqrt(var + EPS) * lng_ref[...] + lnb_ref[...]
        s_buf[rr * CONV_SUB:(rr + 1) * CONV_SUB, :] = (n * _sigmoid(n)).astype(BF16)
    y = jnp.dot(s_buf[...], wout_ref[...], preferred_element_type=F32)
    o_ref[0] = _sigmoid(gc_ref[0]) * y


def _conv_branch(u, gc, w_dw, b_dw, ln_g, ln_b, w_out):
    B, S, _ = u.shape
    rows = CONV_ROWS
    halo_per_tile = rows // CONV_HALO
    grid = (B, S // rows)
    vec = lambda n: pl.BlockSpec((1, n), lambda b, i: (0, 0))
    return pl.pallas_call(
        _conv_kernel, grid=grid,
        in_specs=[
            pl.BlockSpec((1, rows, 2 * CONV_WIDTH), lambda b, i: (b, i, 0)),
            pl.BlockSpec((1, CONV_HALO, 2 * CONV_WIDTH),
                         lambda b, i: (b, jnp.maximum(i * halo_per_tile - 1, 0), 0)),
            pl.BlockSpec((1, rows, D_MODEL), lambda b, i: (b, i, 0)),
            pl.BlockSpec((CONV_KERNEL, CONV_WIDTH), lambda b, i: (0, 0)),
            vec(CONV_WIDTH), vec(CONV_WIDTH), vec(CONV_WIDTH),
            pl.BlockSpec((CONV_WIDTH, D_MODEL), lambda b, i: (0, 0)),
        ],
        out_specs=pl.BlockSpec((1, rows, D_MODEL), lambda b, i: (b, i, 0)),
        out_shape=jax.ShapeDtypeStruct((B, S, D_MODEL), F32),
        scratch_shapes=[pltpu.VMEM((SUBLANES, CONV_HALO + rows, CONV_WIDTH), F32),
                        pltpu.VMEM((rows, CONV_WIDTH), BF16)],
        compiler_params=pltpu.CompilerParams(
            dimension_semantics=("arbitrary", "arbitrary"), vmem_limit_bytes=VMEM_LIMIT_BYTES),
        name="conv_branch",
    )(u, u, gc, w_dw, b_dw, ln_g, ln_b, w_out)


def _pair_rhs(x_bf16, scale=None):
    x = x_bf16.astype(F32)
    if scale is not None:
        x = x * scale
    lane = lax.broadcasted_iota(jnp.int32, x.shape, 1)
    lo = jnp.where(lane < HEAD_DIM, x, 0.0)
    hi = jnp.where(lane >= HEAD_DIM, x, 0.0)
    return jnp.concatenate([lo, hi], axis=0).astype(BF16)


def _fold_rows(x, op, rows=ACC_ROWS):
    n = x.shape[0] // rows
    return op(x.reshape(n, rows, x.shape[1]), axis=0)


def _finish_rows(x, op):
    y = op(x.reshape(x.shape[0] // SUBLANES, SUBLANES, x.shape[1]), axis=0)
    return op(y, axis=0, keepdims=True)


def _nt_dot(a, b):
    return lax.dot_general(a, b, (((1,), (1,)), ((), ())), preferred_element_type=F32)


def _key_to_f32(key):
    bits = jnp.where(key >= 0, key, key ^ jnp.int32(0x7FFFFFFF))
    return lax.bitcast_convert_type(bits, F32)


def _attn_kernel(qi_ref, q_ref, wi_ref, kk_ref, k_ref, vt_ref, o_ref,
                 sc_ref, xs_ref, acc_ref):
    i = pl.program_id(1)
    kc = KEY_CHUNK
    n_chunks = ((i + 1) * ATTN_Q + kc - 1) // kc
    topk = float(TOPK_MAX)
    t_idx = i * ATTN_Q + lax.broadcasted_iota(jnp.int32, (1, ATTN_Q), 1)
    s_iota = lax.broadcasted_iota(jnp.int32, (kc, ATTN_Q), 0)
    n_pairs = N_HEADS // 2

    qi = qi_ref[0]
    idx_rhs = [_pair_rhs(qi[:, p * LANES:(p + 1) * LANES]) for p in range(IDX_HEADS // 2)]
    wi_t = wi_ref[0].T

    def score_body(c, carry):
        start = pl.multiple_of(c * kc, kc)
        kk = kk_ref[0, pl.ds(start, kc), :]
        s = jnp.zeros((kc, ATTN_Q), F32)
        for p in range(IDX_HEADS // 2):
            r = _nt_dot(kk, idx_rhs[p])
            s = s + wi_t[2 * p:2 * p + 1, :] * jnp.maximum(r[:, :ATTN_Q], 0.0)
            s = s + wi_t[2 * p + 1:2 * p + 2, :] * jnp.maximum(r[:, ATTN_Q:], 0.0)
        causal = (start + s_iota) <= t_idx
        sc_ref[pl.ds(start, kc), :] = jnp.where(causal, s, -jnp.inf)
        return carry

    lax.fori_loop(0, n_chunks, score_body, 0)

    n_count = n_chunks * (kc // COUNT_ROWS)

    def count_ge(cand):
        def body(c, acc):
            start = pl.multiple_of(c * COUNT_ROWS, COUNT_ROWS)
            hit = jnp.where(sc_ref[pl.ds(start, COUNT_ROWS), :] >= cand, 1.0, 0.0)
            return acc + _fold_rows(hit, jnp.sum, COUNT_ACC_ROWS)
        acc = lax.fori_loop(0, n_count, body, jnp.zeros((COUNT_ACC_ROWS, ATTN_Q), F32))
        return _finish_rows(acc, jnp.sum)

    c0 = count_ge(jnp.zeros((1, ATTN_Q), F32))
    key0 = jnp.where(c0 >= topk, jnp.int32(0), jnp.int32(INT_MIN))

    def bit_body(j, key):
        cand = key + lax.shift_left(jnp.int32(1), 30 - j)
        cnt = count_ge(_key_to_f32(cand))
        return jnp.where(cnt >= topk, cand, key)

    key = lax.fori_loop(0, 31, bit_body, key0)
    thr = _key_to_f32(key)
    thr = jnp.where(thr != thr, -jnp.inf, thr)

    def count_both(c, carry):
        ge, gt = carry
        start = pl.multiple_of(c * COUNT_ROWS, COUNT_ROWS)
        s = sc_ref[pl.ds(start, COUNT_ROWS), :]
        ge = ge + _fold_rows(jnp.where(s >= thr, 1.0, 0.0), jnp.sum, COUNT_ACC_ROWS)
        gt = gt + _fold_rows(jnp.where(s > thr, 1.0, 0.0), jnp.sum, COUNT_ACC_ROWS)
        return ge, gt

    z_acc = jnp.zeros((COUNT_ACC_ROWS, ATTN_Q), F32)
    ge_acc, gt_acc = lax.fori_loop(0, n_count, count_both, (z_acc, z_acc))
    cnt_ge = _finish_rows(ge_acc, jnp.sum)
    cnt_gt = _finish_rows(gt_acc, jnp.sum)
    need = topk - cnt_gt
    excess = jnp.where((cnt_ge > topk) & (thr > -jnp.inf), 1.0, 0.0)
    has_excess = jnp.sum(excess) > 0.0

    @pl.when(jnp.logical_not(has_excess))
    def _():
        def body(c, carry):
            start = pl.multiple_of(c * kc, kc)
            s = sc_ref[pl.ds(start, kc), :]
            keep = (s >= thr) & ((start + s_iota) <= t_idx)
            sc_ref[pl.ds(start, kc), :] = jnp.where(keep, 0.0, MASK_VALUE)
            return carry
        lax.fori_loop(0, n_chunks, body, 0)

    @pl.when(has_excess)
    def _():
        r_i = lax.broadcasted_iota(jnp.int32, (kc, kc), 0)
        c_i = lax.broadcasted_iota(jnp.int32, (kc, kc), 1)
        strict_lower = jnp.where(c_i < r_i, 1.0, 0.0).astype(BF16)

        def body(c, before):
            start = pl.multiple_of(c * kc, kc)
            s = sc_ref[pl.ds(start, kc), :]
            tie = jnp.where(s == thr, 1.0, 0.0)
            rank = jnp.dot(strict_lower, tie.astype(BF16), preferred_element_type=F32) + before
            keep = (s > thr) | ((s == thr) & (rank < need))
            keep = keep & ((start + s_iota) <= t_idx)
            sc_ref[pl.ds(start, kc), :] = jnp.where(keep, 0.0, MASK_VALUE)
            return before + jnp.sum(tie, axis=0, keepdims=True)
        lax.fori_loop(0, n_chunks, body, jnp.zeros((1, ATTN_Q), F32))

    q = q_ref[0]
    acc_ref[...] = jnp.zeros(acc_ref.shape, F32)
    neg = jnp.full((HEAD_ACC_ROWS, ATTN_Q), -jnp.inf, F32)
    zl = jnp.zeros((HEAD_ACC_ROWS, ATTN_Q), F32)
    n_groups = N_HEADS // HEAD_GROUP
    att_rhs = [_pair_rhs(q[:, p * LANES:(p + 1) * LANES], HEAD_DIM ** -0.5) for p in range(n_pairs)]

    def qk_part(start, g, mx):
        bias = sc_ref[pl.ds(start, kc), :]
        out = []
        for p in range(g * HEAD_GROUP // 2, (g + 1) * HEAD_GROUP // 2):
            kp = k_ref[0, pl.ds(start, kc), p * LANES:(p + 1) * LANES]
            lg = _nt_dot(kp, att_rhs[p])
            for hh in range(2):
                hl = 2 * p + hh - g * HEAD_GROUP
                x = lg[:, hh * ATTN_Q:(hh + 1) * ATTN_Q] + bias
                xs_ref[g % 2, pl.ds(start, kc), hl * ATTN_Q:(hl + 1) * ATTN_Q] = x
                out.append(jnp.maximum(mx[hl], _fold_rows(x, jnp.max, HEAD_ACC_ROWS)))
        return tuple(out)

    def pv_part(start, g, m, ls):
        out = []
        for hl in range(HEAD_GROUP):
            h = g * HEAD_GROUP + hl
            pm = jnp.exp(xs_ref[g % 2, pl.ds(start, kc), hl * ATTN_Q:(hl + 1) * ATTN_Q] - m[hl])
            rows = slice(h * HEAD_DIM, (h + 1) * HEAD_DIM)
            acc_ref[rows, :] += jnp.dot(vt_ref[0, rows, pl.ds(start, kc)], pm.astype(BF16),
                                        preferred_element_type=F32)
            out.append(ls[hl] + _fold_rows(pm, jnp.sum, HEAD_ACC_ROWS))
        return tuple(out)

    m = None
    for s in range(n_groups + 1):
        def body(c, carry, s=s, m=m):
            start = pl.multiple_of(c * kc, kc)
            mx, ls = carry
            if s < n_groups:
                mx = qk_part(start, s, mx)
            if s > 0:
                ls = pv_part(start, s - 1, m, ls)
            return mx, ls

        mx, ls = lax.fori_loop(0, n_chunks, body, ((neg,) * HEAD_GROUP, (zl,) * HEAD_GROUP))
        if s > 0:
            for hl in range(HEAD_GROUP):
                h = (s - 1) * HEAD_GROUP + hl
                rows = slice(h * HEAD_DIM, (h + 1) * HEAD_DIM)
                acc_ref[rows, :] = acc_ref[rows, :] / _finish_rows(ls[hl], jnp.sum)
        if s < n_groups:
            m = [_finish_rows(mx[hl], jnp.max) for hl in range(HEAD_GROUP)]
    o_ref[0] = acc_ref[...].T.astype(BF16)


def _attention(qi, q, wi, kk, k, vt):
    B, S, _ = q.shape
    grid = (B, S // ATTN_Q)
    blk = lambda n: pl.BlockSpec((1, ATTN_Q, n), lambda b, i: (b, i, 0))
    full = lambda n: pl.BlockSpec((1, S, n), lambda b, i: (b, 0, 0), pipeline_mode=pl.Buffered(1))
    return pl.pallas_call(
        _attn_kernel, grid=grid,
        in_specs=[blk(ATTN_WIDTH), blk(ATTN_WIDTH), blk(LANES), full(LANES), full(ATTN_WIDTH),
                  pl.BlockSpec((1, ATTN_WIDTH, S), lambda b, i: (b, 0, 0), pipeline_mode=pl.Buffered(1))],
        out_specs=blk(ATTN_WIDTH),
        out_shape=jax.ShapeDtypeStruct((B, S, ATTN_WIDTH), BF16),
        scratch_shapes=[pltpu.VMEM((S, ATTN_Q), F32),
                        pltpu.VMEM((2, S, HEAD_GROUP * ATTN_Q), F32),
                        pltpu.VMEM((ATTN_WIDTH, ATTN_Q), F32)],
        compiler_params=pltpu.CompilerParams(
            dimension_semantics=("arbitrary", "arbitrary"), vmem_limit_bytes=VMEM_LIMIT_BYTES),
        name="sparse_attn",
    )(qi, q, wi, kk, k, vt)


def _lane_first(mask, lane):
    return jnp.min(jnp.where(mask, lane, ROUTER_LANES), axis=-1, keepdims=True)


def _router(logits):
    lane = lax.broadcasted_iota(jnp.int32, logits.shape, 1)
    neg = -jnp.inf
    gl = jnp.where(lane < N_GROUPS, logits, neg)
    gmax = jnp.max(gl, axis=-1, keepdims=True)
    gsum = jnp.sum(jnp.exp(gl - gmax), axis=-1, keepdims=True)
    p_g = 1.0 / gsum
    gi = _lane_first(gl == gmax, lane)
    e_lo = N_GROUPS + gi * EXPERTS_PER_GROUP
    in_group = (lane >= e_lo) & (lane < e_lo + EXPERTS_PER_GROUP)
    el = jnp.where(in_group, logits, neg)
    emax = jnp.max(el, axis=-1, keepdims=True)
    ee = jnp.exp(el - emax)
    pe = ee / jnp.sum(ee, axis=-1, keepdims=True)
    pe = jnp.where(in_group, pe, -1.0)
    p1 = jnp.max(pe, axis=-1, keepdims=True)
    l1 = _lane_first(pe == p1, lane)
    pe2 = jnp.where(lane == l1, -1.0, pe)
    p2 = jnp.max(pe2, axis=-1, keepdims=True)
    l2 = _lane_first(pe2 == p2, lane)
    tot = p1 + p2
    c1 = p_g * (p1 / tot)
    c2 = p_g * (p2 / tot)
    return jnp.where(lane == l1, c1, 0.0) + jnp.where(lane == l2, c2, 0.0)


def _mix_kernel(x_ref, mconv_ref, attn_ref, ga_ref, wao_ref, wo_ref, gffn_ref, wr_ref, br_ref,
                x1_ref, h2_ref, gw_ref):
    ya = jnp.dot(attn_ref[...], wao_ref[...], preferred_element_type=F32)
    m = mconv_ref[...] + _sigmoid(ga_ref[...]) * ya
    x1 = x_ref[...] + jnp.dot(m.astype(BF16), wo_ref[...], preferred_element_type=F32)
    x1_ref[...] = x1
    h2 = _rms_rows(x1, gffn_ref[...]).astype(BF16)
    h2_ref[...] = h2
    logits = jnp.dot(h2, wr_ref[...], preferred_element_type=F32) + br_ref[...]
    gw_ref[...] = _router(logits)


def _mix(x, mconv, attn, ga, w_ao, w_o, g_ffn, w_r, b_r):
    T = x.shape[0]
    rows = MIX_ROWS
    row = lambda n: pl.BlockSpec((rows, n), lambda i: (i, 0))
    const = lambda a: pl.BlockSpec(a.shape, lambda i: (0, 0))
    return pl.pallas_call(
        _mix_kernel, grid=(T // rows,),
        in_specs=[row(D_MODEL), row(D_MODEL), row(ATTN_WIDTH), row(D_MODEL),
                  const(w_ao), const(w_o), const(g_ffn), const(w_r), const(b_r)],
        out_specs=(row(D_MODEL), row(D_MODEL), row(ROUTER_LANES)),
        out_shape=(jax.ShapeDtypeStruct((T, D_MODEL), F32),
                   jax.ShapeDtypeStruct((T, D_MODEL), BF16),
                   jax.ShapeDtypeStruct((T, ROUTER_LANES), F32)),
        compiler_params=pltpu.CompilerParams(
            dimension_semantics=("arbitrary",), vmem_limit_bytes=VMEM_LIMIT_BYTES),
        name="mix_router",
    )(x, mconv, attn, ga, w_ao, w_o, g_ffn, w_r, b_r)


def _moe_kernel(h2_ref, gw_ref, x1_ref, wg_ref, wu_ref, wd_ref, gfin_ref, o_ref, acc_ref):
    e = pl.program_id(1)

    @pl.when(e == 0)
    def _():
        acc_ref[...] = jnp.zeros(acc_ref.shape, F32)

    h2 = h2_ref[...]
    gate = jnp.dot(h2, wg_ref[0], preferred_element_type=F32)
    up = jnp.dot(h2, wu_ref[0], preferred_element_type=F32)
    gw = gw_ref[...]
    lane = lax.broadcasted_iota(jnp.int32, gw.shape, 1)
    gwe = jnp.sum(jnp.where(lane == e + N_GROUPS, gw, 0.0), axis=-1, keepdims=True)
    hid = (gate * _sigmoid(gate)) * up * gwe
    acc_ref[...] += jnp.dot(hid.astype(BF16), wd_ref[0], preferred_element_type=F32)

    @pl.when(e == N_EXPERTS - 1)
    def _():
        o_ref[...] = _rms_rows(x1_ref[...] + acc_ref[...], gfin_ref[...])


def _moe(h2, gw, x1, w_gate, w_up, w_down, g_final):
    T = h2.shape[0]
    rows = MOE_ROWS
    row = lambda n: pl.BlockSpec((rows, n), lambda i, e: (i, 0))
    return pl.pallas_call(
        _moe_kernel, grid=(T // rows, N_EXPERTS),
        in_specs=[row(D_MODEL), row(ROUTER_LANES), row(D_MODEL),
                  pl.BlockSpec((1, D_MODEL, D_EXPERT), lambda i, e: (e, 0, 0)),
                  pl.BlockSpec((1, D_MODEL, D_EXPERT), lambda i, e: (e, 0, 0)),
                  pl.BlockSpec((1, D_EXPERT, D_MODEL), lambda i, e: (e, 0, 0)),
                  pl.BlockSpec((1, D_MODEL), lambda i, e: (0, 0))],
        out_specs=row(D_MODEL),
        out_shape=jax.ShapeDtypeStruct((T, D_MODEL), F32),
        scratch_shapes=[pltpu.VMEM((rows, D_MODEL), F32)],
        compiler_params=pltpu.CompilerParams(
            dimension_semantics=("arbitrary", "arbitrary"), vmem_limit_bytes=VMEM_LIMIT_BYTES),
        name="moe_final",
    )(h2, gw, x1, w_gate, w_up, w_down, g_final)


def _rope_tables(positions):
    half = HEAD_DIM // 2
    inv = 1.0 / (ROPE_THETA ** (jnp.arange(0, HEAD_DIM, 2, dtype=F32) / HEAD_DIM))
    ang = positions.astype(F32)[..., None] * inv
    cos, sin = jnp.cos(ang), jnp.sin(ang)
    reps = LANES // HEAD_DIM
    cos_t = jnp.tile(jnp.concatenate([cos, cos], axis=-1), (1, 1, reps))
    sin_t = jnp.tile(jnp.concatenate([-sin, sin], axis=-1), (1, 1, reps))
    del half
    return cos_t, sin_t


def _layer(x, cos_t, sin_t, g_mix, w_in, w_dw, b_dw, ln_g, ln_b, w_conv_out, w_attn_out, w_o,
           g_ffn, w_rg, b_rg, w_re, b_re, w_gate, w_up, w_down, g_final):
    B, S, D = x.shape
    T = B * S
    offs = [0]
    for n in IN_SIZES:
        offs.append(offs[-1] + n)
    col = lambda j: w_in[:, offs[j]:offs[j + 1]]
    w_ki, w_wi = col(5), col(6)
    ws = (
        col(0), col(1), col(2), col(3), col(4),
        jnp.concatenate([w_ki, w_ki], axis=1),
        jnp.pad(w_wi, ((0, 0), (0, LANES - IDX_HEADS))),
        col(7), col(8),
    )
    ws = tuple(w.astype(BF16) for w in ws)
    u, q, k, vt, qi, kk, wi, gc, ga = _inproj(x, g_mix.reshape(1, D), cos_t, sin_t, ws)

    mconv = _conv_branch(u, gc, w_dw.reshape(CONV_KERNEL, CONV_WIDTH), b_dw.reshape(1, -1),
                         ln_g.reshape(1, -1), ln_b.reshape(1, -1), w_conv_out.astype(BF16))
    attn = _attention(qi, q, wi, kk, k, vt)

    n_r = N_GROUPS + N_EXPERTS
    w_r = jnp.concatenate([w_rg, w_re.reshape(D, N_EXPERTS)], axis=1)
    w_r = jnp.pad(w_r, ((0, 0), (0, ROUTER_LANES - n_r))).astype(BF16)
    b_r = jnp.pad(jnp.concatenate([b_rg, b_re.reshape(N_EXPERTS)]), (0, ROUTER_LANES - n_r)).reshape(1, -1)
    x1, h2, gw = _mix(x.reshape(T, D), mconv.reshape(T, D), attn.reshape(T, ATTN_WIDTH), ga.reshape(T, D),
                      w_attn_out.astype(BF16), w_o.astype(BF16), g_ffn.reshape(1, D), w_r, b_r)
    out = _moe(h2, gw, x1, w_gate.astype(BF16), w_up.astype(BF16), w_down.astype(BF16),
               g_final.reshape(1, D))
    return out.reshape(B, S, D)


def kernel(x, positions, g_mix, w_in, w_dw, b_dw, ln_g, ln_b, w_conv_out, w_attn_out, w_o, g_ffn,
           w_rg, b_rg, w_re, b_re, w_gate, w_up, w_down, g_final):
    depth = g_mix.shape[0]
    assert depth == 1, "final norm is fused into the single layer's MoE call"
    cos_t, sin_t = _rope_tables(positions)
    return _layer(x, cos_t, sin_t, g_mix[0], w_in[0], w_dw[0], b_dw[0], ln_g[0], ln_b[0],
                  w_conv_out[0], w_attn_out[0], w_o[0], g_ffn[0], w_rg[0], b_rg[0], w_re[0],
                  b_re[0], w_gate[0], w_up[0], w_down[0], g_final)
```

```python
import functools

import jax
import jax.numpy as jnp
from jax import lax
from jax.experimental import pallas as pl
from jax.experimental.pallas import tpu as pltpu

F32 = jnp.float32
BF16 = jnp.bfloat16

D_MODEL = 1024
CONV_WIDTH = 512
CONV_KERNEL = 31
N_HEADS = 8
HEAD_DIM = 64
ATTN_WIDTH = N_HEADS * HEAD_DIM
IDX_HEADS = 8
IDX_DIM = 64
TOPK_MAX = 256
ROPE_THETA = 10000.0
N_GROUPS = 4
EXPERTS_PER_GROUP = 4
N_EXPERTS = N_GROUPS * EXPERTS_PER_GROUP
D_EXPERT = 256
EPS = 1e-6
IN_SIZES = (2 * CONV_WIDTH, ATTN_WIDTH, ATTN_WIDTH, ATTN_WIDTH,
            IDX_HEADS * IDX_DIM, IDX_DIM, IDX_HEADS, D_MODEL, D_MODEL)

LANES = 128
SUBLANES = 8
VMEM_LIMIT_BYTES = 56 * 1024 * 1024

PROJ_ROWS = 512
CONV_ROWS = 512
CONV_HALO = 32
CONV_SUB = 64
ATTN_Q = 256
HEAD_GROUP = 4
KEY_CHUNK = 512
COUNT_ROWS = 256
COUNT_ACC_ROWS = 32
ACC_ROWS = 64
HEAD_ACC_ROWS = 8
MIX_ROWS = 512
MOE_ROWS = 1024
MOE_EXPERTS_PER_STEP = 2
ROUTER_LANES = 128
MASK_VALUE = -1e30
KEY16_MIN = -2 ** 15
PACK_ROWS = 16


def _rms_rows(x, g):
    ms = jnp.mean(x * x, axis=-1, keepdims=True)
    return x * lax.rsqrt(ms + EPS) * g


def _sigmoid(x):
    return jax.nn.sigmoid(x)


def _rope128(z, cos, sin, first_half):
    rot = jnp.where(first_half, pltpu.roll(z, LANES - HEAD_DIM // 2, 1),
                    pltpu.roll(z, HEAD_DIM // 2, 1))
    return z * cos + rot * sin


def _inproj_kernel(x_ref, g_ref, cos_ref, sin_ref, wu_ref, wq_ref, wk_ref, wv_ref, wqi_ref,
                   wkw_ref, wgc_ref, wga_ref,
                   u_ref, q_ref, k_ref, vt_ref, qi_ref, kk_ref, wi_ref, gc_ref, ga_ref):
    h = _rms_rows(x_ref[0], g_ref[...]).astype(BF16)
    cos = cos_ref[0]
    sin = sin_ref[0]
    lane = lax.broadcasted_iota(jnp.int32, cos.shape, 1)
    first_half = (lane % HEAD_DIM) < (HEAD_DIM // 2)

    def proj(w_ref):
        return jnp.dot(h, w_ref[...], preferred_element_type=F32)

    u_ref[0] = proj(wu_ref)
    gc_ref[0] = proj(wgc_ref)
    ga_ref[0] = proj(wga_ref)
    for w_ref, o_ref in ((wq_ref, q_ref), (wk_ref, k_ref), (wqi_ref, qi_ref)):
        z = proj(w_ref)
        for c in range(ATTN_WIDTH // LANES):
            sl = slice(c * LANES, (c + 1) * LANES)
            o_ref[0, :, sl] = _rope128(z[:, sl], cos, sin, first_half).astype(BF16)
    kw = proj(wkw_ref)
    kk_ref[0] = _rope128(kw[:, :LANES], cos, sin, first_half).astype(BF16)
    idx_scale = (IDX_HEADS ** -0.5) * (IDX_DIM ** -0.5)
    wi_ref[0] = kw[:, LANES:] * idx_scale
    vt_ref[0] = proj(wv_ref).T.astype(BF16)


def _inproj(x, g_mix, cos_t, sin_t, ws):
    B, S, D = x.shape
    rows = PROJ_ROWS
    grid = (B, S // rows)
    row_spec = lambda n: pl.BlockSpec((1, rows, n), lambda b, i: (b, i, 0))
    w_spec = lambda w: pl.BlockSpec(w.shape, lambda b, i: (0, 0), pipeline_mode=pl.Buffered(1))
    out_shape = (
        jax.ShapeDtypeStruct((B, S, 2 * CONV_WIDTH), F32),
        jax.ShapeDtypeStruct((B, S, ATTN_WIDTH), BF16),
        jax.ShapeDtypeStruct((B, S, ATTN_WIDTH), BF16),
        jax.ShapeDtypeStruct((B, ATTN_WIDTH, S), BF16),
        jax.ShapeDtypeStruct((B, S, ATTN_WIDTH), BF16),
        jax.ShapeDtypeStruct((B, S, LANES), BF16),
        jax.ShapeDtypeStruct((B, S, LANES), F32),
        jax.ShapeDtypeStruct((B, S, D_MODEL), F32),
        jax.ShapeDtypeStruct((B, S, D_MODEL), F32),
    )
    out_specs = (
        row_spec(2 * CONV_WIDTH), row_spec(ATTN_WIDTH), row_spec(ATTN_WIDTH),
        pl.BlockSpec((1, ATTN_WIDTH, rows), lambda b, i: (b, 0, i)),
        row_spec(ATTN_WIDTH), row_spec(LANES), row_spec(LANES), row_spec(D_MODEL), row_spec(D_MODEL),
    )
    in_specs = [row_spec(D), pl.BlockSpec((1, D), lambda b, i: (0, 0)), row_spec(LANES), row_spec(LANES)]
    in_specs += [w_spec(w) for w in ws]
    return pl.pallas_call(
        _inproj_kernel, grid=grid, in_specs=in_specs, out_specs=out_specs, out_shape=out_shape,
        compiler_params=pltpu.CompilerParams(
            dimension_semantics=("arbitrary", "arbitrary"), vmem_limit_bytes=VMEM_LIMIT_BYTES),
        name="inproj",
    )(x, g_mix, cos_t, sin_t, *ws)


def _conv_kernel(u_ref, uh_ref, gc_ref, wdw_ref, bdw_ref, lng_ref, lnb_ref, wout_ref,
                 o_ref, g_buf, s_buf):
    i = pl.program_id(1)
    rows = u_ref.shape[1]
    uh = uh_ref[0]
    gh = uh[:, :CONV_WIDTH] * _sigmoid(uh[:, CONV_WIDTH:])
    g_buf[0, 0:CONV_HALO, :] = jnp.where(i > 0, gh, 0.0)
    um = u_ref[0]
    g_buf[0, CONV_HALO:CONV_HALO + rows, :] = um[:, :CONV_WIDTH] * _sigmoid(um[:, CONV_WIDTH:])
    span = CONV_HALO + rows - SUBLANES
    for r in range(1, SUBLANES):
        for base in range(0, span, CONV_SUB):
            n = min(CONV_SUB, span - base)
            g_buf[r, base:base + n, :] = g_buf[0, pl.ds(base + r, n), :]
    first = CONV_HALO - (CONV_KERNEL - 1)
    for rr in range(rows // CONV_SUB):
        acc = jnp.zeros((CONV_SUB, CONV_WIDTH), F32)
        for j in range(CONV_KERNEL):
            shift = (first + j) % SUBLANES
            row0 = rr * CONV_SUB + (first + j) - shift
            acc = acc + wdw_ref[j:j + 1, :] * g_buf[shift, row0:row0 + CONV_SUB, :]
        c = acc + bdw_ref[...]
        mu = jnp.mean(c, axis=-1, keepdims=True)
        d = c - mu
        var = jnp.mean(d * d, axis=-1, keepdims=True)
        n = d * lax.rsqrt(var + EPS) * lng_ref[...] + lnb_ref[...]
        s_buf[rr * CONV_SUB:(rr + 1) * CONV_SUB, :] = (n * _sigmoid(n)).astype(BF16)
    y = jnp.dot(s_buf[...], wout_ref[...], preferred_element_type=F32)
    o_ref[0] = _sigmoid(gc_ref[0]) * y


def _conv_branch(u, gc, w_dw, b_dw, ln_g, ln_b, w_out):
    B, S, _ = u.shape
    rows = CONV_ROWS
    halo_per_tile = rows // CONV_HALO
    grid = (B, S // rows)
    vec = lambda n: pl.BlockSpec((1, n), lambda b, i: (0, 0))
    return pl.pallas_call(
        _conv_kernel, grid=grid,
        in_specs=[
            pl.BlockSpec((1, rows, 2 * CONV_WIDTH), lambda b, i: (b, i, 0)),
            pl.BlockSpec((1, CONV_HALO, 2 * CONV_WIDTH),
                         lambda b, i: (b, jnp.maximum(i * halo_per_tile - 1, 0), 0)),
            pl.BlockSpec((1, rows, D_MODEL), lambda b, i: (b, i, 0)),
            pl.BlockSpec((CONV_KERNEL, CONV_WIDTH), lambda b, i: (0, 0)),
            vec(CONV_WIDTH), vec(CONV_WIDTH), vec(CONV_WIDTH),
            pl.BlockSpec((CONV_WIDTH, D_MODEL), lambda b, i: (0, 0)),
        ],
        out_specs=pl.BlockSpec((1, rows, D_MODEL), lambda b, i: (b, i, 0)),
        out_shape=jax.ShapeDtypeStruct((B, S, D_MODEL), F32),
        scratch_shapes=[pltpu.VMEM((SUBLANES, CONV_HALO + rows, CONV_WIDTH), F32),
                        pltpu.VMEM((rows, CONV_WIDTH), BF16)],
        compiler_params=pltpu.CompilerParams(
            dimension_semantics=("arbitrary", "arbitrary"), vmem_limit_bytes=VMEM_LIMIT_BYTES),
        name="conv_branch",
    )(u, u, gc, w_dw, b_dw, ln_g, ln_b, w_out)


def _pair_rhs(x_bf16, scale=None):
    x = x_bf16.astype(F32)
    if scale is not None:
        x = x * scale
    lane = lax.broadcasted_iota(jnp.int32, x.shape, 1)
    lo = jnp.where(lane < HEAD_DIM, x, 0.0)
    hi = jnp.where(lane >= HEAD_DIM, x, 0.0)
    return jnp.concatenate([lo, hi], axis=0).astype(BF16)


def _fold_rows(x, op, rows=ACC_ROWS):
    n = x.shape[0] // rows
    return op(x.reshape(n, rows, x.shape[1]), axis=0)


def _finish_rows(x, op):
    y = op(x.reshape(x.shape[0] // SUBLANES, SUBLANES, x.shape[1]), axis=0)
    return op(y, axis=0, keepdims=True)


def _nt_dot(a, b):
    return lax.dot_general(a, b, (((1,), (1,)), ((), ())), preferred_element_type=F32)


def _key16_to_key32(k16):
    bits = lax.shift_left(jnp.where(k16 >= 0, k16, k16 ^ jnp.int32(0x7FFF)), 16)
    return jnp.where(bits >= 0, bits, bits ^ jnp.int32(0x7FFFFFFF))


def _key_to_f32(key):
    bits = jnp.where(key >= 0, key, key ^ jnp.int32(0x7FFFFFFF))
    return lax.bitcast_convert_type(bits, F32)


def _attn_kernel(qi_ref, q_ref, wi_ref, kk_ref, k_ref, vt_ref, o_ref,
                 sc_ref, sb_ref, xs_ref, acc_ref):
    i = pl.program_id(1)
    kc = KEY_CHUNK
    n_chunks = ((i + 1) * ATTN_Q + kc - 1) // kc
    topk = float(TOPK_MAX)
    t_idx = i * ATTN_Q + lax.broadcasted_iota(jnp.int32, (1, ATTN_Q), 1)
    s_iota = lax.broadcasted_iota(jnp.int32, (kc, ATTN_Q), 0)
    n_pairs = N_HEADS // 2

    qi = qi_ref[0]
    idx_rhs = [_pair_rhs(qi[:, p * LANES:(p + 1) * LANES]) for p in range(IDX_HEADS // 2)]
    wi_t = wi_ref[0].T

    def score_body(c, carry):
        start = pl.multiple_of(c * kc, kc)
        kk = kk_ref[0, pl.ds(start, kc), :]
        s = jnp.zeros((kc, ATTN_Q), F32)
        for p in range(IDX_HEADS // 2):
            r = _nt_dot(kk, idx_rhs[p])
            s = s + wi_t[2 * p:2 * p + 1, :] * jnp.maximum(r[:, :ATTN_Q], 0.0)
            s = s + wi_t[2 * p + 1:2 * p + 2, :] * jnp.maximum(r[:, ATTN_Q:], 0.0)
        causal = (start + s_iota) <= t_idx
        s = jnp.where(causal, s, -jnp.inf)
        sc_ref[pl.ds(start, kc), :] = s
        sb_ref[pl.ds(start, kc), :] = s.astype(BF16)
        return carry

    lax.fori_loop(0, n_chunks, score_body, 0)

    n_count = n_chunks * (kc // COUNT_ROWS)

    def count_ge(cand):
        def body(c, acc):
            start = pl.multiple_of(c * COUNT_ROWS, COUNT_ROWS)
            hit = jnp.where(sc_ref[pl.ds(start, COUNT_ROWS), :] >= cand, 1.0, 0.0)
            return acc + _fold_rows(hit, jnp.sum, COUNT_ACC_ROWS)
        acc = lax.fori_loop(0, n_count, body, jnp.zeros((COUNT_ACC_ROWS, ATTN_Q), F32))
        return _finish_rows(acc, jnp.sum)

    def count_ge_bf16(cand):
        def body(c, acc):
            start = pl.multiple_of(c * kc, kc)
            hit = jnp.where(sb_ref[pl.ds(start, kc), :] >= cand, jnp.ones((), BF16), jnp.zeros((), BF16))
            parts = [hit[r * PACK_ROWS:(r + 1) * PACK_ROWS] for r in range(kc // PACK_ROWS)]
            while len(parts) > 1:
                parts = [a + b for a, b in zip(parts[::2], parts[1::2])]
            return acc + parts[0].astype(F32)
        acc = lax.fori_loop(0, n_chunks, body, jnp.zeros((PACK_ROWS, ATTN_Q), F32))
        return _finish_rows(acc, jnp.sum)

    c0 = count_ge_bf16(jnp.zeros((1, ATTN_Q), BF16))
    k16_0 = jnp.where(c0 >= topk, jnp.int32(0), jnp.int32(KEY16_MIN))

    def bit16_body(j, k16):
        cand = k16 + lax.shift_left(jnp.int32(1), 14 - j)
        cnt = count_ge_bf16(_key_to_f32(_key16_to_key32(cand)).astype(BF16))
        return jnp.where(cnt >= topk, cand, k16)

    k16 = lax.fori_loop(0, 15, bit16_body, k16_0)
    low = _key16_to_key32(k16) - jnp.int32(2 << 16)

    def bit_body(j, key):
        cand = key + lax.shift_left(jnp.int32(1), 17 - j)
        cnt = count_ge(_key_to_f32(cand))
        return jnp.where(cnt >= topk, cand, key)

    key = lax.fori_loop(0, 18, bit_body, low)
    thr = _key_to_f32(key)
    thr = jnp.where((k16 == KEY16_MIN) | (thr != thr), -jnp.inf, thr)

    def count_both(c, carry):
        ge, gt = carry
        start = pl.multiple_of(c * COUNT_ROWS, COUNT_ROWS)
        s = sc_ref[pl.ds(start, COUNT_ROWS), :]
        ge = ge + _fold_rows(jnp.where(s >= thr, 1.0, 0.0), jnp.sum, COUNT_ACC_ROWS)
        gt = gt + _fold_rows(jnp.where(s > thr, 1.0, 0.0), jnp.sum, COUNT_ACC_ROWS)
        return ge, gt

    z_acc = jnp.zeros((COUNT_ACC_ROWS, ATTN_Q), F32)
    ge_acc, gt_acc = lax.fori_loop(0, n_count, count_both, (z_acc, z_acc))
    cnt_ge = _finish_rows(ge_acc, jnp.sum)
    cnt_gt = _finish_rows(gt_acc, jnp.sum)
    need = topk - cnt_gt
    excess = jnp.where((cnt_ge > topk) & (thr > -jnp.inf), 1.0, 0.0)
    has_excess = jnp.sum(excess) > 0.0

    @pl.when(jnp.logical_not(has_excess))
    def _():
        def body(c, carry):
            start = pl.multiple_of(c * kc, kc)
            s = sc_ref[pl.ds(start, kc), :]
            keep = (s >= thr) & ((start + s_iota) <= t_idx)
            sc_ref[pl.ds(start, kc), :] = jnp.where(keep, 0.0, MASK_VALUE)
            return carry
        lax.fori_loop(0, n_chunks, body, 0)

    @pl.when(has_excess)
    def _():
        r_i = lax.broadcasted_iota(jnp.int32, (kc, kc), 0)
        c_i = lax.broadcasted_iota(jnp.int32, (kc, kc), 1)
        strict_lower = jnp.where(c_i < r_i, 1.0, 0.0).astype(BF16)

        def body(c, before):
            start = pl.multiple_of(c * kc, kc)
            s = sc_ref[pl.ds(start, kc), :]
            tie = jnp.where(s == thr, 1.0, 0.0)
            rank = jnp.dot(strict_lower, tie.astype(BF16), preferred_element_type=F32) + before
            keep = (s > thr) | ((s == thr) & (rank < need))
            keep = keep & ((start + s_iota) <= t_idx)
            sc_ref[pl.ds(start, kc), :] = jnp.where(keep, 0.0, MASK_VALUE)
            return before + jnp.sum(tie, axis=0, keepdims=True)
        lax.fori_loop(0, n_chunks, body, jnp.zeros((1, ATTN_Q), F32))

    q = q_ref[0]
    acc_ref[...] = jnp.zeros(acc_ref.shape, F32)
    neg = jnp.full((HEAD_ACC_ROWS, ATTN_Q), -jnp.inf, F32)
    zl = jnp.zeros((HEAD_ACC_ROWS, ATTN_Q), F32)
    n_groups = N_HEADS // HEAD_GROUP
    att_rhs = [_pair_rhs(q[:, p * LANES:(p + 1) * LANES], HEAD_DIM ** -0.5) for p in range(n_pairs)]

    def qk_part(start, g, mx):
        bias = sc_ref[pl.ds(start, kc), :]
        out = []
        for p in range(g * HEAD_GROUP // 2, (g + 1) * HEAD_GROUP // 2):
            kp = k_ref[0, pl.ds(start, kc), p * LANES:(p + 1) * LANES]
            lg = _nt_dot(kp, att_rhs[p])
            for hh in range(2):
                hl = 2 * p + hh - g * HEAD_GROUP
                x = lg[:, hh * ATTN_Q:(hh + 1) * ATTN_Q] + bias
                xs_ref[g % 2, pl.ds(start, kc), hl * ATTN_Q:(hl + 1) * ATTN_Q] = x
                out.append(jnp.maximum(mx[hl], _fold_rows(x, jnp.max, HEAD_ACC_ROWS)))
        return tuple(out)

    def pv_part(start, g, m, ls):
        out = []
        for hl in range(HEAD_GROUP):
            h = g * HEAD_GROUP + hl
            pm = jnp.exp(xs_ref[g % 2, pl.ds(start, kc), hl * ATTN_Q:(hl + 1) * ATTN_Q] - m[hl])
            rows = slice(h * HEAD_DIM, (h + 1) * HEAD_DIM)
            acc_ref[rows, :] += jnp.dot(vt_ref[0, rows, pl.ds(start, kc)], pm.astype(BF16),
                                        preferred_element_type=F32)
            out.append(ls[hl] + _fold_rows(pm, jnp.sum, HEAD_ACC_ROWS))
        return tuple(out)

    m = None
    for s in range(n_groups + 1):
        def body(c, carry, s=s, m=m):
            start = pl.multiple_of(c * kc, kc)
            mx, ls = carry
            if s < n_groups:
                mx = qk_part(start, s, mx)
            if s > 0:
                ls = pv_part(start, s - 1, m, ls)
            return mx, ls

        mx, ls = lax.fori_loop(0, n_chunks, body, ((neg,) * HEAD_GROUP, (zl,) * HEAD_GROUP))
        if s > 0:
            for hl in range(HEAD_GROUP):
                h = (s - 1) * HEAD_GROUP + hl
                rows = slice(h * HEAD_DIM, (h + 1) * HEAD_DIM)
                acc_ref[rows, :] = acc_ref[rows, :] / _finish_rows(ls[hl], jnp.sum)
        if s < n_groups:
            m = [_finish_rows(mx[hl], jnp.max) for hl in range(HEAD_GROUP)]
    o_ref[0] = acc_ref[...].T.astype(BF16)


def _attention(qi, q, wi, kk, k, vt):
    B, S, _ = q.shape
    grid = (B, S // ATTN_Q)
    blk = lambda n: pl.BlockSpec((1, ATTN_Q, n), lambda b, i: (b, i, 0))
    full = lambda n: pl.BlockSpec((1, S, n), lambda b, i: (b, 0, 0), pipeline_mode=pl.Buffered(1))
    return pl.pallas_call(
        _attn_kernel, grid=grid,
        in_specs=[blk(ATTN_WIDTH), blk(ATTN_WIDTH), blk(LANES), full(LANES), full(ATTN_WIDTH),
                  pl.BlockSpec((1, ATTN_WIDTH, S), lambda b, i: (b, 0, 0), pipeline_mode=pl.Buffered(1))],
        out_specs=blk(ATTN_WIDTH),
        out_shape=jax.ShapeDtypeStruct((B, S, ATTN_WIDTH), BF16),
        scratch_shapes=[pltpu.VMEM((S, ATTN_Q), F32),
                        pltpu.VMEM((S, ATTN_Q), BF16),
                        pltpu.VMEM((2, S, HEAD_GROUP * ATTN_Q), F32),
                        pltpu.VMEM((ATTN_WIDTH, ATTN_Q), F32)],
        compiler_params=pltpu.CompilerParams(
            dimension_semantics=("arbitrary", "arbitrary"), vmem_limit_bytes=VMEM_LIMIT_BYTES),
        name="sparse_attn",
    )(qi, q, wi, kk, k, vt)


def _lane_first(mask, lane):
    return jnp.min(jnp.where(mask, lane, ROUTER_LANES), axis=-1, keepdims=True)


def _router(logits):
    lane = lax.broadcasted_iota(jnp.int32, logits.shape, 1)
    neg = -jnp.inf
    gl = jnp.where(lane < N_GROUPS, logits, neg)
    gmax = jnp.max(gl, axis=-1, keepdims=True)
    gsum = jnp.sum(jnp.exp(gl - gmax), axis=-1, keepdims=True)
    p_g = 1.0 / gsum
    gi = _lane_first(gl == gmax, lane)
    e_lo = N_GROUPS + gi * EXPERTS_PER_GROUP
    in_group = (lane >= e_lo) & (lane < e_lo + EXPERTS_PER_GROUP)
    el = jnp.where(in_group, logits, neg)
    emax = jnp.max(el, axis=-1, keepdims=True)
    ee = jnp.exp(el - emax)
    pe = ee / jnp.sum(ee, axis=-1, keepdims=True)
    pe = jnp.where(in_group, pe, -1.0)
    p1 = jnp.max(pe, axis=-1, keepdims=True)
    l1 = _lane_first(pe == p1, lane)
    pe2 = jnp.where(lane == l1, -1.0, pe)
    p2 = jnp.max(pe2, axis=-1, keepdims=True)
    l2 = _lane_first(pe2 == p2, lane)
    tot = p1 + p2
    c1 = p_g * (p1 / tot)
    c2 = p_g * (p2 / tot)
    return jnp.where(lane == l1, c1, 0.0) + jnp.where(lane == l2, c2, 0.0)


def _mix_kernel(x_ref, mconv_ref, attn_ref, ga_ref, wao_ref, wo_ref, gffn_ref, wr_ref, br_ref,
                x1_ref, h2_ref, gw_ref):
    ya = jnp.dot(attn_ref[...], wao_ref[...], preferred_element_type=F32)
    m = mconv_ref[...] + _sigmoid(ga_ref[...]) * ya
    x1 = x_ref[...] + jnp.dot(m.astype(BF16), wo_ref[...], preferred_element_type=F32)
    x1_ref[...] = x1
    h2 = _rms_rows(x1, gffn_ref[...]).astype(BF16)
    h2_ref[...] = h2
    logits = jnp.dot(h2, wr_ref[...], preferred_element_type=F32) + br_ref[...]
    gw_ref[...] = _router(logits)


def _mix(x, mconv, attn, ga, w_ao, w_o, g_ffn, w_r, b_r):
    T = x.shape[0]
    rows = MIX_ROWS
    row = lambda n: pl.BlockSpec((rows, n), lambda i: (i, 0))
    const = lambda a: pl.BlockSpec(a.shape, lambda i: (0, 0))
    return pl.pallas_call(
        _mix_kernel, grid=(T // rows,),
        in_specs=[row(D_MODEL), row(D_MODEL), row(ATTN_WIDTH), row(D_MODEL),
                  const(w_ao), const(w_o), const(g_ffn), const(w_r), const(b_r)],
        out_specs=(row(D_MODEL), row(D_MODEL), row(ROUTER_LANES)),
        out_shape=(jax.ShapeDtypeStruct((T, D_MODEL), F32),
                   jax.ShapeDtypeStruct((T, D_MODEL), BF16),
                   jax.ShapeDtypeStruct((T, ROUTER_LANES), F32)),
        compiler_params=pltpu.CompilerParams(
            dimension_semantics=("arbitrary",), vmem_limit_bytes=VMEM_LIMIT_BYTES),
        name="mix_router",
    )(x, mconv, attn, ga, w_ao, w_o, g_ffn, w_r, b_r)


def _moe_kernel(h2_ref, gw_ref, x1_ref, wg_ref, wu_ref, wd_ref, gfin_ref, o_ref, acc_ref):
    e = pl.program_id(1)

    @pl.when(e == 0)
    def _():
        acc_ref[...] = jnp.zeros(acc_ref.shape, F32)

    h2 = h2_ref[...]
    gw = gw_ref[...]
    lane = lax.broadcasted_iota(jnp.int32, gw.shape, 1)
    y = None
    for j in range(MOE_EXPERTS_PER_STEP):
        gate = jnp.dot(h2, wg_ref[j].astype(BF16), preferred_element_type=F32)
        up = jnp.dot(h2, wu_ref[j].astype(BF16), preferred_element_type=F32)
        expert_lane = e * MOE_EXPERTS_PER_STEP + j + N_GROUPS
        gwe = jnp.sum(jnp.where(lane == expert_lane, gw, 0.0), axis=-1, keepdims=True)
        hid = (gate * _sigmoid(gate)) * up * gwe
        yj = jnp.dot(hid.astype(BF16), wd_ref[j].astype(BF16), preferred_element_type=F32)
        y = yj if y is None else y + yj
    acc_ref[...] += y

    @pl.when(e == N_EXPERTS // MOE_EXPERTS_PER_STEP - 1)
    def _():
        o_ref[...] = _rms_rows(x1_ref[...] + acc_ref[...], gfin_ref[...])


def _moe(h2, gw, x1, w_gate, w_up, w_down, g_final):
    T = h2.shape[0]
    rows = MOE_ROWS
    row = lambda n: pl.BlockSpec((rows, n), lambda i, e: (i, 0))
    return pl.pallas_call(
        _moe_kernel, grid=(T // rows, N_EXPERTS // MOE_EXPERTS_PER_STEP),
        in_specs=[row(D_MODEL), row(ROUTER_LANES), row(D_MODEL),
                  pl.BlockSpec((MOE_EXPERTS_PER_STEP, D_MODEL, D_EXPERT), lambda i, e: (e, 0, 0)),
                  pl.BlockSpec((MOE_EXPERTS_PER_STEP, D_MODEL, D_EXPERT), lambda i, e: (e, 0, 0)),
                  pl.BlockSpec((MOE_EXPERTS_PER_STEP, D_EXPERT, D_MODEL), lambda i, e: (e, 0, 0)),
                  pl.BlockSpec((1, D_MODEL), lambda i, e: (0, 0))],
        out_specs=row(D_MODEL),
        out_shape=jax.ShapeDtypeStruct((T, D_MODEL), F32),
        scratch_shapes=[pltpu.VMEM((rows, D_MODEL), F32)],
        compiler_params=pltpu.CompilerParams(
            dimension_semantics=("arbitrary", "arbitrary"), vmem_limit_bytes=VMEM_LIMIT_BYTES),
        name="moe_final",
    )(h2, gw, x1, w_gate, w_up, w_down, g_final)


def _rope_tables(positions):
    half = HEAD_DIM // 2
    inv = 1.0 / (ROPE_THETA ** (jnp.arange(0, HEAD_DIM, 2, dtype=F32) / HEAD_DIM))
    ang = positions.astype(F32)[..., None] * inv
    cos, sin = jnp.cos(ang), jnp.sin(ang)
    reps = LANES // HEAD_DIM
    cos_t = jnp.tile(jnp.concatenate([cos, cos], axis=-1), (1, 1, reps))
    sin_t = jnp.tile(jnp.concatenate([-sin, sin], axis=-1), (1, 1, reps))
    del half
    return cos_t, sin_t


def _layer(x, cos_t, sin_t, g_mix, w_in, w_dw, b_dw, ln_g, ln_b, w_conv_out, w_attn_out, w_o,
           g_ffn, w_rg, b_rg, w_re, b_re, w_gate, w_up, w_down, g_final):
    B, S, D = x.shape
    T = B * S
    offs = [0]
    for n in IN_SIZES:
        offs.append(offs[-1] + n)
    col = lambda j: w_in[:, offs[j]:offs[j + 1]]
    w_ki, w_wi = col(5), col(6)
    ws = (
        col(0), col(1), col(2), col(3), col(4),
        jnp.concatenate([w_ki, w_ki, jnp.pad(w_wi, ((0, 0), (0, LANES - IDX_HEADS)))], axis=1),
        col(7), col(8),
    )
    ws = tuple(w.astype(BF16) for w in ws)
    u, q, k, vt, qi, kk, wi, gc, ga = _inproj(x, g_mix.reshape(1, D), cos_t, sin_t, ws)

    mconv = _conv_branch(u, gc, w_dw.reshape(CONV_KERNEL, CONV_WIDTH), b_dw.reshape(1, -1),
                         ln_g.reshape(1, -1), ln_b.reshape(1, -1), w_conv_out.astype(BF16))
    attn = _attention(qi, q, wi, kk, k, vt)

    n_r = N_GROUPS + N_EXPERTS
    w_r = jnp.concatenate([w_rg, w_re.reshape(D, N_EXPERTS)], axis=1)
    w_r = jnp.pad(w_r, ((0, 0), (0, ROUTER_LANES - n_r))).astype(BF16)
    b_r = jnp.pad(jnp.concatenate([b_rg, b_re.reshape(N_EXPERTS)]), (0, ROUTER_LANES - n_r)).reshape(1, -1)
    x1, h2, gw = _mix(x.reshape(T, D), mconv.reshape(T, D), attn.reshape(T, ATTN_WIDTH), ga.reshape(T, D),
                      w_attn_out.astype(BF16), w_o.astype(BF16), g_ffn.reshape(1, D), w_r, b_r)
    out = _moe(h2, gw, x1, w_gate, w_up, w_down, g_final.reshape(1, D))
    return out.reshape(B, S, D)


def kernel(x, positions, g_mix, w_in, w_dw, b_dw, ln_g, ln_b, w_conv_out, w_attn_out, w_o, g_ffn,
           w_rg, b_rg, w_re, b_re, w_gate, w_up, w_down, g_final):
    depth = g_mix.shape[0]
    assert depth == 1, "final norm is fused into the single layer's MoE call"
    cos_t, sin_t = _rope_tables(positions)
    return _layer(x, cos_t, sin_t, g_mix[0], w_in[0], w_dw[0], b_dw[0], ln_g[0], ln_b[0],
                  w_conv_out[0], w_attn_out[0], w_o[0], g_ffn[0], w_rg[0], b_rg[0], w_re[0],
                  b_re[0], w_gate[0], w_up[0], w_down[0], g_final)
```

```python
import functools

import jax
import jax.numpy as jnp
from jax import lax
from jax.experimental import pallas as pl
from jax.experimental.pallas import tpu as pltpu

F32 = jnp.float32
BF16 = jnp.bfloat16

D_MODEL = 1024
CONV_WIDTH = 512
CONV_KERNEL = 31
N_HEADS = 8
HEAD_DIM = 64
ATTN_WIDTH = N_HEADS * HEAD_DIM
IDX_HEADS = 8
IDX_DIM = 64
TOPK_MAX = 256
ROPE_THETA = 10000.0
N_GROUPS = 4
EXPERTS_PER_GROUP = 4
N_EXPERTS = N_GROUPS * EXPERTS_PER_GROUP
D_EXPERT = 256
EPS = 1e-6
IN_SIZES = (2 * CONV_WIDTH, ATTN_WIDTH, ATTN_WIDTH, ATTN_WIDTH,
            IDX_HEADS * IDX_DIM, IDX_DIM, IDX_HEADS, D_MODEL, D_MODEL)

LANES = 128
SUBLANES = 8
VMEM_LIMIT_BYTES = 56 * 1024 * 1024

PROJ_ROWS = 512
CONV_ROWS = 512
CONV_HALO = 32
CONV_SUB = 64
ATTN_Q = 256
HEAD_GROUP = 4
ONES_ROWS = 16
V_ROWS = HEAD_DIM + ONES_ROWS
KEY_CHUNK = 512
COUNT_ROWS = 256
COUNT_ACC_ROWS = 32
ACC_ROWS = 64
HEAD_ACC_ROWS = 8
MIX_ROWS = 512
MOE_ROWS = 1024
MOE_EXPERTS_PER_STEP = 2
ROUTER_LANES = 128
MASK_VALUE = -1e30
KEY16_MIN = -2 ** 15
PACK_ROWS = 16
STAGE2_BITS = 17
STAGE2_BELOW = 36000


def _rms_rows(x, g):
    ms = jnp.mean(x * x, axis=-1, keepdims=True)
    return x * lax.rsqrt(ms + EPS) * g


def _sigmoid(x):
    return jax.nn.sigmoid(x)


def _rope128(z, cos, sin, first_half):
    rot = jnp.where(first_half, pltpu.roll(z, LANES - HEAD_DIM // 2, 1),
                    pltpu.roll(z, HEAD_DIM // 2, 1))
    return z * cos + rot * sin


def _inproj_kernel(x_ref, g_ref, cos_ref, sin_ref, wu_ref, wq_ref, wk_ref, wv_ref, wqi_ref,
                   wkw_ref, wgc_ref, wga_ref,
                   u_ref, q_ref, k_ref, vt_ref, qi_ref, kk_ref, wi_ref, gc_ref, ga_ref):
    h = _rms_rows(x_ref[0], g_ref[...]).astype(BF16)
    cos = cos_ref[0]
    sin = sin_ref[0]
    lane = lax.broadcasted_iota(jnp.int32, cos.shape, 1)
    first_half = (lane % HEAD_DIM) < (HEAD_DIM // 2)

    def proj(w_ref):
        return jnp.dot(h, w_ref[...], preferred_element_type=F32)

    u_ref[0] = proj(wu_ref)
    gc_ref[0] = proj(wgc_ref)
    ga_ref[0] = proj(wga_ref)
    for w_ref, o_ref in ((wq_ref, q_ref), (wk_ref, k_ref), (wqi_ref, qi_ref)):
        z = proj(w_ref)
        for c in range(ATTN_WIDTH // LANES):
            sl = slice(c * LANES, (c + 1) * LANES)
            o_ref[0, :, sl] = _rope128(z[:, sl], cos, sin, first_half).astype(BF16)
    kw = proj(wkw_ref)
    kk_ref[0] = _rope128(kw[:, :LANES], cos, sin, first_half).astype(BF16)
    idx_scale = (IDX_HEADS ** -0.5) * (IDX_DIM ** -0.5)
    wi_ref[0] = kw[:, LANES:] * idx_scale
    vt = proj(wv_ref).T.astype(BF16)
    ones = jnp.ones((ONES_ROWS, vt.shape[1]), BF16)
    vt_ref[0] = jnp.concatenate(
        [piece for h in range(N_HEADS) for piece in (vt[h * HEAD_DIM:(h + 1) * HEAD_DIM], ones)], axis=0)


def _inproj(x, g_mix, cos_t, sin_t, ws):
    B, S, D = x.shape
    rows = PROJ_ROWS
    grid = (B, S // rows)
    row_spec = lambda n: pl.BlockSpec((1, rows, n), lambda b, i: (b, i, 0))
    w_spec = lambda w: pl.BlockSpec(w.shape, lambda b, i: (0, 0), pipeline_mode=pl.Buffered(1))
    out_shape = (
        jax.ShapeDtypeStruct((B, S, 2 * CONV_WIDTH), F32),
        jax.ShapeDtypeStruct((B, S, ATTN_WIDTH), BF16),
        jax.ShapeDtypeStruct((B, S, ATTN_WIDTH), BF16),
        jax.ShapeDtypeStruct((B, N_HEADS * V_ROWS, S), BF16),
        jax.ShapeDtypeStruct((B, S, ATTN_WIDTH), BF16),
        jax.ShapeDtypeStruct((B, S, LANES), BF16),
        jax.ShapeDtypeStruct((B, S, LANES), F32),
        jax.ShapeDtypeStruct((B, S, D_MODEL), F32),
        jax.ShapeDtypeStruct((B, S, D_MODEL), F32),
    )
    out_specs = (
        row_spec(2 * CONV_WIDTH), row_spec(ATTN_WIDTH), row_spec(ATTN_WIDTH),
        pl.BlockSpec((1, N_HEADS * V_ROWS, rows), lambda b, i: (b, 0, i)),
        row_spec(ATTN_WIDTH), row_spec(LANES), row_spec(LANES), row_spec(D_MODEL), row_spec(D_MODEL),
    )
    in_specs = [row_spec(D), pl.BlockSpec((1, D), lambda b, i: (0, 0)), row_spec(LANES), row_spec(LANES)]
    in_specs += [w_spec(w) for w in ws]
    return pl.pallas_call(
        _inproj_kernel, grid=grid, in_specs=in_specs, out_specs=out_specs, out_shape=out_shape,
        compiler_params=pltpu.CompilerParams(
            dimension_semantics=("arbitrary", "arbitrary"), vmem_limit_bytes=VMEM_LIMIT_BYTES),
        name="inproj",
    )(x, g_mix, cos_t, sin_t, *ws)


def _conv_kernel(u_ref, uh_ref, gc_ref, wdw_ref, bdw_ref, lng_ref, lnb_ref, wout_ref,
                 o_ref, g_buf, s_buf):
    i = pl.program_id(1)
    rows = u_ref.shape[1]
    uh = uh_ref[0]
    gh = uh[:, :CONV_WIDTH] * _sigmoid(uh[:, CONV_WIDTH:])
    g_buf[0, 0:CONV_HALO, :] = jnp.where(i > 0, gh, 0.0)
    um = u_ref[0]
    g_buf[0, CONV_HALO:CONV_HALO + rows, :] = um[:, :CONV_WIDTH] * _sigmoid(um[:, CONV_WIDTH:])
    span = CONV_HALO + rows - SUBLANES
    for r in range(1, SUBLANES):
        for base in range(0, span, CONV_SUB):
            n = min(CONV_SUB, span - base)
            g_buf[r, base:base + n, :] = g_buf[0, pl.ds(base + r, n), :]
    first = CONV_HALO - (CONV_KERNEL - 1)
    for rr in range(rows // CONV_SUB):
        acc = jnp.zeros((CONV_SUB, CONV_WIDTH), F32)
        for j in range(CONV_KERNEL):
            shift = (first + j) % SUBLANES
            row0 = rr * CONV_SUB + (first + j) - shift
            acc = acc + wdw_ref[j:j + 1, :] * g_buf[shift, row0:row0 + CONV_SUB, :]
        c = acc + bdw_ref[...]
        mu = jnp.mean(c, axis=-1, keepdims=True)
        d = c - mu
        var = jnp.mean(d * d, axis=-1, keepdims=True)
        n = d * lax.rsqrt(var + EPS) * lng_ref[...] + lnb_ref[...]
        s_buf[rr * CONV_SUB:(rr + 1) * CONV_SUB, :] = (n * _sigmoid(n)).astype(BF16)
    y = jnp.dot(s_buf[...], wout_ref[...], preferred_element_type=F32)
    o_ref[0] = _sigmoid(gc_ref[0]) * y


def _conv_branch(u, gc, w_dw, b_dw, ln_g, ln_b, w_out):
    B, S, _ = u.shape
    rows = CONV_ROWS
    halo_per_tile = rows // CONV_HALO
    grid = (B, S // rows)
    vec = lambda n: pl.BlockSpec((1, n), lambda b, i: (0, 0))
    return pl.pallas_call(
        _conv_kernel, grid=grid,
        in_specs=[
            pl.BlockSpec((1, rows, 2 * CONV_WIDTH), lambda b, i: (b, i, 0)),
            pl.BlockSpec((1, CONV_HALO, 2 * CONV_WIDTH),
                         lambda b, i: (b, jnp.maximum(i * halo_per_tile - 1, 0), 0)),
            pl.BlockSpec((1, rows, D_MODEL), lambda b, i: (b, i, 0)),
            pl.BlockSpec((CONV_KERNEL, CONV_WIDTH), lambda b, i: (0, 0)),
            vec(CONV_WIDTH), vec(CONV_WIDTH), vec(CONV_WIDTH),
            pl.BlockSpec((CONV_WIDTH, D_MODEL), lambda b, i: (0, 0)),
        ],
        out_specs=pl.BlockSpec((1, rows, D_MODEL), lambda b, i: (b, i, 0)),
        out_shape=jax.ShapeDtypeStruct((B, S, D_MODEL), F32),
        scratch_shapes=[pltpu.VMEM((SUBLANES, CONV_HALO + rows, CONV_WIDTH), F32),
                        pltpu.VMEM((rows, CONV_WIDTH), BF16)],
        compiler_params=pltpu.CompilerParams(
            dimension_semantics=("arbitrary", "arbitrary"), vmem_limit_bytes=VMEM_LIMIT_BYTES),
        name="conv_branch",
    )(u, u, gc, w_dw, b_dw, ln_g, ln_b, w_out)


def _pair_rhs(x_bf16, scale=None):
    x = x_bf16.astype(F32)
    if scale is not None:
        x = x * scale
    lane = lax.broadcasted_iota(jnp.int32, x.shape, 1)
    lo = jnp.where(lane < HEAD_DIM, x, 0.0)
    hi = jnp.where(lane >= HEAD_DIM, x, 0.0)
    return jnp.concatenate([lo, hi], axis=0).astype(BF16)


def _fold_rows(x, op, rows=ACC_ROWS):
    n = x.shape[0] // rows
    return op(x.reshape(n, rows, x.shape[1]), axis=0)


def _finish_rows(x, op):
    y = op(x.reshape(x.shape[0] // SUBLANES, SUBLANES, x.shape[1]), axis=0)
    return op(y, axis=0, keepdims=True)


def _nt_dot(a, b):
    return lax.dot_general(a, b, (((1,), (1,)), ((), ())), preferred_element_type=F32)


def _key16_to_key32(k16):
    bits = lax.shift_left(jnp.where(k16 >= 0, k16, k16 ^ jnp.int32(0x7FFF)), 16)
    return jnp.where(bits >= 0, bits, bits ^ jnp.int32(0x7FFFFFFF))


def _key_to_f32(key):
    bits = jnp.where(key >= 0, key, key ^ jnp.int32(0x7FFFFFFF))
    return lax.bitcast_convert_type(bits, F32)


def _attn_kernel(qi_ref, q_ref, wi_ref, kk_ref, k_ref, vt_ref, o_ref,
                 sc_ref, sb_ref, xs_ref, acc_ref, out_ref):
    i = pl.program_id(1)
    kc = KEY_CHUNK
    n_chunks = ((i + 1) * ATTN_Q + kc - 1) // kc
    topk = float(TOPK_MAX)
    t_idx = i * ATTN_Q + lax.broadcasted_iota(jnp.int32, (1, ATTN_Q), 1)
    s_iota = lax.broadcasted_iota(jnp.int32, (kc, ATTN_Q), 0)
    n_pairs = N_HEADS // 2

    qi = qi_ref[0]
    idx_rhs = [_pair_rhs(qi[:, p * LANES:(p + 1) * LANES]) for p in range(IDX_HEADS // 2)]
    wi_t = wi_ref[0].T

    def score_body(c, carry):
        start = pl.multiple_of(c * kc, kc)
        kk = kk_ref[0, pl.ds(start, kc), :]
        s = jnp.zeros((kc, ATTN_Q), F32)
        for p in range(IDX_HEADS // 2):
            r = _nt_dot(kk, idx_rhs[p])
            s = s + wi_t[2 * p:2 * p + 1, :] * jnp.maximum(r[:, :ATTN_Q], 0.0)
            s = s + wi_t[2 * p + 1:2 * p + 2, :] * jnp.maximum(r[:, ATTN_Q:], 0.0)
        causal = (start + s_iota) <= t_idx
        s = jnp.where(causal, s, -jnp.inf)
        sc_ref[pl.ds(start, kc), :] = s
        sb_ref[pl.ds(start, kc), :] = s.astype(BF16)
        return carry

    lax.fori_loop(0, n_chunks, score_body, 0)

    n_count = n_chunks * (kc // COUNT_ROWS)

    def count_ge(cand):
        def body(c, acc):
            start = pl.multiple_of(c * COUNT_ROWS, COUNT_ROWS)
            hit = jnp.where(sc_ref[pl.ds(start, COUNT_ROWS), :] >= cand, 1.0, 0.0)
            return acc + _fold_rows(hit, jnp.sum, COUNT_ACC_ROWS)
        acc = lax.fori_loop(0, n_count, body, jnp.zeros((COUNT_ACC_ROWS, ATTN_Q), F32))
        return _finish_rows(acc, jnp.sum)

    def count_ge_bf16(cand):
        def body(c, acc):
            start = pl.multiple_of(c * kc, kc)
            hit = jnp.where(sb_ref[pl.ds(start, kc), :] >= cand, jnp.ones((), BF16), jnp.zeros((), BF16))
            parts = [hit[r * PACK_ROWS:(r + 1) * PACK_ROWS] for r in range(kc // PACK_ROWS)]
            while len(parts) > 1:
                parts = [a + b for a, b in zip(parts[::2], parts[1::2])]
            return acc + parts[0].astype(F32)
        acc = lax.fori_loop(0, n_chunks, body, jnp.zeros((PACK_ROWS, ATTN_Q), F32))
        return _finish_rows(acc, jnp.sum)

    c0 = count_ge_bf16(jnp.zeros((1, ATTN_Q), BF16))
    k16_0 = jnp.where(c0 >= topk, jnp.int32(0), jnp.int32(KEY16_MIN))

    def bit16_body(j, k16):
        cand = k16 + lax.shift_left(jnp.int32(1), 14 - j)
        cnt = count_ge_bf16(_key_to_f32(_key16_to_key32(cand)).astype(BF16))
        return jnp.where(cnt >= topk, cand, k16)

    k16 = lax.fori_loop(0, 15, bit16_body, k16_0)
    low = _key16_to_key32(k16) - jnp.int32(STAGE2_BELOW)

    def bit_body(j, carry):
        key, cnt_key = carry
        cand = key + lax.shift_left(jnp.int32(1), STAGE2_BITS - 1 - j)
        cnt = count_ge(_key_to_f32(cand))
        take = cnt >= topk
        return jnp.where(take, cand, key), jnp.where(take, cnt, cnt_key)

    key, cnt_ge = lax.fori_loop(0, STAGE2_BITS, bit_body, (low, jnp.full((1, ATTN_Q), jnp.inf, F32)))
    thr = _key_to_f32(key)
    thr = jnp.where((k16 == KEY16_MIN) | (thr != thr), -jnp.inf, thr)
    excess = jnp.where((cnt_ge > topk) & (thr > -jnp.inf), 1.0, 0.0)
    has_excess = jnp.sum(excess) > 0.0

    @pl.when(jnp.logical_not(has_excess))
    def _():
        def body(c, carry):
            start = pl.multiple_of(c * kc, kc)
            s = sc_ref[pl.ds(start, kc), :]
            keep = (s >= thr) & ((start + s_iota) <= t_idx)
            sc_ref[pl.ds(start, kc), :] = jnp.where(keep, 0.0, MASK_VALUE)
            return carry
        lax.fori_loop(0, n_chunks, body, 0)

    @pl.when(has_excess)
    def _():
        r_i = lax.broadcasted_iota(jnp.int32, (kc, kc), 0)
        c_i = lax.broadcasted_iota(jnp.int32, (kc, kc), 1)
        strict_lower = jnp.where(c_i < r_i, 1.0, 0.0).astype(BF16)

        def count_gt(c, acc):
            start = pl.multiple_of(c * COUNT_ROWS, COUNT_ROWS)
            hit = jnp.where(sc_ref[pl.ds(start, COUNT_ROWS), :] > thr, 1.0, 0.0)
            return acc + _fold_rows(hit, jnp.sum, COUNT_ACC_ROWS)

        gt_acc = lax.fori_loop(0, n_count, count_gt, jnp.zeros((COUNT_ACC_ROWS, ATTN_Q), F32))
        need = topk - _finish_rows(gt_acc, jnp.sum)

        def body(c, before):
            start = pl.multiple_of(c * kc, kc)
            s = sc_ref[pl.ds(start, kc), :]
            tie = jnp.where(s == thr, 1.0, 0.0)
            rank = jnp.dot(strict_lower, tie.astype(BF16), preferred_element_type=F32) + before
            keep = (s > thr) | ((s == thr) & (rank < need))
            keep = keep & ((start + s_iota) <= t_idx)
            sc_ref[pl.ds(start, kc), :] = jnp.where(keep, 0.0, MASK_VALUE)
            return before + jnp.sum(tie, axis=0, keepdims=True)
        lax.fori_loop(0, n_chunks, body, jnp.zeros((1, ATTN_Q), F32))

    q = q_ref[0]
    acc_ref[...] = jnp.zeros(acc_ref.shape, F32)
    neg = jnp.full((HEAD_ACC_ROWS, ATTN_Q), -jnp.inf, F32)
    n_groups = N_HEADS // HEAD_GROUP
    att_rhs = [_pair_rhs(q[:, p * LANES:(p + 1) * LANES], HEAD_DIM ** -0.5) for p in range(n_pairs)]

    def qk_part(start, g, mx):
        bias = sc_ref[pl.ds(start, kc), :]
        out = []
        for p in range(g * HEAD_GROUP // 2, (g + 1) * HEAD_GROUP // 2):
            kp = k_ref[0, pl.ds(start, kc), p * LANES:(p + 1) * LANES]
            lg = _nt_dot(kp, att_rhs[p])
            for hh in range(2):
                hl = 2 * p + hh - g * HEAD_GROUP
                x = lg[:, hh * ATTN_Q:(hh + 1) * ATTN_Q] + bias
                xs_ref[g % 2, pl.ds(start, kc), hl * ATTN_Q:(hl + 1) * ATTN_Q] = x
                out.append(jnp.maximum(mx[hl], _fold_rows(x, jnp.max, HEAD_ACC_ROWS)))
        return tuple(out)

    def pv_part(start, g, m):
        for hl in range(HEAD_GROUP):
            h = g * HEAD_GROUP + hl
            pm = jnp.exp(xs_ref[g % 2, pl.ds(start, kc), hl * ATTN_Q:(hl + 1) * ATTN_Q] - m[hl])
            rows = slice(h * V_ROWS, (h + 1) * V_ROWS)
            acc_ref[rows, :] += jnp.dot(vt_ref[0, rows, pl.ds(start, kc)], pm.astype(BF16),
                                        preferred_element_type=F32)

    m = None
    for s in range(n_groups + 1):
        def body(c, mx, s=s, m=m):
            start = pl.multiple_of(c * kc, kc)
            if s < n_groups:
                mx = qk_part(start, s, mx)
            if s > 0:
                pv_part(start, s - 1, m)
            return mx

        mx = lax.fori_loop(0, n_chunks, body, (neg,) * HEAD_GROUP)
        if s < n_groups:
            m = [_finish_rows(mx[hl], jnp.max) for hl in range(HEAD_GROUP)]
    for h in range(N_HEADS):
        pv = acc_ref[h * V_ROWS:h * V_ROWS + HEAD_DIM, :]
        denom = acc_ref[h * V_ROWS + HEAD_DIM:h * V_ROWS + HEAD_DIM + 1, :]
        out_ref[h * HEAD_DIM:(h + 1) * HEAD_DIM, :] = pv / denom
    o_ref[0] = out_ref[...].T.astype(BF16)


def _attention(qi, q, wi, kk, k, vt):
    B, S, _ = q.shape
    grid = (B, S // ATTN_Q)
    blk = lambda n: pl.BlockSpec((1, ATTN_Q, n), lambda b, i: (b, i, 0))
    full = lambda n: pl.BlockSpec((1, S, n), lambda b, i: (b, 0, 0), pipeline_mode=pl.Buffered(1))
    return pl.pallas_call(
        _attn_kernel, grid=grid,
        in_specs=[blk(ATTN_WIDTH), blk(ATTN_WIDTH), blk(LANES), full(LANES), full(ATTN_WIDTH),
                  pl.BlockSpec((1, N_HEADS * V_ROWS, S), lambda b, i: (b, 0, 0), pipeline_mode=pl.Buffered(1))],
        out_specs=blk(ATTN_WIDTH),
        out_shape=jax.ShapeDtypeStruct((B, S, ATTN_WIDTH), BF16),
        scratch_shapes=[pltpu.VMEM((S, ATTN_Q), F32),
                        pltpu.VMEM((S, ATTN_Q), BF16),
                        pltpu.VMEM((2, S, HEAD_GROUP * ATTN_Q), F32),
                        pltpu.VMEM((N_HEADS * V_ROWS, ATTN_Q), F32),
                        pltpu.VMEM((ATTN_WIDTH, ATTN_Q), F32)],
        compiler_params=pltpu.CompilerParams(
            dimension_semantics=("arbitrary", "arbitrary"), vmem_limit_bytes=VMEM_LIMIT_BYTES),
        name="sparse_attn",
    )(qi, q, wi, kk, k, vt)


def _lane_first(mask, lane):
    return jnp.min(jnp.where(mask, lane, ROUTER_LANES), axis=-1, keepdims=True)


def _router(logits):
    lane = lax.broadcasted_iota(jnp.int32, logits.shape, 1)
    neg = -jnp.inf
    gl = jnp.where(lane < N_GROUPS, logits, neg)
    gmax = jnp.max(gl, axis=-1, keepdims=True)
    gsum = jnp.sum(jnp.exp(gl - gmax), axis=-1, keepdims=True)
    p_g = 1.0 / gsum
    gi = _lane_first(gl == gmax, lane)
    e_lo = N_GROUPS + gi * EXPERTS_PER_GROUP
    in_group = (lane >= e_lo) & (lane < e_lo + EXPERTS_PER_GROUP)
    el = jnp.where(in_group, logits, neg)
    emax = jnp.max(el, axis=-1, keepdims=True)
    ee = jnp.exp(el - emax)
    pe = ee / jnp.sum(ee, axis=-1, keepdims=True)
    pe = jnp.where(in_group, pe, -1.0)
    p1 = jnp.max(pe, axis=-1, keepdims=True)
    l1 = _lane_first(pe == p1, lane)
    pe2 = jnp.where(lane == l1, -1.0, pe)
    p2 = jnp.max(pe2, axis=-1, keepdims=True)
    l2 = _lane_first(pe2 == p2, lane)
    tot = p1 + p2
    c1 = p_g * (p1 / tot)
    c2 = p_g * (p2 / tot)
    return jnp.where(lane == l1, c1, 0.0) + jnp.where(lane == l2, c2, 0.0)


def _mix_kernel(x_ref, mconv_ref, attn_ref, ga_ref, wao_ref, wo_ref, gffn_ref, wr_ref, br_ref,
                x1_ref, h2_ref, gw_ref):
    ya = jnp.dot(attn_ref[...], wao_ref[...], preferred_element_type=F32)
    m = mconv_ref[...] + _sigmoid(ga_ref[...]) * ya
    x1 = x_ref[...] + jnp.dot(m.astype(BF16), wo_ref[...], preferred_element_type=F32)
    x1_ref[...] = x1
    h2 = _rms_rows(x1, gffn_ref[...]).astype(BF16)
    h2_ref[...] = h2
    logits = jnp.dot(h2, wr_ref[...], preferred_element_type=F32) + br_ref[...]
    gw_ref[...] = _router(logits)


def _mix(x, mconv, attn, ga, w_ao, w_o, g_ffn, w_r, b_r):
    T = x.shape[0]
    rows = MIX_ROWS
    row = lambda n: pl.BlockSpec((rows, n), lambda i: (i, 0))
    const = lambda a: pl.BlockSpec(a.shape, lambda i: (0, 0))
    return pl.pallas_call(
        _mix_kernel, grid=(T // rows,),
        in_specs=[row(D_MODEL), row(D_MODEL), row(ATTN_WIDTH), row(D_MODEL),
                  const(w_ao), const(w_o), const(g_ffn), const(w_r), const(b_r)],
        out_specs=(row(D_MODEL), row(D_MODEL), row(ROUTER_LANES)),
        out_shape=(jax.ShapeDtypeStruct((T, D_MODEL), F32),
                   jax.ShapeDtypeStruct((T, D_MODEL), BF16),
                   jax.ShapeDtypeStruct((T, ROUTER_LANES), F32)),
        compiler_params=pltpu.CompilerParams(
            dimension_semantics=("arbitrary",), vmem_limit_bytes=VMEM_LIMIT_BYTES),
        name="mix_router",
    )(x, mconv, attn, ga, w_ao, w_o, g_ffn, w_r, b_r)


def _moe_kernel(h2_ref, gw_ref, x1_ref, wg_ref, wu_ref, wd_ref, gfin_ref, o_ref, acc_ref):
    e = pl.program_id(1)

    @pl.when(e == 0)
    def _():
        acc_ref[...] = jnp.zeros(acc_ref.shape, F32)

    h2 = h2_ref[...]
    gw = gw_ref[...]
    lane = lax.broadcasted_iota(jnp.int32, gw.shape, 1)
    y = None
    for j in range(MOE_EXPERTS_PER_STEP):
        gate = jnp.dot(h2, wg_ref[j].astype(BF16), preferred_element_type=F32)
        up = jnp.dot(h2, wu_ref[j].astype(BF16), preferred_element_type=F32)
        expert_lane = e * MOE_EXPERTS_PER_STEP + j + N_GROUPS
        gwe = jnp.sum(jnp.where(lane == expert_lane, gw, 0.0), axis=-1, keepdims=True)
        hid = (gate * _sigmoid(gate)) * up * gwe
        yj = jnp.dot(hid.astype(BF16), wd_ref[j].astype(BF16), preferred_element_type=F32)
        y = yj if y is None else y + yj
    acc_ref[...] += y

    @pl.when(e == N_EXPERTS // MOE_EXPERTS_PER_STEP - 1)
    def _():
        o_ref[...] = _rms_rows(x1_ref[...] + acc_ref[...], gfin_ref[...])


def _moe(h2, gw, x1, w_gate, w_up, w_down, g_final):
    T = h2.shape[0]
    rows = MOE_ROWS
    row = lambda n: pl.BlockSpec((rows, n), lambda i, e: (i, 0))
    return pl.pallas_call(
        _moe_kernel, grid=(T // rows, N_EXPERTS // MOE_EXPERTS_PER_STEP),
        in_specs=[row(D_MODEL), row(ROUTER_LANES), row(D_MODEL),
                  pl.BlockSpec((MOE_EXPERTS_PER_STEP, D_MODEL, D_EXPERT), lambda i, e: (e, 0, 0)),
                  pl.BlockSpec((MOE_EXPERTS_PER_STEP, D_MODEL, D_EXPERT), lambda i, e: (e, 0, 0)),
                  pl.BlockSpec((MOE_EXPERTS_PER_STEP, D_EXPERT, D_MODEL), lambda i, e: (e, 0, 0)),
                  pl.BlockSpec((1, D_MODEL), lambda i, e: (0, 0))],
        out_specs=row(D_MODEL),
        out_shape=jax.ShapeDtypeStruct((T, D_MODEL), F32),
        scratch_shapes=[pltpu.VMEM((rows, D_MODEL), F32)],
        compiler_params=pltpu.CompilerParams(
            dimension_semantics=("arbitrary", "arbitrary"), vmem_limit_bytes=VMEM_LIMIT_BYTES),
        name="moe_final",
    )(h2, gw, x1, w_gate, w_up, w_down, g_final)


def _rope_tables(positions):
    half = HEAD_DIM // 2
    inv = 1.0 / (ROPE_THETA ** (jnp.arange(0, HEAD_DIM, 2, dtype=F32) / HEAD_DIM))
    lane = jnp.arange(LANES)
    inv_t = inv[lane % half]
    sign = jnp.where((lane % HEAD_DIM) < half, -1.0, 1.0).astype(F32)
    ang = positions.astype(F32)[..., None] * inv_t
    return jnp.cos(ang), jnp.sin(ang) * sign


def _layer(x, cos_t, sin_t, g_mix, w_in, w_dw, b_dw, ln_g, ln_b, w_conv_out, w_attn_out, w_o,
           g_ffn, w_rg, b_rg, w_re, b_re, w_gate, w_up, w_down, g_final):
    B, S, D = x.shape
    T = B * S
    offs = [0]
    for n in IN_SIZES:
        offs.append(offs[-1] + n)
    col = lambda j: w_in[:, offs[j]:offs[j + 1]]
    w_ki, w_wi = col(5), col(6)
    ws = (
        col(0), col(1), col(2), col(3), col(4),
        jnp.concatenate([w_ki, w_ki, jnp.pad(w_wi, ((0, 0), (0, LANES - IDX_HEADS)))], axis=1),
        col(7), col(8),
    )
    ws = tuple(w.astype(BF16) for w in ws)
    u, q, k, vt, qi, kk, wi, gc, ga = _inproj(x, g_mix.reshape(1, D), cos_t, sin_t, ws)

    mconv = _conv_branch(u, gc, w_dw.reshape(CONV_KERNEL, CONV_WIDTH), b_dw.reshape(1, -1),
                         ln_g.reshape(1, -1), ln_b.reshape(1, -1), w_conv_out.astype(BF16))
    attn = _attention(qi, q, wi, kk, k, vt)

    n_r = N_GROUPS + N_EXPERTS
    w_r = jnp.concatenate([w_rg, w_re.reshape(D, N_EXPERTS)], axis=1)
    w_r = jnp.pad(w_r, ((0, 0), (0, ROUTER_LANES - n_r))).astype(BF16)
    b_r = jnp.pad(jnp.concatenate([b_rg, b_re.reshape(N_EXPERTS)]), (0, ROUTER_LANES - n_r)).reshape(1, -1)
    x1, h2, gw = _mix(x.reshape(T, D), mconv.reshape(T, D), attn.reshape(T, ATTN_WIDTH), ga.reshape(T, D),
                      w_attn_out.astype(BF16), w_o.astype(BF16), g_ffn.reshape(1, D), w_r, b_r)
    out = _moe(h2, gw, x1, w_gate, w_up, w_down, g_final.reshape(1, D))
    return out.reshape(B, S, D)


def kernel(x, positions, g_mix, w_in, w_dw, b_dw, ln_g, ln_b, w_conv_out, w_attn_out, w_o, g_ffn,
           w_rg, b_rg, w_re, b_re, w_gate, w_up, w_down, g_final):
    depth = g_mix.shape[0]
    assert depth == 1, "final norm is fused into the single layer's MoE call"
    cos_t, sin_t = _rope_tables(positions)
    return _layer(x, cos_t, sin_t, g_mix[0], w_in[0], w_dw[0], b_dw[0], ln_g[0], ln_b[0],
                  w_conv_out[0], w_attn_out[0], w_o[0], g_ffn[0], w_rg[0], b_rg[0], w_re[0],
                  b_re[0], w_gate[0], w_up[0], w_down[0], g_final)
```

```python
import functools

import jax
import jax.numpy as jnp
from jax import lax
from jax.experimental import pallas as pl
from jax.experimental.pallas import tpu as pltpu

F32 = jnp.float32
BF16 = jnp.bfloat16

D_MODEL = 1024
CONV_WIDTH = 512
CONV_KERNEL = 31
N_HEADS = 8
HEAD_DIM = 64
ATTN_WIDTH = N_HEADS * HEAD_DIM
IDX_HEADS = 8
IDX_DIM = 64
TOPK_MAX = 256
ROPE_THETA = 10000.0
N_GROUPS = 4
EXPERTS_PER_GROUP = 4
N_EXPERTS = N_GROUPS * EXPERTS_PER_GROUP
D_EXPERT = 256
EPS = 1e-6
IN_SIZES = (2 * CONV_WIDTH, ATTN_WIDTH, ATTN_WIDTH, ATTN_WIDTH,
            IDX_HEADS * IDX_DIM, IDX_DIM, IDX_HEADS, D_MODEL, D_MODEL)

LANES = 128
SUBLANES = 8
VMEM_LIMIT_BYTES = 56 * 1024 * 1024

PROJ_ROWS = 512
CONV_ROWS = 512
CONV_HALO = 32
CONV_SUB = 64
ATTN_Q = 256
HEAD_GROUP = 4
ONES_ROWS = 16
V_ROWS = HEAD_DIM + ONES_ROWS
KEY_CHUNK = 512
COUNT_ROWS = 256
COUNT_ACC_ROWS = 32
ACC_ROWS = 64
HEAD_ACC_ROWS = 8
MIX_ROWS = 1024
MOE_ROWS = 1024
MOE_EXPERTS_PER_STEP = 2
ROUTER_LANES = 128
MASK_VALUE = -1e30
KEY16_MIN = -2 ** 15
PACK_ROWS = 16
STAGE2_BITS = 17
STAGE2_BELOW = 36000


def _rms_rows(x, g):
    ms = jnp.mean(x * x, axis=-1, keepdims=True)
    return x * lax.rsqrt(ms + EPS) * g


def _sigmoid(x):
    return jax.nn.sigmoid(x)


def _rope128(z, cos, sin, first_half):
    rot = jnp.where(first_half, pltpu.roll(z, LANES - HEAD_DIM // 2, 1),
                    pltpu.roll(z, HEAD_DIM // 2, 1))
    return z * cos + rot * sin


def _inproj_kernel(x_ref, g_ref, cos_ref, sin_ref, wu_ref, wq_ref, wk_ref, wv_ref, wqi_ref,
                   wkw_ref, wgc_ref, wga_ref,
                   u_ref, q_ref, k_ref, vt_ref, qi_ref, kk_ref, wi_ref, gc_ref, ga_ref):
    h = _rms_rows(x_ref[0], g_ref[...]).astype(BF16)
    c32 = cos_ref[0]
    s32 = sin_ref[0]
    cos = jnp.concatenate([c32] * (LANES // c32.shape[1]), axis=1)
    sin = jnp.concatenate([-s32, s32] * (LANES // (2 * s32.shape[1])), axis=1)
    lane = lax.broadcasted_iota(jnp.int32, cos.shape, 1)
    first_half = (lane % HEAD_DIM) < (HEAD_DIM // 2)

    def proj(w_ref):
        return jnp.dot(h, w_ref[...], preferred_element_type=F32)

    u_ref[0] = proj(wu_ref)
    gc_ref[0] = proj(wgc_ref)
    ga_ref[0] = proj(wga_ref)
    for w_ref, o_ref in ((wq_ref, q_ref), (wk_ref, k_ref), (wqi_ref, qi_ref)):
        z = proj(w_ref)
        for c in range(ATTN_WIDTH // LANES):
            sl = slice(c * LANES, (c + 1) * LANES)
            o_ref[0, :, sl] = _rope128(z[:, sl], cos, sin, first_half).astype(BF16)
    kw = proj(wkw_ref)
    kk_ref[0] = _rope128(kw[:, :LANES], cos, sin, first_half).astype(BF16)
    idx_scale = (IDX_HEADS ** -0.5) * (IDX_DIM ** -0.5)
    wi_ref[0] = kw[:, LANES:] * idx_scale
    vt = proj(wv_ref).T.astype(BF16)
    ones = jnp.ones((ONES_ROWS, vt.shape[1]), BF16)
    vt_ref[0] = jnp.concatenate(
        [piece for hd in range(N_HEADS) for piece in (vt[hd * HEAD_DIM:(hd + 1) * HEAD_DIM], ones)], axis=0)


def _inproj(x, g_mix, cos_t, sin_t, ws):
    B, S, D = x.shape
    rows = PROJ_ROWS
    grid = (B, S // rows)
    row_spec = lambda n: pl.BlockSpec((1, rows, n), lambda b, i: (b, i, 0))
    w_spec = lambda w: pl.BlockSpec(w.shape, lambda b, i: (0, 0), pipeline_mode=pl.Buffered(1))
    out_shape = (
        jax.ShapeDtypeStruct((B, S, 2 * CONV_WIDTH), F32),
        jax.ShapeDtypeStruct((B, S, ATTN_WIDTH), BF16),
        jax.ShapeDtypeStruct((B, S, ATTN_WIDTH), BF16),
        jax.ShapeDtypeStruct((B, N_HEADS * V_ROWS, S), BF16),
        jax.ShapeDtypeStruct((B, S, ATTN_WIDTH), BF16),
        jax.ShapeDtypeStruct((B, S, LANES), BF16),
        jax.ShapeDtypeStruct((B, S, LANES), F32),
        jax.ShapeDtypeStruct((B, S, D_MODEL), F32),
        jax.ShapeDtypeStruct((B, S, D_MODEL), F32),
    )
    out_specs = (
        row_spec(2 * CONV_WIDTH), row_spec(ATTN_WIDTH), row_spec(ATTN_WIDTH),
        pl.BlockSpec((1, N_HEADS * V_ROWS, rows), lambda b, i: (b, 0, i)),
        row_spec(ATTN_WIDTH), row_spec(LANES), row_spec(LANES), row_spec(D_MODEL), row_spec(D_MODEL),
    )
    half = HEAD_DIM // 2
    in_specs = [row_spec(D), pl.BlockSpec((1, D), lambda b, i: (0, 0)), row_spec(half), row_spec(half)]
    in_specs += [w_spec(w) for w in ws]
    return pl.pallas_call(
        _inproj_kernel, grid=grid, in_specs=in_specs, out_specs=out_specs, out_shape=out_shape,
        compiler_params=pltpu.CompilerParams(
            dimension_semantics=("arbitrary", "arbitrary"), vmem_limit_bytes=VMEM_LIMIT_BYTES),
        name="inproj",
    )(x, g_mix, cos_t, sin_t, *ws)


def _conv_kernel(u_ref, uh_ref, gc_ref, wdw_ref, bdw_ref, lng_ref, lnb_ref, wout_ref,
                 o_ref, g_buf, s_buf):
    i = pl.program_id(1)
    rows = u_ref.shape[1]
    uh = uh_ref[0]
    gh = uh[:, :CONV_WIDTH] * _sigmoid(uh[:, CONV_WIDTH:])
    g_buf[0, 0:CONV_HALO, :] = jnp.where(i > 0, gh, 0.0)
    um = u_ref[0]
    g_buf[0, CONV_HALO:CONV_HALO + rows, :] = um[:, :CONV_WIDTH] * _sigmoid(um[:, CONV_WIDTH:])
    span = CONV_HALO + rows - SUBLANES
    for r in range(1, SUBLANES):
        for base in range(0, span, CONV_SUB):
            n = min(CONV_SUB, span - base)
            g_buf[r, base:base + n, :] = g_buf[0, pl.ds(base + r, n), :]
    first = CONV_HALO - (CONV_KERNEL - 1)
    for rr in range(rows // CONV_SUB):
        acc = jnp.zeros((CONV_SUB, CONV_WIDTH), F32)
        for t in range(CONV_KERNEL):
            shift = (first + t) % SUBLANES
            row0 = rr * CONV_SUB + (first + t) - shift
            acc = acc + wdw_ref[t:t + 1, :] * g_buf[shift, row0:row0 + CONV_SUB, :]
        c = acc + bdw_ref[...]
        mu = jnp.mean(c, axis=-1, keepdims=True)
        d = c - mu
        var = jnp.mean(d * d, axis=-1, keepdims=True)
        n = d * lax.rsqrt(var + EPS) * lng_ref[...] + lnb_ref[...]
        s_buf[rr * CONV_SUB:(rr + 1) * CONV_SUB, :] = (n * _sigmoid(n)).astype(BF16)
    y = jnp.dot(s_buf[...], wout_ref[...], preferred_element_type=F32)
    o_ref[0] = _sigmoid(gc_ref[0]) * y


def _conv_branch(u, gc, w_dw, b_dw, ln_g, ln_b, w_out):
    B, S, _ = u.shape
    rows = CONV_ROWS
    halo_per_tile = rows // CONV_HALO
    grid = (B, S // rows)
    vec = lambda n: pl.BlockSpec((1, n), lambda b, i: (0, 0))
    return pl.pallas_call(
        _conv_kernel, grid=grid,
        in_specs=[
            pl.BlockSpec((1, rows, 2 * CONV_WIDTH), lambda b, i: (b, i, 0)),
            pl.BlockSpec((1, CONV_HALO, 2 * CONV_WIDTH),
                         lambda b, i: (b, jnp.maximum(i * halo_per_tile - 1, 0), 0)),
            pl.BlockSpec((1, rows, D_MODEL), lambda b, i: (b, i, 0)),
            pl.BlockSpec((CONV_KERNEL, CONV_WIDTH), lambda b, i: (0, 0)),
            vec(CONV_WIDTH), vec(CONV_WIDTH), vec(CONV_WIDTH),
            pl.BlockSpec((CONV_WIDTH, D_MODEL), lambda b, i: (0, 0)),
        ],
        out_specs=pl.BlockSpec((1, rows, D_MODEL), lambda b, i: (b, i, 0)),
        out_shape=jax.ShapeDtypeStruct((B, S, D_MODEL), F32),
        scratch_shapes=[pltpu.VMEM((SUBLANES, CONV_HALO + rows, CONV_WIDTH), F32),
                        pltpu.VMEM((rows, CONV_WIDTH), BF16)],
        compiler_params=pltpu.CompilerParams(
            dimension_semantics=("arbitrary", "arbitrary"), vmem_limit_bytes=VMEM_LIMIT_BYTES),
        name="conv_branch",
    )(u, u, gc, w_dw, b_dw, ln_g, ln_b, w_out)


def _pair_rhs(x_bf16, scale=None):
    x = x_bf16.astype(F32)
    if scale is not None:
        x = x * scale
    lane = lax.broadcasted_iota(jnp.int32, x.shape, 1)
    lo = jnp.where(lane < HEAD_DIM, x, 0.0)
    hi = jnp.where(lane >= HEAD_DIM, x, 0.0)
    return jnp.concatenate([lo, hi], axis=0).astype(BF16)


def _fold_rows(x, op, rows=ACC_ROWS):
    n = x.shape[0] // rows
    return op(x.reshape(n, rows, x.shape[1]), axis=0)


def _finish_rows(x, op):
    y = op(x.reshape(x.shape[0] // SUBLANES, SUBLANES, x.shape[1]), axis=0)
    return op(y, axis=0, keepdims=True)


def _nt_dot(a, b):
    return lax.dot_general(a, b, (((1,), (1,)), ((), ())), preferred_element_type=F32)


def _key16_to_key32(k16):
    bits = lax.shift_left(jnp.where(k16 >= 0, k16, k16 ^ jnp.int32(0x7FFF)), 16)
    return jnp.where(bits >= 0, bits, bits ^ jnp.int32(0x7FFFFFFF))


def _key_to_f32(key):
    bits = jnp.where(key >= 0, key, key ^ jnp.int32(0x7FFFFFFF))
    return lax.bitcast_convert_type(bits, F32)


def _attn_kernel(qi_ref, q_ref, wi_ref, kk_ref, k_ref, vt_ref, o_ref,
                 sc_ref, sb_ref, xs_ref, acc_ref, out_ref):
    i = pl.program_id(1)
    kc = KEY_CHUNK
    n_chunks = ((i + 1) * ATTN_Q + kc - 1) // kc
    topk = float(TOPK_MAX)
    t_idx = i * ATTN_Q + lax.broadcasted_iota(jnp.int32, (1, ATTN_Q), 1)
    s_iota = lax.broadcasted_iota(jnp.int32, (kc, ATTN_Q), 0)
    n_pairs = N_HEADS // 2

    qi = qi_ref[0]
    idx_rhs = [_pair_rhs(qi[:, p * LANES:(p + 1) * LANES]) for p in range(IDX_HEADS // 2)]
    wi_t = wi_ref[0].T

    def score_body(c, carry):
        start = pl.multiple_of(c * kc, kc)
        kk = kk_ref[0, pl.ds(start, kc), :]
        s = jnp.zeros((kc, ATTN_Q), F32)
        for p in range(IDX_HEADS // 2):
            r = _nt_dot(kk, idx_rhs[p])
            s = s + wi_t[2 * p:2 * p + 1, :] * jnp.maximum(r[:, :ATTN_Q], 0.0)
            s = s + wi_t[2 * p + 1:2 * p + 2, :] * jnp.maximum(r[:, ATTN_Q:], 0.0)
        causal = (start + s_iota) <= t_idx
        s = jnp.where(causal, s, -jnp.inf)
        sc_ref[pl.ds(start, kc), :] = s
        sb_ref[pl.ds(start, kc), :] = s.astype(BF16)
        return carry

    lax.fori_loop(0, n_chunks, score_body, 0)

    n_count = n_chunks * (kc // COUNT_ROWS)

    def count_ge(cand):
        def body(c, acc):
            start = pl.multiple_of(c * COUNT_ROWS, COUNT_ROWS)
            hit = jnp.where(sc_ref[pl.ds(start, COUNT_ROWS), :] >= cand, 1.0, 0.0)
            return acc + _fold_rows(hit, jnp.sum, COUNT_ACC_ROWS)
        acc = lax.fori_loop(0, n_count, body, jnp.zeros((COUNT_ACC_ROWS, ATTN_Q), F32))
        return _finish_rows(acc, jnp.sum)

    def count_ge_bf16(cand):
        def body(c, acc):
            start = pl.multiple_of(c * kc, kc)
            hit = jnp.where(sb_ref[pl.ds(start, kc), :] >= cand, jnp.ones((), BF16), jnp.zeros((), BF16))
            parts = [hit[r * PACK_ROWS:(r + 1) * PACK_ROWS] for r in range(kc // PACK_ROWS)]
            while len(parts) > 1:
                parts = [a + b for a, b in zip(parts[::2], parts[1::2])]
            return acc + parts[0].astype(F32)
        acc = lax.fori_loop(0, n_chunks, body, jnp.zeros((PACK_ROWS, ATTN_Q), F32))
        return _finish_rows(acc, jnp.sum)

    c0 = count_ge_bf16(jnp.zeros((1, ATTN_Q), BF16))
    k16_0 = jnp.where(c0 >= topk, jnp.int32(0), jnp.int32(KEY16_MIN))

    def bit16_body(j, k16):
        cand = k16 + lax.shift_left(jnp.int32(1), 14 - j)
        cnt = count_ge_bf16(_key_to_f32(_key16_to_key32(cand)).astype(BF16))
        return jnp.where(cnt >= topk, cand, k16)

    k16 = lax.fori_loop(0, 15, bit16_body, k16_0)
    low = _key16_to_key32(k16) - jnp.int32(STAGE2_BELOW)

    def bit_body(j, carry):
        key, cnt_key = carry
        cand = key + lax.shift_left(jnp.int32(1), STAGE2_BITS - 1 - j)
        cnt = count_ge(_key_to_f32(cand))
        take = cnt >= topk
        return jnp.where(take, cand, key), jnp.where(take, cnt, cnt_key)

    key, cnt_ge = lax.fori_loop(0, STAGE2_BITS, bit_body, (low, jnp.full((1, ATTN_Q), jnp.inf, F32)))
    thr = _key_to_f32(key)
    thr = jnp.where((k16 == KEY16_MIN) | (thr != thr), -jnp.inf, thr)
    excess = jnp.where((cnt_ge > topk) & (thr > -jnp.inf), 1.0, 0.0)
    has_excess = jnp.sum(excess) > 0.0

    @pl.when(jnp.logical_not(has_excess))
    def _():
        def body(c, carry):
            start = pl.multiple_of(c * kc, kc)
            s = sc_ref[pl.ds(start, kc), :]
            keep = (s >= thr) & ((start + s_iota) <= t_idx)
            sc_ref[pl.ds(start, kc), :] = jnp.where(keep, 0.0, MASK_VALUE)
            return carry
        lax.fori_loop(0, n_chunks, body, 0)

    @pl.when(has_excess)
    def _():
        r_i = lax.broadcasted_iota(jnp.int32, (kc, kc), 0)
        c_i = lax.broadcasted_iota(jnp.int32, (kc, kc), 1)
        strict_lower = jnp.where(c_i < r_i, 1.0, 0.0).astype(BF16)

        def count_gt(c, acc):
            start = pl.multiple_of(c * COUNT_ROWS, COUNT_ROWS)
            hit = jnp.where(sc_ref[pl.ds(start, COUNT_ROWS), :] > thr, 1.0, 0.0)
            return acc + _fold_rows(hit, jnp.sum, COUNT_ACC_ROWS)

        gt_acc = lax.fori_loop(0, n_count, count_gt, jnp.zeros((COUNT_ACC_ROWS, ATTN_Q), F32))
        need = topk - _finish_rows(gt_acc, jnp.sum)

        def body(c, before):
            start = pl.multiple_of(c * kc, kc)
            s = sc_ref[pl.ds(start, kc), :]
            tie = jnp.where(s == thr, 1.0, 0.0)
            rank = jnp.dot(strict_lower, tie.astype(BF16), preferred_element_type=F32) + before
            keep = (s > thr) | ((s == thr) & (rank < need))
            keep = keep & ((start + s_iota) <= t_idx)
            sc_ref[pl.ds(start, kc), :] = jnp.where(keep, 0.0, MASK_VALUE)
            return before + jnp.sum(tie, axis=0, keepdims=True)
        lax.fori_loop(0, n_chunks, body, jnp.zeros((1, ATTN_Q), F32))

    q = q_ref[0]
    acc_ref[...] = jnp.zeros(acc_ref.shape, F32)
    neg = jnp.full((HEAD_ACC_ROWS, ATTN_Q), -jnp.inf, F32)
    n_groups = N_HEADS // HEAD_GROUP
    att_rhs = [_pair_rhs(q[:, p * LANES:(p + 1) * LANES], HEAD_DIM ** -0.5) for p in range(n_pairs)]

    def qk_part(start, g, mx):
        bias = sc_ref[pl.ds(start, kc), :]
        out = []
        for p in range(g * HEAD_GROUP // 2, (g + 1) * HEAD_GROUP // 2):
            kp = k_ref[0, pl.ds(start, kc), p * LANES:(p + 1) * LANES]
            lg = _nt_dot(kp, att_rhs[p])
            for hh in range(2):
                hl = 2 * p + hh - g * HEAD_GROUP
                x = lg[:, hh * ATTN_Q:(hh + 1) * ATTN_Q] + bias
                xs_ref[g % 2, pl.ds(start, kc), hl * ATTN_Q:(hl + 1) * ATTN_Q] = x
                out.append(jnp.maximum(mx[hl], _fold_rows(x, jnp.max, HEAD_ACC_ROWS)))
        return tuple(out)

    def pv_part(start, g, m):
        for hl in range(HEAD_GROUP):
            h = g * HEAD_GROUP + hl
            pm = jnp.exp(xs_ref[g % 2, pl.ds(start, kc), hl * ATTN_Q:(hl + 1) * ATTN_Q] - m[hl])
            rows = slice(h * V_ROWS, (h + 1) * V_ROWS)
            acc_ref[rows, :] += jnp.dot(vt_ref[0, rows, pl.ds(start, kc)], pm.astype(BF16),
                                        preferred_element_type=F32)

    m = None
    for s in range(n_groups + 1):
        def body(c, mx, s=s, m=m):
            start = pl.multiple_of(c * kc, kc)
            if s < n_groups:
                mx = qk_part(start, s, mx)
            if s > 0:
                pv_part(start, s - 1, m)
            return mx

        mx = lax.fori_loop(0, n_chunks, body, (neg,) * HEAD_GROUP)
        if s < n_groups:
            m = [_finish_rows(mx[hl], jnp.max) for hl in range(HEAD_GROUP)]
    for h in range(N_HEADS):
        pv = acc_ref[h * V_ROWS:h * V_ROWS + HEAD_DIM, :]
        denom = acc_ref[h * V_ROWS + HEAD_DIM:h * V_ROWS + HEAD_DIM + 1, :]
        out_ref[h * HEAD_DIM:(h + 1) * HEAD_DIM, :] = pv / denom
    o_ref[0] = out_ref[...].T.astype(BF16)


def _attention(qi, q, wi, kk, k, vt):
    B, S, _ = q.shape
    grid = (B, S // ATTN_Q)
    blk = lambda n: pl.BlockSpec((1, ATTN_Q, n), lambda b, i: (b, i, 0))
    full = lambda n: pl.BlockSpec((1, S, n), lambda b, i: (b, 0, 0), pipeline_mode=pl.Buffered(1))
    return pl.pallas_call(
        _attn_kernel, grid=grid,
        in_specs=[blk(ATTN_WIDTH), blk(ATTN_WIDTH), blk(LANES), full(LANES), full(ATTN_WIDTH),
                  pl.BlockSpec((1, N_HEADS * V_ROWS, S), lambda b, i: (b, 0, 0), pipeline_mode=pl.Buffered(1))],
        out_specs=blk(ATTN_WIDTH),
        out_shape=jax.ShapeDtypeStruct((B, S, ATTN_WIDTH), BF16),
        scratch_shapes=[pltpu.VMEM((S, ATTN_Q), F32),
                        pltpu.VMEM((S, ATTN_Q), BF16),
                        pltpu.VMEM((2, S, HEAD_GROUP * ATTN_Q), F32),
                        pltpu.VMEM((N_HEADS * V_ROWS, ATTN_Q), F32),
                        pltpu.VMEM((ATTN_WIDTH, ATTN_Q), F32)],
        compiler_params=pltpu.CompilerParams(
            dimension_semantics=("arbitrary", "arbitrary"), vmem_limit_bytes=VMEM_LIMIT_BYTES),
        name="sparse_attn",
    )(qi, q, wi, kk, k, vt)


def _lane_first(mask, lane):
    return jnp.min(jnp.where(mask, lane, ROUTER_LANES), axis=-1, keepdims=True)


def _router(logits):
    lane = lax.broadcasted_iota(jnp.int32, logits.shape, 1)
    neg = -jnp.inf
    gl = jnp.where(lane < N_GROUPS, logits, neg)
    gmax = jnp.max(gl, axis=-1, keepdims=True)
    gsum = jnp.sum(jnp.exp(gl - gmax), axis=-1, keepdims=True)
    p_g = 1.0 / gsum
    gi = _lane_first(gl == gmax, lane)
    e_lo = N_GROUPS + gi * EXPERTS_PER_GROUP
    in_group = (lane >= e_lo) & (lane < e_lo + EXPERTS_PER_GROUP)
    el = jnp.where(in_group, logits, neg)
    emax = jnp.max(el, axis=-1, keepdims=True)
    ee = jnp.exp(el - emax)
    pe = ee / jnp.sum(ee, axis=-1, keepdims=True)
    pe = jnp.where(in_group, pe, -1.0)
    p1 = jnp.max(pe, axis=-1, keepdims=True)
    l1 = _lane_first(pe == p1, lane)
    pe2 = jnp.where(lane == l1, -1.0, pe)
    p2 = jnp.max(pe2, axis=-1, keepdims=True)
    l2 = _lane_first(pe2 == p2, lane)
    tot = p1 + p2
    c1 = p_g * (p1 / tot)
    c2 = p_g * (p2 / tot)
    return jnp.where(lane == l1, c1, 0.0) + jnp.where(lane == l2, c2, 0.0)


def _mix_kernel(x_ref, mconv_ref, attn_ref, ga_ref, wao_ref, wo_ref, gffn_ref, wr_ref, br_ref,
                x1_ref, h2_ref, gw_ref):
    ya = jnp.dot(attn_ref[...], wao_ref[...], preferred_element_type=F32)
    m = mconv_ref[...] + _sigmoid(ga_ref[...]) * ya
    x1 = x_ref[...] + jnp.dot(m.astype(BF16), wo_ref[...], preferred_element_type=F32)
    x1_ref[...] = x1
    h2 = _rms_rows(x1, gffn_ref[...]).astype(BF16)
    h2_ref[...] = h2
    logits = jnp.dot(h2, wr_ref[...], preferred_element_type=F32) + br_ref[...]
    gw_ref[...] = _router(logits)


def _mix(x, mconv, attn, ga, w_ao, w_o, g_ffn, w_r, b_r):
    T = x.shape[0]
    rows = MIX_ROWS
    row = lambda n: pl.BlockSpec((rows, n), lambda i: (i, 0))
    const = lambda a: pl.BlockSpec(a.shape, lambda i: (0, 0))
    return pl.pallas_call(
        _mix_kernel, grid=(T // rows,),
        in_specs=[row(D_MODEL), row(D_MODEL), row(ATTN_WIDTH), row(D_MODEL),
                  const(w_ao), const(w_o), const(g_ffn), const(w_r), const(b_r)],
        out_specs=(row(D_MODEL), row(D_MODEL), row(ROUTER_LANES)),
        out_shape=(jax.ShapeDtypeStruct((T, D_MODEL), F32),
                   jax.ShapeDtypeStruct((T, D_MODEL), BF16),
                   jax.ShapeDtypeStruct((T, ROUTER_LANES), F32)),
        compiler_params=pltpu.CompilerParams(
            dimension_semantics=("arbitrary",), vmem_limit_bytes=VMEM_LIMIT_BYTES),
        name="mix_router",
    )(x, mconv, attn, ga, w_ao, w_o, g_ffn, w_r, b_r)


def _moe_kernel(h2_ref, gw_ref, x1_ref, wg_ref, wu_ref, wd_ref, gfin_ref, o_ref, acc_ref):
    e = pl.program_id(1)

    @pl.when(e == 0)
    def _():
        acc_ref[...] = jnp.zeros(acc_ref.shape, F32)

    h2 = h2_ref[...]
    gw = gw_ref[...]
    lane = lax.broadcasted_iota(jnp.int32, gw.shape, 1)
    y = None
    for j in range(MOE_EXPERTS_PER_STEP):
        gate = jnp.dot(h2, wg_ref[j].astype(BF16), preferred_element_type=F32)
        up = jnp.dot(h2, wu_ref[j].astype(BF16), preferred_element_type=F32)
        expert_lane = e * MOE_EXPERTS_PER_STEP + j + N_GROUPS
        gwe = jnp.sum(jnp.where(lane == expert_lane, gw, 0.0), axis=-1, keepdims=True)
        hid = (gate * _sigmoid(gate)) * up * gwe
        yj = jnp.dot(hid.astype(BF16), wd_ref[j].astype(BF16), preferred_element_type=F32)
        y = yj if y is None else y + yj
    acc_ref[...] += y

    @pl.when(e == N_EXPERTS // MOE_EXPERTS_PER_STEP - 1)
    def _():
        o_ref[...] = _rms_rows(x1_ref[...] + acc_ref[...], gfin_ref[...])


def _moe(h2, gw, x1, w_gate, w_up, w_down, g_final):
    T = h2.shape[0]
    rows = MOE_ROWS
    row = lambda n: pl.BlockSpec((rows, n), lambda i, e: (i, 0))
    return pl.pallas_call(
        _moe_kernel, grid=(T // rows, N_EXPERTS // MOE_EXPERTS_PER_STEP),
        in_specs=[row(D_MODEL), row(ROUTER_LANES), row(D_MODEL),
                  pl.BlockSpec((MOE_EXPERTS_PER_STEP, D_MODEL, D_EXPERT), lambda i, e: (e, 0, 0)),
                  pl.BlockSpec((MOE_EXPERTS_PER_STEP, D_MODEL, D_EXPERT), lambda i, e: (e, 0, 0)),
                  pl.BlockSpec((MOE_EXPERTS_PER_STEP, D_EXPERT, D_MODEL), lambda i, e: (e, 0, 0)),
                  pl.BlockSpec((1, D_MODEL), lambda i, e: (0, 0))],
        out_specs=row(D_MODEL),
        out_shape=jax.ShapeDtypeStruct((T, D_MODEL), F32),
        scratch_shapes=[pltpu.VMEM((rows, D_MODEL), F32)],
        compiler_params=pltpu.CompilerParams(
            dimension_semantics=("arbitrary", "arbitrary"), vmem_limit_bytes=VMEM_LIMIT_BYTES),
        name="moe_final",
    )(h2, gw, x1, w_gate, w_up, w_down, g_final)


def _rope_tables(positions):
    inv = 1.0 / (ROPE_THETA ** (jnp.arange(0, HEAD_DIM, 2, dtype=F32) / HEAD_DIM))
    ang = positions.astype(F32)[..., None] * inv
    return jnp.cos(ang), jnp.sin(ang)


def _layer(x, cos_t, sin_t, g_mix, w_in, w_dw, b_dw, ln_g, ln_b, w_conv_out, w_attn_out, w_o,
           g_ffn, w_rg, b_rg, w_re, b_re, w_gate, w_up, w_down, g_final):
    B, S, D = x.shape
    T = B * S
    offs = [0]
    for n in IN_SIZES:
        offs.append(offs[-1] + n)
    col = lambda j: w_in[:, offs[j]:offs[j + 1]]
    w_ki, w_wi = col(5), col(6)
    ws = (
        col(0), col(1), col(2), col(3), col(4),
        jnp.concatenate([w_ki, w_ki, jnp.pad(w_wi, ((0, 0), (0, LANES - IDX_HEADS)))], axis=1),
        col(7), col(8),
    )
    ws = tuple(w.astype(BF16) for w in ws)
    u, q, k, vt, qi, kk, wi, gc, ga = _inproj(x, g_mix.reshape(1, D), cos_t, sin_t, ws)

    mconv = _conv_branch(u, gc, w_dw.reshape(CONV_KERNEL, CONV_WIDTH), b_dw.reshape(1, -1),
                         ln_g.reshape(1, -1), ln_b.reshape(1, -1), w_conv_out.astype(BF16))
    attn = _attention(qi, q, wi, kk, k, vt)

    n_r = N_GROUPS + N_EXPERTS
    w_r = jnp.concatenate([w_rg, w_re.reshape(D, N_EXPERTS)], axis=1)
    w_r = jnp.pad(w_r, ((0, 0), (0, ROUTER_LANES - n_r))).astype(BF16)
    b_r = jnp.pad(jnp.concatenate([b_rg, b_re.reshape(N_EXPERTS)]), (0, ROUTER_LANES - n_r)).reshape(1, -1)
    x1, h2, gw = _mix(x.reshape(T, D), mconv.reshape(T, D), attn.reshape(T, ATTN_WIDTH), ga.reshape(T, D),
                      w_attn_out.astype(BF16), w_o.astype(BF16), g_ffn.reshape(1, D), w_r, b_r)
    out = _moe(h2, gw, x1, w_gate, w_up, w_down, g_final.reshape(1, D))
    return out.reshape(B, S, D)


def kernel(x, positions, g_mix, w_in, w_dw, b_dw, ln_g, ln_b, w_conv_out, w_attn_out, w_o, g_ffn,
           w_rg, b_rg, w_re, b_re, w_gate, w_up, w_down, g_final):
    depth = g_mix.shape[0]
    assert depth == 1, "final norm is fused into the single layer's MoE call"
    cos_t, sin_t = _rope_tables(positions)
    return _layer(x, cos_t, sin_t, g_mix[0], w_in[0], w_dw[0], b_dw[0], ln_g[0], ln_b[0],
                  w_conv_out[0], w_attn_out[0], w_o[0], g_ffn[0], w_rg[0], b_rg[0], w_re[0],
                  b_re[0], w_gate[0], w_up[0], w_down[0], g_final)
```

```python
import functools

import jax
import jax.numpy as jnp
from jax import lax
from jax.experimental import pallas as pl
from jax.experimental.pallas import tpu as pltpu

F32 = jnp.float32
BF16 = jnp.bfloat16

D_MODEL = 1024
CONV_WIDTH = 512
CONV_KERNEL = 31
N_HEADS = 8
HEAD_DIM = 64
ATTN_WIDTH = N_HEADS * HEAD_DIM
IDX_HEADS = 8
IDX_DIM = 64
TOPK_MAX = 256
ROPE_THETA = 10000.0
N_GROUPS = 4
EXPERTS_PER_GROUP = 4
N_EXPERTS = N_GROUPS * EXPERTS_PER_GROUP
D_EXPERT = 256
EPS = 1e-6
IN_SIZES = (2 * CONV_WIDTH, ATTN_WIDTH, ATTN_WIDTH, ATTN_WIDTH,
            IDX_HEADS * IDX_DIM, IDX_DIM, IDX_HEADS, D_MODEL, D_MODEL)

LANES = 128
SUBLANES = 8
VMEM_LIMIT_BYTES = 56 * 1024 * 1024

PROJ_ROWS = 512
CONV_ROWS = 512
CONV_HALO = 32
CONV_SUB = 64
ATTN_Q = 256
HEAD_GROUP = 4
ONES_ROWS = 16
V_ROWS = HEAD_DIM + ONES_ROWS
KEY_CHUNK = 512
COUNT_ROWS = 256
COUNT_ACC_ROWS = 32
ACC_ROWS = 64
HEAD_ACC_ROWS = 8
MIX_ROWS = 1024
MOE_ROWS = 1024
MOE_EXPERTS_PER_STEP = 2
ROUTER_LANES = 128
MASK_VALUE = -1e30
KEY16_MIN = -2 ** 15
PACK_ROWS = 16
STAGE2_BITS = 17
STAGE2_BELOW = 36000


def _rms_rows(x, g):
    ms = jnp.mean(x * x, axis=-1, keepdims=True)
    return x * lax.rsqrt(ms + EPS) * g


def _sigmoid(x):
    return jax.nn.sigmoid(x)


def _rope128(z, cos, sin, first_half):
    rot = jnp.where(first_half, pltpu.roll(z, LANES - HEAD_DIM // 2, 1),
                    pltpu.roll(z, HEAD_DIM // 2, 1))
    return z * cos + rot * sin


def _inproj_kernel(x_ref, g_ref, pos_ref, inv_ref, wu_ref, wq_ref, wk_ref, wv_ref, wqi_ref,
                   wkw_ref, wgc_ref, wga_ref,
                   u_ref, q_ref, k_ref, vt_ref, qi_ref, kk_ref, wi_ref, gc_ref, ga_ref):
    h = _rms_rows(x_ref[0], g_ref[...]).astype(BF16)
    ang = pos_ref[0] * inv_ref[...]
    lane = lax.broadcasted_iota(jnp.int32, ang.shape, 1)
    first_half = (lane % HEAD_DIM) < (HEAD_DIM // 2)
    cos = jnp.cos(ang)
    sin = jnp.where(first_half, -jnp.sin(ang), jnp.sin(ang))

    def proj(w_ref):
        return jnp.dot(h, w_ref[...], preferred_element_type=F32)

    u_ref[0] = proj(wu_ref)
    gc_ref[0] = proj(wgc_ref)
    ga_ref[0] = proj(wga_ref)
    for w_ref, o_ref in ((wq_ref, q_ref), (wk_ref, k_ref), (wqi_ref, qi_ref)):
        z = proj(w_ref)
        for c in range(ATTN_WIDTH // LANES):
            sl = slice(c * LANES, (c + 1) * LANES)
            o_ref[0, :, sl] = _rope128(z[:, sl], cos, sin, first_half).astype(BF16)
    kw = proj(wkw_ref)
    kk_ref[0] = _rope128(kw[:, :LANES], cos, sin, first_half).astype(BF16)
    idx_scale = (IDX_HEADS ** -0.5) * (IDX_DIM ** -0.5)
    wi_ref[0] = kw[:, LANES:] * idx_scale
    vt = proj(wv_ref).T.astype(BF16)
    ones = jnp.ones((ONES_ROWS, vt.shape[1]), BF16)
    vt_ref[0] = jnp.concatenate(
        [piece for hd in range(N_HEADS) for piece in (vt[hd * HEAD_DIM:(hd + 1) * HEAD_DIM], ones)], axis=0)


def _inproj(x, g_mix, pos, inv_freq, ws):
    B, S, D = x.shape
    rows = PROJ_ROWS
    grid = (B, S // rows)
    row_spec = lambda n: pl.BlockSpec((1, rows, n), lambda b, i: (b, i, 0))
    w_spec = lambda w: pl.BlockSpec(w.shape, lambda b, i: (0, 0), pipeline_mode=pl.Buffered(1))
    out_shape = (
        jax.ShapeDtypeStruct((B, S, 2 * CONV_WIDTH), F32),
        jax.ShapeDtypeStruct((B, S, ATTN_WIDTH), BF16),
        jax.ShapeDtypeStruct((B, S, ATTN_WIDTH), BF16),
        jax.ShapeDtypeStruct((B, N_HEADS * V_ROWS, S), BF16),
        jax.ShapeDtypeStruct((B, S, ATTN_WIDTH), BF16),
        jax.ShapeDtypeStruct((B, S, LANES), BF16),
        jax.ShapeDtypeStruct((B, S, LANES), F32),
        jax.ShapeDtypeStruct((B, S, D_MODEL), F32),
        jax.ShapeDtypeStruct((B, S, D_MODEL), F32),
    )
    out_specs = (
        row_spec(2 * CONV_WIDTH), row_spec(ATTN_WIDTH), row_spec(ATTN_WIDTH),
        pl.BlockSpec((1, N_HEADS * V_ROWS, rows), lambda b, i: (b, 0, i)),
        row_spec(ATTN_WIDTH), row_spec(LANES), row_spec(LANES), row_spec(D_MODEL), row_spec(D_MODEL),
    )
    in_specs = [row_spec(D), pl.BlockSpec((1, D), lambda b, i: (0, 0)), row_spec(1),
                pl.BlockSpec((1, LANES), lambda b, i: (0, 0))]
    in_specs += [w_spec(w) for w in ws]
    return pl.pallas_call(
        _inproj_kernel, grid=grid, in_specs=in_specs, out_specs=out_specs, out_shape=out_shape,
        compiler_params=pltpu.CompilerParams(
            dimension_semantics=("arbitrary", "arbitrary"), vmem_limit_bytes=VMEM_LIMIT_BYTES),
        name="inproj",
    )(x, g_mix, pos, inv_freq, *ws)


def _conv_kernel(u_ref, uh_ref, gc_ref, wdw_ref, bdw_ref, lng_ref, lnb_ref, wout_ref,
                 o_ref, g_buf, s_buf):
    i = pl.program_id(1)
    rows = u_ref.shape[1]
    uh = uh_ref[0]
    gh = uh[:, :CONV_WIDTH] * _sigmoid(uh[:, CONV_WIDTH:])
    g_buf[0, 0:CONV_HALO, :] = jnp.where(i > 0, gh, 0.0)
    um = u_ref[0]
    g_buf[0, CONV_HALO:CONV_HALO + rows, :] = um[:, :CONV_WIDTH] * _sigmoid(um[:, CONV_WIDTH:])
    span = CONV_HALO + rows - SUBLANES
    for r in range(1, SUBLANES):
        for base in range(0, span, CONV_SUB):
            n = min(CONV_SUB, span - base)
            g_buf[r, base:base + n, :] = g_buf[0, pl.ds(base + r, n), :]
    first = CONV_HALO - (CONV_KERNEL - 1)
    for rr in range(rows // CONV_SUB):
        acc = jnp.zeros((CONV_SUB, CONV_WIDTH), F32)
        for t in range(CONV_KERNEL):
            shift = (first + t) % SUBLANES
            row0 = rr * CONV_SUB + (first + t) - shift
            acc = acc + wdw_ref[t:t + 1, :] * g_buf[shift, row0:row0 + CONV_SUB, :]
        c = acc + bdw_ref[...]
        mu = jnp.mean(c, axis=-1, keepdims=True)
        d = c - mu
        var = jnp.mean(d * d, axis=-1, keepdims=True)
        n = d * lax.rsqrt(var + EPS) * lng_ref[...] + lnb_ref[...]
        s_buf[rr * CONV_SUB:(rr + 1) * CONV_SUB, :] = (n * _sigmoid(n)).astype(BF16)
    y = jnp.dot(s_buf[...], wout_ref[...], preferred_element_type=F32)
    o_ref[0] = _sigmoid(gc_ref[0]) * y


def _conv_branch(u, gc, w_dw, b_dw, ln_g, ln_b, w_out):
    B, S, _ = u.shape
    rows = CONV_ROWS
    halo_per_tile = rows // CONV_HALO
    grid = (B, S // rows)
    vec = lambda n: pl.BlockSpec((1, n), lambda b, i: (0, 0))
    return pl.pallas_call(
        _conv_kernel, grid=grid,
        in_specs=[
            pl.BlockSpec((1, rows, 2 * CONV_WIDTH), lambda b, i: (b, i, 0)),
            pl.BlockSpec((1, CONV_HALO, 2 * CONV_WIDTH),
                         lambda b, i: (b, jnp.maximum(i * halo_per_tile - 1, 0), 0)),
            pl.BlockSpec((1, rows, D_MODEL), lambda b, i: (b, i, 0)),
            pl.BlockSpec((CONV_KERNEL, CONV_WIDTH), lambda b, i: (0, 0)),
            vec(CONV_WIDTH), vec(CONV_WIDTH), vec(CONV_WIDTH),
            pl.BlockSpec((CONV_WIDTH, D_MODEL), lambda b, i: (0, 0)),
        ],
        out_specs=pl.BlockSpec((1, rows, D_MODEL), lambda b, i: (b, i, 0)),
        out_shape=jax.ShapeDtypeStruct((B, S, D_MODEL), F32),
        scratch_shapes=[pltpu.VMEM((SUBLANES, CONV_HALO + rows, CONV_WIDTH), F32),
                        pltpu.VMEM((rows, CONV_WIDTH), BF16)],
        compiler_params=pltpu.CompilerParams(
            dimension_semantics=("arbitrary", "arbitrary"), vmem_limit_bytes=VMEM_LIMIT_BYTES),
        name="conv_branch",
    )(u, u, gc, w_dw, b_dw, ln_g, ln_b, w_out)


def _pair_rhs(x_bf16, scale=None):
    x = x_bf16.astype(F32)
    if scale is not None:
        x = x * scale
    lane = lax.broadcasted_iota(jnp.int32, x.shape, 1)
    lo = jnp.where(lane < HEAD_DIM, x, 0.0)
    hi = jnp.where(lane >= HEAD_DIM, x, 0.0)
    return jnp.concatenate([lo, hi], axis=0).astype(BF16)


def _fold_rows(x, op, rows=ACC_ROWS):
    n = x.shape[0] // rows
    return op(x.reshape(n, rows, x.shape[1]), axis=0)


def _finish_rows(x, op):
    y = op(x.reshape(x.shape[0] // SUBLANES, SUBLANES, x.shape[1]), axis=0)
    return op(y, axis=0, keepdims=True)


def _nt_dot(a, b):
    return lax.dot_general(a, b, (((1,), (1,)), ((), ())), preferred_element_type=F32)


def _key16_to_key32(k16):
    bits = lax.shift_left(jnp.where(k16 >= 0, k16, k16 ^ jnp.int32(0x7FFF)), 16)
    return jnp.where(bits >= 0, bits, bits ^ jnp.int32(0x7FFFFFFF))


def _key_to_f32(key):
    bits = jnp.where(key >= 0, key, key ^ jnp.int32(0x7FFFFFFF))
    return lax.bitcast_convert_type(bits, F32)


def _attn_kernel(qi_ref, q_ref, wi_ref, kk_ref, k_ref, vt_ref, o_ref,
                 sc_ref, sb_ref, xs_ref, acc_ref, out_ref):
    i = pl.program_id(1)
    kc = KEY_CHUNK
    n_chunks = ((i + 1) * ATTN_Q + kc - 1) // kc
    topk = float(TOPK_MAX)
    t_idx = i * ATTN_Q + lax.broadcasted_iota(jnp.int32, (1, ATTN_Q), 1)
    s_iota = lax.broadcasted_iota(jnp.int32, (kc, ATTN_Q), 0)
    n_pairs = N_HEADS // 2

    qi = qi_ref[0]
    idx_rhs = [_pair_rhs(qi[:, p * LANES:(p + 1) * LANES]) for p in range(IDX_HEADS // 2)]
    wi_t = wi_ref[0].T

    def score_body(c, carry):
        start = pl.multiple_of(c * kc, kc)
        kk = kk_ref[0, pl.ds(start, kc), :]
        s = jnp.zeros((kc, ATTN_Q), F32)
        for p in range(IDX_HEADS // 2):
            r = _nt_dot(kk, idx_rhs[p])
            s = s + wi_t[2 * p:2 * p + 1, :] * jnp.maximum(r[:, :ATTN_Q], 0.0)
            s = s + wi_t[2 * p + 1:2 * p + 2, :] * jnp.maximum(r[:, ATTN_Q:], 0.0)
        causal = (start + s_iota) <= t_idx
        s = jnp.where(causal, s, -jnp.inf)
        sc_ref[pl.ds(start, kc), :] = s
        sb_ref[pl.ds(start, kc), :] = s.astype(BF16)
        return carry

    lax.fori_loop(0, n_chunks, score_body, 0)

    n_count = n_chunks * (kc // COUNT_ROWS)

    def count_ge(cand):
        def body(c, acc):
            start = pl.multiple_of(c * COUNT_ROWS, COUNT_ROWS)
            hit = jnp.where(sc_ref[pl.ds(start, COUNT_ROWS), :] >= cand, 1.0, 0.0)
            return acc + _fold_rows(hit, jnp.sum, COUNT_ACC_ROWS)
        acc = lax.fori_loop(0, n_count, body, jnp.zeros((COUNT_ACC_ROWS, ATTN_Q), F32))
        return _finish_rows(acc, jnp.sum)

    def count_ge_bf16(cand):
        def body(c, acc):
            start = pl.multiple_of(c * kc, kc)
            hit = jnp.where(sb_ref[pl.ds(start, kc), :] >= cand, jnp.ones((), BF16), jnp.zeros((), BF16))
            parts = [hit[r * PACK_ROWS:(r + 1) * PACK_ROWS] for r in range(kc // PACK_ROWS)]
            while len(parts) > 1:
                parts = [a + b for a, b in zip(parts[::2], parts[1::2])]
            return acc + parts[0].astype(F32)
        acc = lax.fori_loop(0, n_chunks, body, jnp.zeros((PACK_ROWS, ATTN_Q), F32))
        return _finish_rows(acc, jnp.sum)

    c0 = count_ge_bf16(jnp.zeros((1, ATTN_Q), BF16))
    k16_0 = jnp.where(c0 >= topk, jnp.int32(0), jnp.int32(KEY16_MIN))

    def bit16_body(j, k16):
        cand = k16 + lax.shift_left(jnp.int32(1), 14 - j)
        cnt = count_ge_bf16(_key_to_f32(_key16_to_key32(cand)).astype(BF16))
        return jnp.where(cnt >= topk, cand, k16)

    k16 = lax.fori_loop(0, 15, bit16_body, k16_0)
    low = _key16_to_key32(k16) - jnp.int32(STAGE2_BELOW)

    def bit_body(j, carry):
        key, cnt_key = carry
        cand = key + lax.shift_left(jnp.int32(1), STAGE2_BITS - 1 - j)
        cnt = count_ge(_key_to_f32(cand))
        take = cnt >= topk
        return jnp.where(take, cand, key), jnp.where(take, cnt, cnt_key)

    key, cnt_ge = lax.fori_loop(0, STAGE2_BITS, bit_body, (low, jnp.full((1, ATTN_Q), jnp.inf, F32)))
    thr = _key_to_f32(key)
    thr = jnp.where((k16 == KEY16_MIN) | (thr != thr), -jnp.inf, thr)
    excess = jnp.where((cnt_ge > topk) & (thr > -jnp.inf), 1.0, 0.0)
    has_excess = jnp.sum(excess) > 0.0

    @pl.when(has_excess)
    def _():
        r_i = lax.broadcasted_iota(jnp.int32, (kc, kc), 0)
        c_i = lax.broadcasted_iota(jnp.int32, (kc, kc), 1)
        strict_lower = jnp.where(c_i < r_i, 1.0, 0.0).astype(BF16)

        def count_gt(c, acc):
            start = pl.multiple_of(c * COUNT_ROWS, COUNT_ROWS)
            hit = jnp.where(sc_ref[pl.ds(start, COUNT_ROWS), :] > thr, 1.0, 0.0)
            return acc + _fold_rows(hit, jnp.sum, COUNT_ACC_ROWS)

        gt_acc = lax.fori_loop(0, n_count, count_gt, jnp.zeros((COUNT_ACC_ROWS, ATTN_Q), F32))
        need = topk - _finish_rows(gt_acc, jnp.sum)

        def body(c, before):
            start = pl.multiple_of(c * kc, kc)
            s = sc_ref[pl.ds(start, kc), :]
            tie = jnp.where(s == thr, 1.0, 0.0)
            rank = jnp.dot(strict_lower, tie.astype(BF16), preferred_element_type=F32) + before
            keep = (s > thr) | ((s == thr) & (rank < need))
            sc_ref[pl.ds(start, kc), :] = jnp.where(keep, jnp.inf, -jnp.inf)
            return before + jnp.sum(tie, axis=0, keepdims=True)
        lax.fori_loop(0, n_chunks, body, jnp.zeros((1, ATTN_Q), F32))

    q = q_ref[0]
    acc_ref[...] = jnp.zeros(acc_ref.shape, F32)
    neg = jnp.full((HEAD_ACC_ROWS, ATTN_Q), -jnp.inf, F32)
    n_groups = N_HEADS // HEAD_GROUP
    att_rhs = [_pair_rhs(q[:, p * LANES:(p + 1) * LANES], HEAD_DIM ** -0.5) for p in range(n_pairs)]

    def qk_part(start, g, mx):
        if g == 0:
            keep = (sc_ref[pl.ds(start, kc), :] >= thr) & ((start + s_iota) <= t_idx)
            bias = jnp.where(keep, 0.0, MASK_VALUE)
            sc_ref[pl.ds(start, kc), :] = bias
        else:
            bias = sc_ref[pl.ds(start, kc), :]
        out = []
        for p in range(g * HEAD_GROUP // 2, (g + 1) * HEAD_GROUP // 2):
            kp = k_ref[0, pl.ds(start, kc), p * LANES:(p + 1) * LANES]
            lg = _nt_dot(kp, att_rhs[p])
            for hh in range(2):
                hl = 2 * p + hh - g * HEAD_GROUP
                x = lg[:, hh * ATTN_Q:(hh + 1) * ATTN_Q] + bias
                xs_ref[g % 2, pl.ds(start, kc), hl * ATTN_Q:(hl + 1) * ATTN_Q] = x
                out.append(jnp.maximum(mx[hl], _fold_rows(x, jnp.max, HEAD_ACC_ROWS)))
        return tuple(out)

    def pv_part(start, g, m):
        for hl in range(HEAD_GROUP):
            h = g * HEAD_GROUP + hl
            pm = jnp.exp(xs_ref[g % 2, pl.ds(start, kc), hl * ATTN_Q:(hl + 1) * ATTN_Q] - m[hl])
            rows = slice(h * V_ROWS, (h + 1) * V_ROWS)
            acc_ref[rows, :] += jnp.dot(vt_ref[0, rows, pl.ds(start, kc)], pm.astype(BF16),
                                        preferred_element_type=F32)

    m = None
    for s in range(n_groups + 1):
        def body(c, mx, s=s, m=m):
            start = pl.multiple_of(c * kc, kc)
            if s < n_groups:
                mx = qk_part(start, s, mx)
            if s > 0:
                pv_part(start, s - 1, m)
            return mx

        mx = lax.fori_loop(0, n_chunks, body, (neg,) * HEAD_GROUP)
        if s < n_groups:
            m = [_finish_rows(mx[hl], jnp.max) for hl in range(HEAD_GROUP)]
    for h in range(N_HEADS):
        pv = acc_ref[h * V_ROWS:h * V_ROWS + HEAD_DIM, :]
        denom = acc_ref[h * V_ROWS + HEAD_DIM:h * V_ROWS + HEAD_DIM + 1, :]
        out_ref[h * HEAD_DIM:(h + 1) * HEAD_DIM, :] = pv / denom
    o_ref[0] = out_ref[...].T.astype(BF16)


def _attention(qi, q, wi, kk, k, vt):
    B, S, _ = q.shape
    grid = (B, S // ATTN_Q)
    blk = lambda n: pl.BlockSpec((1, ATTN_Q, n), lambda b, i: (b, i, 0))
    full = lambda n: pl.BlockSpec((1, S, n), lambda b, i: (b, 0, 0), pipeline_mode=pl.Buffered(1))
    return pl.pallas_call(
        _attn_kernel, grid=grid,
        in_specs=[blk(ATTN_WIDTH), blk(ATTN_WIDTH), blk(LANES), full(LANES), full(ATTN_WIDTH),
                  pl.BlockSpec((1, N_HEADS * V_ROWS, S), lambda b, i: (b, 0, 0), pipeline_mode=pl.Buffered(1))],
        out_specs=blk(ATTN_WIDTH),
        out_shape=jax.ShapeDtypeStruct((B, S, ATTN_WIDTH), BF16),
        scratch_shapes=[pltpu.VMEM((S, ATTN_Q), F32),
                        pltpu.VMEM((S, ATTN_Q), BF16),
                        pltpu.VMEM((2, S, HEAD_GROUP * ATTN_Q), F32),
                        pltpu.VMEM((N_HEADS * V_ROWS, ATTN_Q), F32),
                        pltpu.VMEM((ATTN_WIDTH, ATTN_Q), F32)],
        compiler_params=pltpu.CompilerParams(
            dimension_semantics=("arbitrary", "arbitrary"), vmem_limit_bytes=VMEM_LIMIT_BYTES),
        name="sparse_attn",
    )(qi, q, wi, kk, k, vt)


def _lane_first(mask, lane):
    return jnp.min(jnp.where(mask, lane, ROUTER_LANES), axis=-1, keepdims=True)


def _router(logits):
    lane = lax.broadcasted_iota(jnp.int32, logits.shape, 1)
    neg = -jnp.inf
    gl = jnp.where(lane < N_GROUPS, logits, neg)
    gmax = jnp.max(gl, axis=-1, keepdims=True)
    gsum = jnp.sum(jnp.exp(gl - gmax), axis=-1, keepdims=True)
    p_g = 1.0 / gsum
    gi = _lane_first(gl == gmax, lane)
    e_lo = N_GROUPS + gi * EXPERTS_PER_GROUP
    in_group = (lane >= e_lo) & (lane < e_lo + EXPERTS_PER_GROUP)
    el = jnp.where(in_group, logits, neg)
    emax = jnp.max(el, axis=-1, keepdims=True)
    ee = jnp.exp(el - emax)
    pe = ee / jnp.sum(ee, axis=-1, keepdims=True)
    pe = jnp.where(in_group, pe, -1.0)
    p1 = jnp.max(pe, axis=-1, keepdims=True)
    l1 = _lane_first(pe == p1, lane)
    pe2 = jnp.where(lane == l1, -1.0, pe)
    p2 = jnp.max(pe2, axis=-1, keepdims=True)
    l2 = _lane_first(pe2 == p2, lane)
    tot = p1 + p2
    c1 = p_g * (p1 / tot)
    c2 = p_g * (p2 / tot)
    return jnp.where(lane == l1, c1, 0.0) + jnp.where(lane == l2, c2, 0.0)


def _mix_kernel(x_ref, mconv_ref, attn_ref, ga_ref, wao_ref, wo_ref, gffn_ref, wr_ref, br_ref,
                x1_ref, h2_ref, gw_ref):
    ya = jnp.dot(attn_ref[...], wao_ref[...], preferred_element_type=F32)
    m = mconv_ref[...] + _sigmoid(ga_ref[...]) * ya
    x1 = x_ref[...] + jnp.dot(m.astype(BF16), wo_ref[...], preferred_element_type=F32)
    x1_ref[...] = x1
    h2 = _rms_rows(x1, gffn_ref[...]).astype(BF16)
    h2_ref[...] = h2
    logits = jnp.dot(h2, wr_ref[...], preferred_element_type=F32) + br_ref[...]
    gw_ref[...] = _router(logits)


def _mix(x, mconv, attn, ga, w_ao, w_o, g_ffn, w_r, b_r):
    T = x.shape[0]
    rows = MIX_ROWS
    row = lambda n: pl.BlockSpec((rows, n), lambda i: (i, 0))
    const = lambda a: pl.BlockSpec(a.shape, lambda i: (0, 0))
    return pl.pallas_call(
        _mix_kernel, grid=(T // rows,),
        in_specs=[row(D_MODEL), row(D_MODEL), row(ATTN_WIDTH), row(D_MODEL),
                  const(w_ao), const(w_o), const(g_ffn), const(w_r), const(b_r)],
        out_specs=(row(D_MODEL), row(D_MODEL), row(ROUTER_LANES)),
        out_shape=(jax.ShapeDtypeStruct((T, D_MODEL), F32),
                   jax.ShapeDtypeStruct((T, D_MODEL), BF16),
                   jax.ShapeDtypeStruct((T, ROUTER_LANES), F32)),
        compiler_params=pltpu.CompilerParams(
            dimension_semantics=("arbitrary",), vmem_limit_bytes=VMEM_LIMIT_BYTES),
        name="mix_router",
    )(x, mconv, attn, ga, w_ao, w_o, g_ffn, w_r, b_r)


def _moe_kernel(h2_ref, gw_ref, x1_ref, wg_ref, wu_ref, wd_ref, gfin_ref, o_ref, acc_ref):
    e = pl.program_id(1)

    @pl.when(e == 0)
    def _():
        acc_ref[...] = jnp.zeros(acc_ref.shape, F32)

    h2 = h2_ref[...]
    gw = gw_ref[...]
    lane = lax.broadcasted_iota(jnp.int32, gw.shape, 1)
    y = None
    for j in range(MOE_EXPERTS_PER_STEP):
        gate = jnp.dot(h2, wg_ref[j].astype(BF16), preferred_element_type=F32)
        up = jnp.dot(h2, wu_ref[j].astype(BF16), preferred_element_type=F32)
        expert_lane = e * MOE_EXPERTS_PER_STEP + j + N_GROUPS
        gwe = jnp.sum(jnp.where(lane == expert_lane, gw, 0.0), axis=-1, keepdims=True)
        hid = (gate * _sigmoid(gate)) * up * gwe
        yj = jnp.dot(hid.astype(BF16), wd_ref[j].astype(BF16), preferred_element_type=F32)
        y = yj if y is None else y + yj
    acc_ref[...] += y

    @pl.when(e == N_EXPERTS // MOE_EXPERTS_PER_STEP - 1)
    def _():
        o_ref[...] = _rms_rows(x1_ref[...] + acc_ref[...], gfin_ref[...])


def _moe(h2, gw, x1, w_gate, w_up, w_down, g_final):
    T = h2.shape[0]
    rows = MOE_ROWS
    row = lambda n: pl.BlockSpec((rows, n), lambda i, e: (i, 0))
    return pl.pallas_call(
        _moe_kernel, grid=(T // rows, N_EXPERTS // MOE_EXPERTS_PER_STEP),
        in_specs=[row(D_MODEL), row(ROUTER_LANES), row(D_MODEL),
                  pl.BlockSpec((MOE_EXPERTS_PER_STEP, D_MODEL, D_EXPERT), lambda i, e: (e, 0, 0)),
                  pl.BlockSpec((MOE_EXPERTS_PER_STEP, D_MODEL, D_EXPERT), lambda i, e: (e, 0, 0)),
                  pl.BlockSpec((MOE_EXPERTS_PER_STEP, D_EXPERT, D_MODEL), lambda i, e: (e, 0, 0)),
                  pl.BlockSpec((1, D_MODEL), lambda i, e: (0, 0))],
        out_specs=row(D_MODEL),
        out_shape=jax.ShapeDtypeStruct((T, D_MODEL), F32),
        scratch_shapes=[pltpu.VMEM((rows, D_MODEL), F32)],
        compiler_params=pltpu.CompilerParams(
            dimension_semantics=("arbitrary", "arbitrary"), vmem_limit_bytes=VMEM_LIMIT_BYTES),
        name="moe_final",
    )(h2, gw, x1, w_gate, w_up, w_down, g_final)


def _rope_inputs(positions):
    half = HEAD_DIM // 2
    inv = 1.0 / (ROPE_THETA ** (jnp.arange(0, HEAD_DIM, 2, dtype=F32) / HEAD_DIM))
    return positions.astype(F32)[..., None], inv[jnp.arange(LANES) % half].reshape(1, LANES)


def _layer(x, pos, inv_freq, g_mix, w_in, w_dw, b_dw, ln_g, ln_b, w_conv_out, w_attn_out, w_o,
           g_ffn, w_rg, b_rg, w_re, b_re, w_gate, w_up, w_down, g_final):
    B, S, D = x.shape
    T = B * S
    offs = [0]
    for n in IN_SIZES:
        offs.append(offs[-1] + n)
    col = lambda j: w_in[:, offs[j]:offs[j + 1]]
    w_ki, w_wi = col(5), col(6)
    ws = (
        col(0), col(1), col(2), col(3), col(4),
        jnp.concatenate([w_ki, w_ki, jnp.pad(w_wi, ((0, 0), (0, LANES - IDX_HEADS)))], axis=1),
        col(7), col(8),
    )
    ws = tuple(w.astype(BF16) for w in ws)
    u, q, k, vt, qi, kk, wi, gc, ga = _inproj(x, g_mix.reshape(1, D), pos, inv_freq, ws)

    mconv = _conv_branch(u, gc, w_dw.reshape(CONV_KERNEL, CONV_WIDTH), b_dw.reshape(1, -1),
                         ln_g.reshape(1, -1), ln_b.reshape(1, -1), w_conv_out.astype(BF16))
    attn = _attention(qi, q, wi, kk, k, vt)

    n_r = N_GROUPS + N_EXPERTS
    w_r = jnp.concatenate([w_rg, w_re.reshape(D, N_EXPERTS)], axis=1)
    w_r = jnp.pad(w_r, ((0, 0), (0, ROUTER_LANES - n_r))).astype(BF16)
    b_r = jnp.pad(jnp.concatenate([b_rg, b_re.reshape(N_EXPERTS)]), (0, ROUTER_LANES - n_r)).reshape(1, -1)
    x1, h2, gw = _mix(x.reshape(T, D), mconv.reshape(T, D), attn.reshape(T, ATTN_WIDTH), ga.reshape(T, D),
                      w_attn_out.astype(BF16), w_o.astype(BF16), g_ffn.reshape(1, D), w_r, b_r)
    out = _moe(h2, gw, x1, w_gate, w_up, w_down, g_final.reshape(1, D))
    return out.reshape(B, S, D)


def kernel(x, positions, g_mix, w_in, w_dw, b_dw, ln_g, ln_b, w_conv_out, w_attn_out, w_o, g_ffn,
           w_rg, b_rg, w_re, b_re, w_gate, w_up, w_down, g_final):
    depth = g_mix.shape[0]
    assert depth == 1, "final norm is fused into the single layer's MoE call"
    pos, inv_freq = _rope_inputs(positions)
    return _layer(x, pos, inv_freq, g_mix[0], w_in[0], w_dw[0], b_dw[0], ln_g[0], ln_b[0],
                  w_conv_out[0], w_attn_out[0], w_o[0], g_ffn[0], w_rg[0], b_rg[0], w_re[0],
                  b_re[0], w_gate[0], w_up[0], w_down[0], g_final)
```

```python
import functools

import jax
import jax.numpy as jnp
from jax import lax
from jax.experimental import pallas as pl
from jax.experimental.pallas import tpu as pltpu

F32 = jnp.float32
BF16 = jnp.bfloat16

D_MODEL = 1024
CONV_WIDTH = 512
CONV_KERNEL = 31
N_HEADS = 8
HEAD_DIM = 64
ATTN_WIDTH = N_HEADS * HEAD_DIM
IDX_HEADS = 8
IDX_DIM = 64
TOPK_MAX = 256
ROPE_THETA = 10000.0
N_GROUPS = 4
EXPERTS_PER_GROUP = 4
N_EXPERTS = N_GROUPS * EXPERTS_PER_GROUP
D_EXPERT = 256
EPS = 1e-6
IN_SIZES = (2 * CONV_WIDTH, ATTN_WIDTH, ATTN_WIDTH, ATTN_WIDTH,
            IDX_HEADS * IDX_DIM, IDX_DIM, IDX_HEADS, D_MODEL, D_MODEL)

LANES = 128
SUBLANES = 8
VMEM_LIMIT_BYTES = 56 * 1024 * 1024

PROJ_ROWS = 512
CONV_ROWS = 512
CONV_HALO = 32
CONV_SUB = 64
ATTN_Q = 256
HEAD_GROUP = 4
ONES_ROWS = 16
V_ROWS = HEAD_DIM + ONES_ROWS
KEY_CHUNK = 512
COUNT_ROWS = 256
COUNT_ACC_ROWS = 32
ACC_ROWS = 64
HEAD_ACC_ROWS = 8
MIX_ROWS = 1024
MOE_ROWS = 1024
MOE_EXPERTS_PER_STEP = 2
ROUTER_LANES = 128
MASK_VALUE = -1e30
KEY16_MIN = -2 ** 15
PACK_ROWS = 16
STAGE2_BITS = 17
STAGE2_BELOW = 36000


def _rms_rows(x, g):
    ms = jnp.mean(x * x, axis=-1, keepdims=True)
    return x * lax.rsqrt(ms + EPS) * g


def _sigmoid(x):
    return jax.nn.sigmoid(x)


def _rope128(z, cos, sin, first_half):
    rot = jnp.where(first_half, pltpu.roll(z, LANES - HEAD_DIM // 2, 1),
                    pltpu.roll(z, HEAD_DIM // 2, 1))
    return z * cos + rot * sin


def _inproj_kernel(x_ref, g_ref, pos_ref, inv_ref, wu_ref, wq_ref, wk_ref, wv_ref, wqi_ref,
                   wkw_ref, wgc_ref, wga_ref,
                   u_ref, q_ref, k_ref, vt_ref, qi_ref, kk_ref, wi_ref, gc_ref, ga_ref):
    h = _rms_rows(x_ref[0], g_ref[...]).astype(BF16)
    ang = pos_ref[0] * inv_ref[...]
    lane = lax.broadcasted_iota(jnp.int32, ang.shape, 1)
    first_half = (lane % HEAD_DIM) < (HEAD_DIM // 2)
    cos = jnp.cos(ang)
    sin = jnp.where(first_half, -jnp.sin(ang), jnp.sin(ang))

    def proj(w_ref):
        return jnp.dot(h, w_ref[...], preferred_element_type=F32)

    u_ref[0] = proj(wu_ref)
    gc_ref[0] = proj(wgc_ref)
    ga_ref[0] = proj(wga_ref)
    for w_ref, o_ref in ((wq_ref, q_ref), (wk_ref, k_ref), (wqi_ref, qi_ref)):
        z = proj(w_ref)
        for c in range(ATTN_WIDTH // LANES):
            sl = slice(c * LANES, (c + 1) * LANES)
            o_ref[0, :, sl] = _rope128(z[:, sl], cos, sin, first_half).astype(BF16)
    kw = proj(wkw_ref)
    kk_ref[0] = _rope128(kw[:, :LANES], cos, sin, first_half).astype(BF16)
    idx_scale = (IDX_HEADS ** -0.5) * (IDX_DIM ** -0.5)
    wi_ref[0] = kw[:, LANES:] * idx_scale
    vt = proj(wv_ref).T.astype(BF16)
    ones = jnp.ones((ONES_ROWS, vt.shape[1]), BF16)
    vt_ref[0] = jnp.concatenate(
        [piece for hd in range(N_HEADS) for piece in (vt[hd * HEAD_DIM:(hd + 1) * HEAD_DIM], ones)], axis=0)


def _inproj(x, g_mix, pos, inv_freq, ws):
    B, S, D = x.shape
    rows = PROJ_ROWS
    grid = (B, S // rows)
    row_spec = lambda n: pl.BlockSpec((1, rows, n), lambda b, i: (b, i, 0))
    w_spec = lambda w: pl.BlockSpec(w.shape, lambda b, i: (0, 0), pipeline_mode=pl.Buffered(1))
    out_shape = (
        jax.ShapeDtypeStruct((B, S, 2 * CONV_WIDTH), F32),
        jax.ShapeDtypeStruct((B, S, ATTN_WIDTH), BF16),
        jax.ShapeDtypeStruct((B, S, ATTN_WIDTH), BF16),
        jax.ShapeDtypeStruct((B, N_HEADS * V_ROWS, S), BF16),
        jax.ShapeDtypeStruct((B, S, ATTN_WIDTH), BF16),
        jax.ShapeDtypeStruct((B, S, LANES), BF16),
        jax.ShapeDtypeStruct((B, S, LANES), F32),
        jax.ShapeDtypeStruct((B, S, D_MODEL), F32),
        jax.ShapeDtypeStruct((B, S, D_MODEL), F32),
    )
    out_specs = (
        row_spec(2 * CONV_WIDTH), row_spec(ATTN_WIDTH), row_spec(ATTN_WIDTH),
        pl.BlockSpec((1, N_HEADS * V_ROWS, rows), lambda b, i: (b, 0, i)),
        row_spec(ATTN_WIDTH), row_spec(LANES), row_spec(LANES), row_spec(D_MODEL), row_spec(D_MODEL),
    )
    in_specs = [row_spec(D), pl.BlockSpec((1, D), lambda b, i: (0, 0)), row_spec(1),
                pl.BlockSpec((1, LANES), lambda b, i: (0, 0))]
    in_specs += [w_spec(w) for w in ws]
    return pl.pallas_call(
        _inproj_kernel, grid=grid, in_specs=in_specs, out_specs=out_specs, out_shape=out_shape,
        compiler_params=pltpu.CompilerParams(
            dimension_semantics=("arbitrary", "arbitrary"), vmem_limit_bytes=VMEM_LIMIT_BYTES),
        name="inproj",
    )(x, g_mix, pos, inv_freq, *ws)


def _conv_kernel(u_ref, uh_ref, gc_ref, wdw_ref, bdw_ref, lng_ref, lnb_ref, wout_ref,
                 o_ref, g_buf, s_buf):
    i = pl.program_id(1)
    rows = u_ref.shape[1]
    uh = uh_ref[0]
    gh = uh[:, :CONV_WIDTH] * _sigmoid(uh[:, CONV_WIDTH:])
    g_buf[0, 0:CONV_HALO, :] = jnp.where(i > 0, gh, 0.0)
    um = u_ref[0]
    g_buf[0, CONV_HALO:CONV_HALO + rows, :] = um[:, :CONV_WIDTH] * _sigmoid(um[:, CONV_WIDTH:])
    span = CONV_HALO + rows - SUBLANES
    for r in range(1, SUBLANES):
        for base in range(0, span, CONV_SUB):
            n = min(CONV_SUB, span - base)
            g_buf[r, base:base + n, :] = g_buf[0, pl.ds(base + r, n), :]
    first = CONV_HALO - (CONV_KERNEL - 1)
    for rr in range(rows // CONV_SUB):
        acc = jnp.zeros((CONV_SUB, CONV_WIDTH), F32)
        for t in range(CONV_KERNEL):
            shift = (first + t) % SUBLANES
            row0 = rr * CONV_SUB + (first + t) - shift
            acc = acc + wdw_ref[t:t + 1, :] * g_buf[shift, row0:row0 + CONV_SUB, :]
        c = acc + bdw_ref[...]
        mu = jnp.mean(c, axis=-1, keepdims=True)
        d = c - mu
        var = jnp.mean(d * d, axis=-1, keepdims=True)
        n = d * lax.rsqrt(var + EPS) * lng_ref[...] + lnb_ref[...]
        s_buf[rr * CONV_SUB:(rr + 1) * CONV_SUB, :] = (n * _sigmoid(n)).astype(BF16)
    y = jnp.dot(s_buf[...], wout_ref[...], preferred_element_type=F32)
    o_ref[0] = _sigmoid(gc_ref[0]) * y


def _conv_branch(u, gc, w_dw, b_dw, ln_g, ln_b, w_out):
    B, S, _ = u.shape
    rows = CONV_ROWS
    halo_per_tile = rows // CONV_HALO
    grid = (B, S // rows)
    vec = lambda n: pl.BlockSpec((1, n), lambda b, i: (0, 0))
    return pl.pallas_call(
        _conv_kernel, grid=grid,
        in_specs=[
            pl.BlockSpec((1, rows, 2 * CONV_WIDTH), lambda b, i: (b, i, 0)),
            pl.BlockSpec((1, CONV_HALO, 2 * CONV_WIDTH),
                         lambda b, i: (b, jnp.maximum(i * halo_per_tile - 1, 0), 0)),
            pl.BlockSpec((1, rows, D_MODEL), lambda b, i: (b, i, 0)),
            pl.BlockSpec((CONV_KERNEL, CONV_WIDTH), lambda b, i: (0, 0)),
            vec(CONV_WIDTH), vec(CONV_WIDTH), vec(CONV_WIDTH),
            pl.BlockSpec((CONV_WIDTH, D_MODEL), lambda b, i: (0, 0)),
        ],
        out_specs=pl.BlockSpec((1, rows, D_MODEL), lambda b, i: (b, i, 0)),
        out_shape=jax.ShapeDtypeStruct((B, S, D_MODEL), F32),
        scratch_shapes=[pltpu.VMEM((SUBLANES, CONV_HALO + rows, CONV_WIDTH), F32),
                        pltpu.VMEM((rows, CONV_WIDTH), BF16)],
        compiler_params=pltpu.CompilerParams(
            dimension_semantics=("arbitrary", "arbitrary"), vmem_limit_bytes=VMEM_LIMIT_BYTES),
        name="conv_branch",
    )(u, u, gc, w_dw, b_dw, ln_g, ln_b, w_out)


def _pair_rhs(x_bf16, scale=None):
    x = x_bf16.astype(F32)
    if scale is not None:
        x = x * scale
    lane = lax.broadcasted_iota(jnp.int32, x.shape, 1)
    lo = jnp.where(lane < HEAD_DIM, x, 0.0)
    hi = jnp.where(lane >= HEAD_DIM, x, 0.0)
    return jnp.concatenate([lo, hi], axis=0).astype(BF16)


def _fold_rows(x, op, rows=ACC_ROWS):
    n = x.shape[0] // rows
    return op(x.reshape(n, rows, x.shape[1]), axis=0)


def _finish_rows(x, op):
    y = op(x.reshape(x.shape[0] // SUBLANES, SUBLANES, x.shape[1]), axis=0)
    return op(y, axis=0, keepdims=True)


def _chunk_loop(n, body, init):
    carry = lax.fori_loop(0, n // 2, lambda i, cr: body(2 * i + 1, body(2 * i, cr)), init)
    return lax.cond(n % 2 == 1, lambda cr: body(n - 1, cr), lambda cr: cr, carry)


def _nt_dot(a, b):
    return lax.dot_general(a, b, (((1,), (1,)), ((), ())), preferred_element_type=F32)


def _key16_to_key32(k16):
    bits = lax.shift_left(jnp.where(k16 >= 0, k16, k16 ^ jnp.int32(0x7FFF)), 16)
    return jnp.where(bits >= 0, bits, bits ^ jnp.int32(0x7FFFFFFF))


def _key_to_f32(key):
    bits = jnp.where(key >= 0, key, key ^ jnp.int32(0x7FFFFFFF))
    return lax.bitcast_convert_type(bits, F32)


def _attn_kernel(qi_ref, q_ref, wi_ref, kk_ref, k_ref, vt_ref, o_ref,
                 sc_ref, sb_ref, xs_ref, acc_ref, out_ref):
    i = pl.program_id(1)
    kc = KEY_CHUNK
    n_chunks = ((i + 1) * ATTN_Q + kc - 1) // kc
    topk = float(TOPK_MAX)
    t_idx = i * ATTN_Q + lax.broadcasted_iota(jnp.int32, (1, ATTN_Q), 1)
    s_iota = lax.broadcasted_iota(jnp.int32, (kc, ATTN_Q), 0)
    n_pairs = N_HEADS // 2

    qi = qi_ref[0]
    idx_rhs = [_pair_rhs(qi[:, p * LANES:(p + 1) * LANES]) for p in range(IDX_HEADS // 2)]
    wi_t = wi_ref[0].T

    def score_body(c, carry):
        start = pl.multiple_of(c * kc, kc)
        kk = kk_ref[0, pl.ds(start, kc), :]
        s = jnp.zeros((kc, ATTN_Q), F32)
        for p in range(IDX_HEADS // 2):
            r = _nt_dot(kk, idx_rhs[p])
            s = s + wi_t[2 * p:2 * p + 1, :] * jnp.maximum(r[:, :ATTN_Q], 0.0)
            s = s + wi_t[2 * p + 1:2 * p + 2, :] * jnp.maximum(r[:, ATTN_Q:], 0.0)
        causal = (start + s_iota) <= t_idx
        s = jnp.where(causal, s, -jnp.inf)
        sc_ref[pl.ds(start, kc), :] = s
        sb_ref[pl.ds(start, kc), :] = s.astype(BF16)
        return carry

    _chunk_loop(n_chunks, score_body, 0)

    n_count = n_chunks * (kc // COUNT_ROWS)

    def count_ge(cand):
        def body(c, acc):
            start = pl.multiple_of(c * COUNT_ROWS, COUNT_ROWS)
            hit = jnp.where(sc_ref[pl.ds(start, COUNT_ROWS), :] >= cand, 1.0, 0.0)
            return acc + _fold_rows(hit, jnp.sum, COUNT_ACC_ROWS)
        acc = lax.fori_loop(0, n_count, body, jnp.zeros((COUNT_ACC_ROWS, ATTN_Q), F32))
        return _finish_rows(acc, jnp.sum)

    def count_ge_bf16(cand):
        def body(c, acc):
            start = pl.multiple_of(c * kc, kc)
            hit = jnp.where(sb_ref[pl.ds(start, kc), :] >= cand, jnp.ones((), BF16), jnp.zeros((), BF16))
            parts = [hit[r * PACK_ROWS:(r + 1) * PACK_ROWS] for r in range(kc // PACK_ROWS)]
            while len(parts) > 1:
                parts = [a + b for a, b in zip(parts[::2], parts[1::2])]
            return acc + parts[0].astype(F32)
        acc = lax.fori_loop(0, n_chunks, body, jnp.zeros((PACK_ROWS, ATTN_Q), F32))
        return _finish_rows(acc, jnp.sum)

    c0 = count_ge_bf16(jnp.zeros((1, ATTN_Q), BF16))
    k16_0 = jnp.where(c0 >= topk, jnp.int32(0), jnp.int32(KEY16_MIN))

    def bit16_body(j, k16):
        cand = k16 + lax.shift_left(jnp.int32(1), 14 - j)
        cnt = count_ge_bf16(_key_to_f32(_key16_to_key32(cand)).astype(BF16))
        return jnp.where(cnt >= topk, cand, k16)

    k16 = lax.fori_loop(0, 15, bit16_body, k16_0)
    low = _key16_to_key32(k16) - jnp.int32(STAGE2_BELOW)

    def bit_body(j, carry):
        key, cnt_key = carry
        cand = key + lax.shift_left(jnp.int32(1), STAGE2_BITS - 1 - j)
        cnt = count_ge(_key_to_f32(cand))
        take = cnt >= topk
        return jnp.where(take, cand, key), jnp.where(take, cnt, cnt_key)

    key, cnt_ge = lax.fori_loop(0, STAGE2_BITS, bit_body, (low, jnp.full((1, ATTN_Q), jnp.inf, F32)))
    thr = _key_to_f32(key)
    thr = jnp.where((k16 == KEY16_MIN) | (thr != thr), -jnp.inf, thr)
    excess = jnp.where((cnt_ge > topk) & (thr > -jnp.inf), 1.0, 0.0)
    has_excess = jnp.sum(excess) > 0.0

    @pl.when(has_excess)
    def _():
        r_i = lax.broadcasted_iota(jnp.int32, (kc, kc), 0)
        c_i = lax.broadcasted_iota(jnp.int32, (kc, kc), 1)
        strict_lower = jnp.where(c_i < r_i, 1.0, 0.0).astype(BF16)

        def count_gt(c, acc):
            start = pl.multiple_of(c * COUNT_ROWS, COUNT_ROWS)
            hit = jnp.where(sc_ref[pl.ds(start, COUNT_ROWS), :] > thr, 1.0, 0.0)
            return acc + _fold_rows(hit, jnp.sum, COUNT_ACC_ROWS)

        gt_acc = lax.fori_loop(0, n_count, count_gt, jnp.zeros((COUNT_ACC_ROWS, ATTN_Q), F32))
        need = topk - _finish_rows(gt_acc, jnp.sum)

        def body(c, before):
            start = pl.multiple_of(c * kc, kc)
            s = sc_ref[pl.ds(start, kc), :]
            tie = jnp.where(s == thr, 1.0, 0.0)
            rank = jnp.dot(strict_lower, tie.astype(BF16), preferred_element_type=F32) + before
            keep = (s > thr) | ((s == thr) & (rank < need))
            sc_ref[pl.ds(start, kc), :] = jnp.where(keep, jnp.inf, -jnp.inf)
            return before + jnp.sum(tie, axis=0, keepdims=True)
        lax.fori_loop(0, n_chunks, body, jnp.zeros((1, ATTN_Q), F32))

    q = q_ref[0]
    acc_ref[...] = jnp.zeros(acc_ref.shape, F32)
    neg = jnp.full((HEAD_ACC_ROWS, ATTN_Q), -jnp.inf, F32)
    n_groups = N_HEADS // HEAD_GROUP
    att_rhs = [_pair_rhs(q[:, p * LANES:(p + 1) * LANES], HEAD_DIM ** -0.5) for p in range(n_pairs)]

    def qk_part(start, g, mx):
        if g == 0:
            keep = (sc_ref[pl.ds(start, kc), :] >= thr) & ((start + s_iota) <= t_idx)
            bias = jnp.where(keep, 0.0, MASK_VALUE)
            sc_ref[pl.ds(start, kc), :] = bias
        else:
            bias = sc_ref[pl.ds(start, kc), :]
        out = []
        for p in range(g * HEAD_GROUP // 2, (g + 1) * HEAD_GROUP // 2):
            kp = k_ref[0, pl.ds(start, kc), p * LANES:(p + 1) * LANES]
            lg = _nt_dot(kp, att_rhs[p])
            for hh in range(2):
                hl = 2 * p + hh - g * HEAD_GROUP
                x = lg[:, hh * ATTN_Q:(hh + 1) * ATTN_Q] + bias
                xs_ref[g % 2, pl.ds(start, kc), hl * ATTN_Q:(hl + 1) * ATTN_Q] = x
                out.append(jnp.maximum(mx[hl], _fold_rows(x, jnp.max, HEAD_ACC_ROWS)))
        return tuple(out)

    def pv_part(start, g, m):
        for hl in range(HEAD_GROUP):
            h = g * HEAD_GROUP + hl
            pm = jnp.exp(xs_ref[g % 2, pl.ds(start, kc), hl * ATTN_Q:(hl + 1) * ATTN_Q] - m[hl])
            rows = slice(h * V_ROWS, (h + 1) * V_ROWS)
            acc_ref[rows, :] += jnp.dot(vt_ref[0, rows, pl.ds(start, kc)], pm.astype(BF16),
                                        preferred_element_type=F32)

    m = None
    for s in range(n_groups + 1):
        def body(c, mx, s=s, m=m):
            start = pl.multiple_of(c * kc, kc)
            if s < n_groups:
                mx = qk_part(start, s, mx)
            if s > 0:
                pv_part(start, s - 1, m)
            return mx

        mx = _chunk_loop(n_chunks, body, (neg,) * HEAD_GROUP)
        if s < n_groups:
            m = [_finish_rows(mx[hl], jnp.max) for hl in range(HEAD_GROUP)]
    for h in range(N_HEADS):
        pv = acc_ref[h * V_ROWS:h * V_ROWS + HEAD_DIM, :]
        denom = acc_ref[h * V_ROWS + HEAD_DIM:h * V_ROWS + HEAD_DIM + 1, :]
        out_ref[h * HEAD_DIM:(h + 1) * HEAD_DIM, :] = pv / denom
    o_ref[0] = out_ref[...].T.astype(BF16)


def _attention(qi, q, wi, kk, k, vt):
    B, S, _ = q.shape
    grid = (B, S // ATTN_Q)
    blk = lambda n: pl.BlockSpec((1, ATTN_Q, n), lambda b, i: (b, i, 0))
    full = lambda n: pl.BlockSpec((1, S, n), lambda b, i: (b, 0, 0), pipeline_mode=pl.Buffered(1))
    return pl.pallas_call(
        _attn_kernel, grid=grid,
        in_specs=[blk(ATTN_WIDTH), blk(ATTN_WIDTH), blk(LANES), full(LANES), full(ATTN_WIDTH),
                  pl.BlockSpec((1, N_HEADS * V_ROWS, S), lambda b, i: (b, 0, 0), pipeline_mode=pl.Buffered(1))],
        out_specs=blk(ATTN_WIDTH),
        out_shape=jax.ShapeDtypeStruct((B, S, ATTN_WIDTH), BF16),
        scratch_shapes=[pltpu.VMEM((S, ATTN_Q), F32),
                        pltpu.VMEM((S, ATTN_Q), BF16),
                        pltpu.VMEM((2, S, HEAD_GROUP * ATTN_Q), F32),
                        pltpu.VMEM((N_HEADS * V_ROWS, ATTN_Q), F32),
                        pltpu.VMEM((ATTN_WIDTH, ATTN_Q), F32)],
        compiler_params=pltpu.CompilerParams(
            dimension_semantics=("arbitrary", "arbitrary"), vmem_limit_bytes=VMEM_LIMIT_BYTES),
        name="sparse_attn",
    )(qi, q, wi, kk, k, vt)


def _lane_first(mask, lane):
    return jnp.min(jnp.where(mask, lane, ROUTER_LANES), axis=-1, keepdims=True)


def _router(logits):
    lane = lax.broadcasted_iota(jnp.int32, logits.shape, 1)
    neg = -jnp.inf
    gl = jnp.where(lane < N_GROUPS, logits, neg)
    gmax = jnp.max(gl, axis=-1, keepdims=True)
    gsum = jnp.sum(jnp.exp(gl - gmax), axis=-1, keepdims=True)
    p_g = 1.0 / gsum
    gi = _lane_first(gl == gmax, lane)
    e_lo = N_GROUPS + gi * EXPERTS_PER_GROUP
    in_group = (lane >= e_lo) & (lane < e_lo + EXPERTS_PER_GROUP)
    el = jnp.where(in_group, logits, neg)
    emax = jnp.max(el, axis=-1, keepdims=True)
    ee = jnp.exp(el - emax)
    pe = ee / jnp.sum(ee, axis=-1, keepdims=True)
    pe = jnp.where(in_group, pe, -1.0)
    p1 = jnp.max(pe, axis=-1, keepdims=True)
    l1 = _lane_first(pe == p1, lane)
    pe2 = jnp.where(lane == l1, -1.0, pe)
    p2 = jnp.max(pe2, axis=-1, keepdims=True)
    l2 = _lane_first(pe2 == p2, lane)
    tot = p1 + p2
    c1 = p_g * (p1 / tot)
    c2 = p_g * (p2 / tot)
    return jnp.where(lane == l1, c1, 0.0) + jnp.where(lane == l2, c2, 0.0)


def _mix_kernel(x_ref, mconv_ref, attn_ref, ga_ref, wao_ref, wo_ref, gffn_ref, wr_ref, br_ref,
                x1_ref, h2_ref, gw_ref):
    ya = jnp.dot(attn_ref[...], wao_ref[...], preferred_element_type=F32)
    m = mconv_ref[...] + _sigmoid(ga_ref[...]) * ya
    x1 = x_ref[...] + jnp.dot(m.astype(BF16), wo_ref[...], preferred_element_type=F32)
    x1_ref[...] = x1
    h2 = _rms_rows(x1, gffn_ref[...]).astype(BF16)
    h2_ref[...] = h2
    logits = jnp.dot(h2, wr_ref[...], preferred_element_type=F32) + br_ref[...]
    gw_ref[...] = _router(logits)


def _mix(x, mconv, attn, ga, w_ao, w_o, g_ffn, w_r, b_r):
    T = x.shape[0]
    rows = MIX_ROWS
    row = lambda n: pl.BlockSpec((rows, n), lambda i: (i, 0))
    const = lambda a: pl.BlockSpec(a.shape, lambda i: (0, 0))
    return pl.pallas_call(
        _mix_kernel, grid=(T // rows,),
        in_specs=[row(D_MODEL), row(D_MODEL), row(ATTN_WIDTH), row(D_MODEL),
                  const(w_ao), const(w_o), const(g_ffn), const(w_r), const(b_r)],
        out_specs=(row(D_MODEL), row(D_MODEL), row(ROUTER_LANES)),
        out_shape=(jax.ShapeDtypeStruct((T, D_MODEL), F32),
                   jax.ShapeDtypeStruct((T, D_MODEL), BF16),
                   jax.ShapeDtypeStruct((T, ROUTER_LANES), F32)),
        compiler_params=pltpu.CompilerParams(
            dimension_semantics=("arbitrary",), vmem_limit_bytes=VMEM_LIMIT_BYTES),
        name="mix_router",
    )(x, mconv, attn, ga, w_ao, w_o, g_ffn, w_r, b_r)


def _moe_kernel(h2_ref, gw_ref, x1_ref, wg_ref, wu_ref, wd_ref, gfin_ref, o_ref, acc_ref):
    e = pl.program_id(1)

    @pl.when(e == 0)
    def _():
        acc_ref[...] = jnp.zeros(acc_ref.shape, F32)

    h2 = h2_ref[...]
    gw = gw_ref[...]
    lane = lax.broadcasted_iota(jnp.int32, gw.shape, 1)
    y = None
    for j in range(MOE_EXPERTS_PER_STEP):
        gate = jnp.dot(h2, wg_ref[j].astype(BF16), preferred_element_type=F32)
        up = jnp.dot(h2, wu_ref[j].astype(BF16), preferred_element_type=F32)
        expert_lane = e * MOE_EXPERTS_PER_STEP + j + N_GROUPS
        gwe = jnp.sum(jnp.where(lane == expert_lane, gw, 0.0), axis=-1, keepdims=True)
        hid = (gate * _sigmoid(gate)) * up * gwe
        yj = jnp.dot(hid.astype(BF16), wd_ref[j].astype(BF16), preferred_element_type=F32)
        y = yj if y is None else y + yj
    acc_ref[...] += y

    @pl.when(e == N_EXPERTS // MOE_EXPERTS_PER_STEP - 1)
    def _():
        o_ref[...] = _rms_rows(x1_ref[...] + acc_ref[...], gfin_ref[...])


def _moe(h2, gw, x1, w_gate, w_up, w_down, g_final):
    T = h2.shape[0]
    rows = MOE_ROWS
    row = lambda n: pl.BlockSpec((rows, n), lambda i, e: (i, 0))
    return pl.pallas_call(
        _moe_kernel, grid=(T // rows, N_EXPERTS // MOE_EXPERTS_PER_STEP),
        in_specs=[row(D_MODEL), row(ROUTER_LANES), row(D_MODEL),
                  pl.BlockSpec((MOE_EXPERTS_PER_STEP, D_MODEL, D_EXPERT), lambda i, e: (e, 0, 0)),
                  pl.BlockSpec((MOE_EXPERTS_PER_STEP, D_MODEL, D_EXPERT), lambda i, e: (e, 0, 0)),
                  pl.BlockSpec((MOE_EXPERTS_PER_STEP, D_EXPERT, D_MODEL), lambda i, e: (e, 0, 0)),
                  pl.BlockSpec((1, D_MODEL), lambda i, e: (0, 0))],
        out_specs=row(D_MODEL),
        out_shape=jax.ShapeDtypeStruct((T, D_MODEL), F32),
        scratch_shapes=[pltpu.VMEM((rows, D_MODEL), F32)],
        compiler_params=pltpu.CompilerParams(
            dimension_semantics=("arbitrary", "arbitrary"), vmem_limit_bytes=VMEM_LIMIT_BYTES),
        name="moe_final",
    )(h2, gw, x1, w_gate, w_up, w_down, g_final)


def _rope_inputs(positions):
    half = HEAD_DIM // 2
    inv = 1.0 / (ROPE_THETA ** (jnp.arange(0, HEAD_DIM, 2, dtype=F32) / HEAD_DIM))
    return positions.astype(F32)[..., None], inv[jnp.arange(LANES) % half].reshape(1, LANES)


def _layer(x, pos, inv_freq, g_mix, w_in, w_dw, b_dw, ln_g, ln_b, w_conv_out, w_attn_out, w_o,
           g_ffn, w_rg, b_rg, w_re, b_re, w_gate, w_up, w_down, g_final):
    B, S, D = x.shape
    T = B * S
    offs = [0]
    for n in IN_SIZES:
        offs.append(offs[-1] + n)
    col = lambda j: w_in[:, offs[j]:offs[j + 1]]
    w_ki, w_wi = col(5), col(6)
    ws = (
        col(0), col(1), col(2), col(3), col(4),
        jnp.concatenate([w_ki, w_ki, jnp.pad(w_wi, ((0, 0), (0, LANES - IDX_HEADS)))], axis=1),
        col(7), col(8),
    )
    ws = tuple(w.astype(BF16) for w in ws)
    u, q, k, vt, qi, kk, wi, gc, ga = _inproj(x, g_mix.reshape(1, D), pos, inv_freq, ws)

    mconv = _conv_branch(u, gc, w_dw.reshape(CONV_KERNEL, CONV_WIDTH), b_dw.reshape(1, -1),
                         ln_g.reshape(1, -1), ln_b.reshape(1, -1), w_conv_out.astype(BF16))
    attn = _attention(qi, q, wi, kk, k, vt)

    n_r = N_GROUPS + N_EXPERTS
    w_r = jnp.concatenate([w_rg, w_re.reshape(D, N_EXPERTS)], axis=1)
    w_r = jnp.pad(w_r, ((0, 0), (0, ROUTER_LANES - n_r))).astype(BF16)
    b_r = jnp.pad(jnp.concatenate([b_rg, b_re.reshape(N_EXPERTS)]), (0, ROUTER_LANES - n_r)).reshape(1, -1)
    x1, h2, gw = _mix(x.reshape(T, D), mconv.reshape(T, D), attn.reshape(T, ATTN_WIDTH), ga.reshape(T, D),
                      w_attn_out.astype(BF16), w_o.astype(BF16), g_ffn.reshape(1, D), w_r, b_r)
    out = _moe(h2, gw, x1, w_gate, w_up, w_down, g_final.reshape(1, D))
    return out.reshape(B, S, D)


def kernel(x, positions, g_mix, w_in, w_dw, b_dw, ln_g, ln_b, w_conv_out, w_attn_out, w_o, g_ffn,
           w_rg, b_rg, w_re, b_re, w_gate, w_up, w_down, g_final):
    depth = g_mix.shape[0]
    assert depth == 1, "final norm is fused into the single layer's MoE call"
    pos, inv_freq = _rope_inputs(positions)
    return _layer(x, pos, inv_freq, g_mix[0], w_in[0], w_dw[0], b_dw[0], ln_g[0], ln_b[0],
                  w_conv_out[0], w_attn_out[0], w_o[0], g_ffn[0], w_rg[0], b_rg[0], w_re[0],
                  b_re[0], w_gate[0], w_up[0], w_down[0], g_final)
```

```python
import functools

import jax
import jax.numpy as jnp
from jax import lax
from jax.experimental import pallas as pl
from jax.experimental.pallas import tpu as pltpu

F32 = jnp.float32
BF16 = jnp.bfloat16

D_MODEL = 1024
CONV_WIDTH = 512
CONV_KERNEL = 31
N_HEADS = 8
HEAD_DIM = 64
ATTN_WIDTH = N_HEADS * HEAD_DIM
IDX_HEADS = 8
IDX_DIM = 64
TOPK_MAX = 256
ROPE_THETA = 10000.0
N_GROUPS = 4
EXPERTS_PER_GROUP = 4
N_EXPERTS = N_GROUPS * EXPERTS_PER_GROUP
D_EXPERT = 256
EPS = 1e-6
IN_SIZES = (2 * CONV_WIDTH, ATTN_WIDTH, ATTN_WIDTH, ATTN_WIDTH,
            IDX_HEADS * IDX_DIM, IDX_DIM, IDX_HEADS, D_MODEL, D_MODEL)

LANES = 128
SUBLANES = 8
VMEM_LIMIT_BYTES = 56 * 1024 * 1024

PROJ_ROWS = 512
CONV_ROWS = 512
CONV_HALO = 32
CONV_SUB = 64
ATTN_Q = 256
HEAD_GROUP = 4
ONES_ROWS = 16
V_ROWS = HEAD_DIM + ONES_ROWS
KEY_CHUNK = 512
COUNT_ROWS = 256
COUNT_ACC_ROWS = 32
ACC_ROWS = 64
HEAD_ACC_ROWS = 8
MIX_ROWS = 1024
MOE_ROWS = 1024
MOE_EXPERTS_PER_STEP = 2
ROUTER_LANES = 128
MASK_VALUE = -1e30
KEY16_MIN = -2 ** 15
PACK_ROWS = 16
STAGE2_BITS = 17
STAGE2_BELOW = 36000


def _rms_rows(x, g):
    ms = jnp.mean(x * x, axis=-1, keepdims=True)
    return x * lax.rsqrt(ms + EPS) * g


def _sigmoid(x):
    return jax.nn.sigmoid(x)


def _rope128(z, cos, sin, first_half):
    rot = jnp.where(first_half, pltpu.roll(z, LANES - HEAD_DIM // 2, 1),
                    pltpu.roll(z, HEAD_DIM // 2, 1))
    return z * cos + rot * sin


def _inproj_kernel(x_ref, g_ref, pos_ref, inv_ref, wu_ref, wq_ref, wk_ref, wv_ref, wqi_ref,
                   wkw_ref, wgc_ref, wga_ref,
                   u_ref, q_ref, k_ref, vt_ref, qi_ref, kk_ref, wi_ref, gc_ref, ga_ref):
    h = _rms_rows(x_ref[0], g_ref[...]).astype(BF16)
    ang = pos_ref[0] * inv_ref[...]
    lane = lax.broadcasted_iota(jnp.int32, ang.shape, 1)
    first_half = (lane % HEAD_DIM) < (HEAD_DIM // 2)
    cos = jnp.cos(ang)
    sin = jnp.where(first_half, -jnp.sin(ang), jnp.sin(ang))

    def proj(w_ref):
        return jnp.dot(h, w_ref[...], preferred_element_type=F32)

    u_ref[0] = proj(wu_ref)
    gc_ref[0] = proj(wgc_ref)
    ga_ref[0] = proj(wga_ref)
    for w_ref, o_ref in ((wq_ref, q_ref), (wk_ref, k_ref), (wqi_ref, qi_ref)):
        z = proj(w_ref)
        for c in range(ATTN_WIDTH // LANES):
            sl = slice(c * LANES, (c + 1) * LANES)
            o_ref[0, :, sl] = _rope128(z[:, sl], cos, sin, first_half).astype(BF16)
    kw = proj(wkw_ref)
    kk_ref[0] = _rope128(kw[:, :LANES], cos, sin, first_half).astype(BF16)
    idx_scale = (IDX_HEADS ** -0.5) * (IDX_DIM ** -0.5)
    wi_ref[0] = kw[:, LANES:] * idx_scale
    vt = proj(wv_ref).T.astype(BF16)
    ones = jnp.ones((ONES_ROWS, vt.shape[1]), BF16)
    vt_ref[0] = jnp.concatenate(
        [piece for hd in range(N_HEADS) for piece in (vt[hd * HEAD_DIM:(hd + 1) * HEAD_DIM], ones)], axis=0)


def _inproj(x, g_mix, pos, inv_freq, ws):
    B, S, D = x.shape
    rows = PROJ_ROWS
    grid = (B, S // rows)
    row_spec = lambda n: pl.BlockSpec((1, rows, n), lambda b, i: (b, i, 0))
    w_spec = lambda w: pl.BlockSpec(w.shape, lambda b, i: (0, 0), pipeline_mode=pl.Buffered(1))
    out_shape = (
        jax.ShapeDtypeStruct((B, S, 2 * CONV_WIDTH), F32),
        jax.ShapeDtypeStruct((B, S, ATTN_WIDTH), BF16),
        jax.ShapeDtypeStruct((B, S, ATTN_WIDTH), BF16),
        jax.ShapeDtypeStruct((B, N_HEADS * V_ROWS, S), BF16),
        jax.ShapeDtypeStruct((B, S, ATTN_WIDTH), BF16),
        jax.ShapeDtypeStruct((B, S, LANES), BF16),
        jax.ShapeDtypeStruct((B, S, LANES), F32),
        jax.ShapeDtypeStruct((B, S, D_MODEL), F32),
        jax.ShapeDtypeStruct((B, S, D_MODEL), F32),
    )
    out_specs = (
        row_spec(2 * CONV_WIDTH), row_spec(ATTN_WIDTH), row_spec(ATTN_WIDTH),
        pl.BlockSpec((1, N_HEADS * V_ROWS, rows), lambda b, i: (b, 0, i)),
        row_spec(ATTN_WIDTH), row_spec(LANES), row_spec(LANES), row_spec(D_MODEL), row_spec(D_MODEL),
    )
    in_specs = [row_spec(D), pl.BlockSpec((1, D), lambda b, i: (0, 0)), row_spec(1),
                pl.BlockSpec((1, LANES), lambda b, i: (0, 0))]
    in_specs += [w_spec(w) for w in ws]
    return pl.pallas_call(
        _inproj_kernel, grid=grid, in_specs=in_specs, out_specs=out_specs, out_shape=out_shape,
        compiler_params=pltpu.CompilerParams(
            dimension_semantics=("arbitrary", "arbitrary"), vmem_limit_bytes=VMEM_LIMIT_BYTES),
        name="inproj",
    )(x, g_mix, pos, inv_freq, *ws)


def _conv_kernel(u_ref, uh_ref, gc_ref, wdw_ref, bdw_ref, lng_ref, lnb_ref, wout_ref,
                 o_ref, g_buf, s_buf):
    i = pl.program_id(1)
    rows = u_ref.shape[1]
    uh = uh_ref[0]
    gh = uh[:, :CONV_WIDTH] * _sigmoid(uh[:, CONV_WIDTH:])
    g_buf[0, 0:CONV_HALO, :] = jnp.where(i > 0, gh, 0.0)
    um = u_ref[0]
    g_buf[0, CONV_HALO:CONV_HALO + rows, :] = um[:, :CONV_WIDTH] * _sigmoid(um[:, CONV_WIDTH:])
    span = CONV_HALO + rows - SUBLANES
    for r in range(1, SUBLANES):
        for base in range(0, span, CONV_SUB):
            n = min(CONV_SUB, span - base)
            g_buf[r, base:base + n, :] = g_buf[0, pl.ds(base + r, n), :]
    first = CONV_HALO - (CONV_KERNEL - 1)
    for rr in range(rows // CONV_SUB):
        acc = jnp.zeros((CONV_SUB, CONV_WIDTH), F32)
        for t in range(CONV_KERNEL):
            shift = (first + t) % SUBLANES
            row0 = rr * CONV_SUB + (first + t) - shift
            acc = acc + wdw_ref[t:t + 1, :] * g_buf[shift, row0:row0 + CONV_SUB, :]
        c = acc + bdw_ref[...]
        mu = jnp.mean(c, axis=-1, keepdims=True)
        d = c - mu
        var = jnp.mean(d * d, axis=-1, keepdims=True)
        n = d * lax.rsqrt(var + EPS) * lng_ref[...] + lnb_ref[...]
        s_buf[rr * CONV_SUB:(rr + 1) * CONV_SUB, :] = (n * _sigmoid(n)).astype(BF16)
    y = jnp.dot(s_buf[...], wout_ref[...], preferred_element_type=F32)
    o_ref[0] = _sigmoid(gc_ref[0]) * y


def _conv_branch(u, gc, w_dw, b_dw, ln_g, ln_b, w_out):
    B, S, _ = u.shape
    rows = CONV_ROWS
    halo_per_tile = rows // CONV_HALO
    grid = (B, S // rows)
    vec = lambda n: pl.BlockSpec((1, n), lambda b, i: (0, 0))
    return pl.pallas_call(
        _conv_kernel, grid=grid,
        in_specs=[
            pl.BlockSpec((1, rows, 2 * CONV_WIDTH), lambda b, i: (b, i, 0)),
            pl.BlockSpec((1, CONV_HALO, 2 * CONV_WIDTH),
                         lambda b, i: (b, jnp.maximum(i * halo_per_tile - 1, 0), 0)),
            pl.BlockSpec((1, rows, D_MODEL), lambda b, i: (b, i, 0)),
            pl.BlockSpec((CONV_KERNEL, CONV_WIDTH), lambda b, i: (0, 0)),
            vec(CONV_WIDTH), vec(CONV_WIDTH), vec(CONV_WIDTH),
            pl.BlockSpec((CONV_WIDTH, D_MODEL), lambda b, i: (0, 0)),
        ],
        out_specs=pl.BlockSpec((1, rows, D_MODEL), lambda b, i: (b, i, 0)),
        out_shape=jax.ShapeDtypeStruct((B, S, D_MODEL), F32),
        scratch_shapes=[pltpu.VMEM((SUBLANES, CONV_HALO + rows, CONV_WIDTH), F32),
                        pltpu.VMEM((rows, CONV_WIDTH), BF16)],
        compiler_params=pltpu.CompilerParams(
            dimension_semantics=("arbitrary", "arbitrary"), vmem_limit_bytes=VMEM_LIMIT_BYTES),
        name="conv_branch",
    )(u, u, gc, w_dw, b_dw, ln_g, ln_b, w_out)


def _pair_rhs(x_bf16, scale=None):
    x = x_bf16.astype(F32)
    if scale is not None:
        x = x * scale
    lane = lax.broadcasted_iota(jnp.int32, x.shape, 1)
    lo = jnp.where(lane < HEAD_DIM, x, 0.0)
    hi = jnp.where(lane >= HEAD_DIM, x, 0.0)
    return jnp.concatenate([lo, hi], axis=0).astype(BF16)


def _fold_rows(x, op, rows=ACC_ROWS):
    n = x.shape[0] // rows
    return op(x.reshape(n, rows, x.shape[1]), axis=0)


def _finish_rows(x, op):
    y = op(x.reshape(x.shape[0] // SUBLANES, SUBLANES, x.shape[1]), axis=0)
    return op(y, axis=0, keepdims=True)


def _chunk_loop(n, body, init):
    def quad(i, cr):
        for k in range(4):
            cr = body(4 * i + k, cr)
        return cr
    carry = lax.fori_loop(0, n // 4, quad, init)
    base = (n // 4) * 4
    carry = lax.cond(n % 4 >= 2, lambda cr: body(base + 1, body(base, cr)), lambda cr: cr, carry)
    return lax.cond(n % 2 == 1, lambda cr: body(n - 1, cr), lambda cr: cr, carry)


def _nt_dot(a, b):
    return lax.dot_general(a, b, (((1,), (1,)), ((), ())), preferred_element_type=F32)


def _key16_to_key32(k16):
    bits = lax.shift_left(jnp.where(k16 >= 0, k16, k16 ^ jnp.int32(0x7FFF)), 16)
    return jnp.where(bits >= 0, bits, bits ^ jnp.int32(0x7FFFFFFF))


def _key_to_f32(key):
    bits = jnp.where(key >= 0, key, key ^ jnp.int32(0x7FFFFFFF))
    return lax.bitcast_convert_type(bits, F32)


def _attn_kernel(qi_ref, q_ref, wi_ref, kk_ref, k_ref, vt_ref, o_ref,
                 sc_ref, sb_ref, xs_ref, acc_ref, out_ref):
    i = pl.program_id(1)
    kc = KEY_CHUNK
    n_chunks = ((i + 1) * ATTN_Q + kc - 1) // kc
    topk = float(TOPK_MAX)
    t_idx = i * ATTN_Q + lax.broadcasted_iota(jnp.int32, (1, ATTN_Q), 1)
    s_iota = lax.broadcasted_iota(jnp.int32, (kc, ATTN_Q), 0)
    n_pairs = N_HEADS // 2

    qi = qi_ref[0]
    idx_rhs = [_pair_rhs(qi[:, p * LANES:(p + 1) * LANES]) for p in range(IDX_HEADS // 2)]
    wi_t = wi_ref[0].T

    def score_body(c, carry):
        start = pl.multiple_of(c * kc, kc)
        kk = kk_ref[0, pl.ds(start, kc), :]
        s = jnp.zeros((kc, ATTN_Q), F32)
        for p in range(IDX_HEADS // 2):
            r = _nt_dot(kk, idx_rhs[p])
            s = s + wi_t[2 * p:2 * p + 1, :] * jnp.maximum(r[:, :ATTN_Q], 0.0)
            s = s + wi_t[2 * p + 1:2 * p + 2, :] * jnp.maximum(r[:, ATTN_Q:], 0.0)
        causal = (start + s_iota) <= t_idx
        s = jnp.where(causal, s, -jnp.inf)
        sc_ref[pl.ds(start, kc), :] = s
        sb_ref[pl.ds(start, kc), :] = s.astype(BF16)
        return carry

    _chunk_loop(n_chunks, score_body, 0)

    n_count = n_chunks * (kc // COUNT_ROWS)

    def count_ge(cand):
        def body(c, acc):
            start = pl.multiple_of(c * COUNT_ROWS, COUNT_ROWS)
            hit = jnp.where(sc_ref[pl.ds(start, COUNT_ROWS), :] >= cand, 1.0, 0.0)
            return acc + _fold_rows(hit, jnp.sum, COUNT_ACC_ROWS)
        acc = lax.fori_loop(0, n_count, body, jnp.zeros((COUNT_ACC_ROWS, ATTN_Q), F32))
        return _finish_rows(acc, jnp.sum)

    def count_ge_bf16(cand):
        def body(c, acc):
            start = pl.multiple_of(c * kc, kc)
            hit = jnp.where(sb_ref[pl.ds(start, kc), :] >= cand, jnp.ones((), BF16), jnp.zeros((), BF16))
            parts = [hit[r * PACK_ROWS:(r + 1) * PACK_ROWS] for r in range(kc // PACK_ROWS)]
            while len(parts) > 1:
                parts = [a + b for a, b in zip(parts[::2], parts[1::2])]
            return acc + parts[0].astype(F32)
        acc = lax.fori_loop(0, n_chunks, body, jnp.zeros((PACK_ROWS, ATTN_Q), F32))
        return _finish_rows(acc, jnp.sum)

    c0 = count_ge_bf16(jnp.zeros((1, ATTN_Q), BF16))
    k16_0 = jnp.where(c0 >= topk, jnp.int32(0), jnp.int32(KEY16_MIN))

    def bit16_body(j, k16):
        cand = k16 + lax.shift_left(jnp.int32(1), 14 - j)
        cnt = count_ge_bf16(_key_to_f32(_key16_to_key32(cand)).astype(BF16))
        return jnp.where(cnt >= topk, cand, k16)

    k16 = lax.fori_loop(0, 15, bit16_body, k16_0)
    low = _key16_to_key32(k16) - jnp.int32(STAGE2_BELOW)

    def bit_body(j, carry):
        key, cnt_key = carry
        cand = key + lax.shift_left(jnp.int32(1), STAGE2_BITS - 1 - j)
        cnt = count_ge(_key_to_f32(cand))
        take = cnt >= topk
        return jnp.where(take, cand, key), jnp.where(take, cnt, cnt_key)

    key, cnt_ge = lax.fori_loop(0, STAGE2_BITS, bit_body, (low, jnp.full((1, ATTN_Q), jnp.inf, F32)))
    thr = _key_to_f32(key)
    thr = jnp.where((k16 == KEY16_MIN) | (thr != thr), -jnp.inf, thr)
    excess = jnp.where((cnt_ge > topk) & (thr > -jnp.inf), 1.0, 0.0)
    has_excess = jnp.sum(excess) > 0.0

    @pl.when(has_excess)
    def _():
        r_i = lax.broadcasted_iota(jnp.int32, (kc, kc), 0)
        c_i = lax.broadcasted_iota(jnp.int32, (kc, kc), 1)
        strict_lower = jnp.where(c_i < r_i, 1.0, 0.0).astype(BF16)

        def count_gt(c, acc):
            start = pl.multiple_of(c * COUNT_ROWS, COUNT_ROWS)
            hit = jnp.where(sc_ref[pl.ds(start, COUNT_ROWS), :] > thr, 1.0, 0.0)
            return acc + _fold_rows(hit, jnp.sum, COUNT_ACC_ROWS)

        gt_acc = lax.fori_loop(0, n_count, count_gt, jnp.zeros((COUNT_ACC_ROWS, ATTN_Q), F32))
        need = topk - _finish_rows(gt_acc, jnp.sum)

        def body(c, before):
            start = pl.multiple_of(c * kc, kc)
            s = sc_ref[pl.ds(start, kc), :]
            tie = jnp.where(s == thr, 1.0, 0.0)
            rank = jnp.dot(strict_lower, tie.astype(BF16), preferred_element_type=F32) + before
            keep = (s > thr) | ((s == thr) & (rank < need))
            sc_ref[pl.ds(start, kc), :] = jnp.where(keep, jnp.inf, -jnp.inf)
            return before + jnp.sum(tie, axis=0, keepdims=True)
        lax.fori_loop(0, n_chunks, body, jnp.zeros((1, ATTN_Q), F32))

    q = q_ref[0]
    acc_ref[...] = jnp.zeros(acc_ref.shape, F32)
    neg = jnp.full((HEAD_ACC_ROWS, ATTN_Q), -jnp.inf, F32)
    n_groups = N_HEADS // HEAD_GROUP
    att_rhs = [_pair_rhs(q[:, p * LANES:(p + 1) * LANES], HEAD_DIM ** -0.5) for p in range(n_pairs)]

    def qk_part(start, g, mx):
        if g == 0:
            keep = (sc_ref[pl.ds(start, kc), :] >= thr) & ((start + s_iota) <= t_idx)
            bias = jnp.where(keep, 0.0, MASK_VALUE)
            sc_ref[pl.ds(start, kc), :] = bias
        else:
            bias = sc_ref[pl.ds(start, kc), :]
        out = []
        for p in range(g * HEAD_GROUP // 2, (g + 1) * HEAD_GROUP // 2):
            kp = k_ref[0, pl.ds(start, kc), p * LANES:(p + 1) * LANES]
            lg = _nt_dot(kp, att_rhs[p])
            for hh in range(2):
                hl = 2 * p + hh - g * HEAD_GROUP
                x = lg[:, hh * ATTN_Q:(hh + 1) * ATTN_Q] + bias
                xs_ref[g % 2, pl.ds(start, kc), hl * ATTN_Q:(hl + 1) * ATTN_Q] = x
                out.append(jnp.maximum(mx[hl], _fold_rows(x, jnp.max, HEAD_ACC_ROWS)))
        return tuple(out)

    def pv_part(start, g, m):
        for hl in range(HEAD_GROUP):
            h = g * HEAD_GROUP + hl
            pm = jnp.exp(xs_ref[g % 2, pl.ds(start, kc), hl * ATTN_Q:(hl + 1) * ATTN_Q] - m[hl])
            rows = slice(h * V_ROWS, (h + 1) * V_ROWS)
            acc_ref[rows, :] += jnp.dot(vt_ref[0, rows, pl.ds(start, kc)], pm.astype(BF16),
                                        preferred_element_type=F32)

    m = None
    for s in range(n_groups + 1):
        def body(c, mx, s=s, m=m):
            start = pl.multiple_of(c * kc, kc)
            if s < n_groups:
                mx = qk_part(start, s, mx)
            if s > 0:
                pv_part(start, s - 1, m)
            return mx

        mx = _chunk_loop(n_chunks, body, (neg,) * HEAD_GROUP)
        if s < n_groups:
            m = [_finish_rows(mx[hl], jnp.max) for hl in range(HEAD_GROUP)]
    for h in range(N_HEADS):
        pv = acc_ref[h * V_ROWS:h * V_ROWS + HEAD_DIM, :]
        denom = acc_ref[h * V_ROWS + HEAD_DIM:h * V_ROWS + HEAD_DIM + 1, :]
        out_ref[h * HEAD_DIM:(h + 1) * HEAD_DIM, :] = pv / denom
    o_ref[0] = out_ref[...].T.astype(BF16)


def _attention(qi, q, wi, kk, k, vt):
    B, S, _ = q.shape
    grid = (B, S // ATTN_Q)
    blk = lambda n: pl.BlockSpec((1, ATTN_Q, n), lambda b, i: (b, i, 0))
    full = lambda n: pl.BlockSpec((1, S, n), lambda b, i: (b, 0, 0), pipeline_mode=pl.Buffered(1))
    return pl.pallas_call(
        _attn_kernel, grid=grid,
        in_specs=[blk(ATTN_WIDTH), blk(ATTN_WIDTH), blk(LANES), full(LANES), full(ATTN_WIDTH),
                  pl.BlockSpec((1, N_HEADS * V_ROWS, S), lambda b, i: (b, 0, 0), pipeline_mode=pl.Buffered(1))],
        out_specs=blk(ATTN_WIDTH),
        out_shape=jax.ShapeDtypeStruct((B, S, ATTN_WIDTH), BF16),
        scratch_shapes=[pltpu.VMEM((S, ATTN_Q), F32),
                        pltpu.VMEM((S, ATTN_Q), BF16),
                        pltpu.VMEM((2, S, HEAD_GROUP * ATTN_Q), F32),
                        pltpu.VMEM((N_HEADS * V_ROWS, ATTN_Q), F32),
                        pltpu.VMEM((ATTN_WIDTH, ATTN_Q), F32)],
        compiler_params=pltpu.CompilerParams(
            dimension_semantics=("arbitrary", "arbitrary"), vmem_limit_bytes=VMEM_LIMIT_BYTES),
        name="sparse_attn",
    )(qi, q, wi, kk, k, vt)


def _lane_first(mask, lane):
    return jnp.min(jnp.where(mask, lane, ROUTER_LANES), axis=-1, keepdims=True)


def _router(logits):
    lane = lax.broadcasted_iota(jnp.int32, logits.shape, 1)
    neg = -jnp.inf
    gl = jnp.where(lane < N_GROUPS, logits, neg)
    gmax = jnp.max(gl, axis=-1, keepdims=True)
    gsum = jnp.sum(jnp.exp(gl - gmax), axis=-1, keepdims=True)
    p_g = 1.0 / gsum
    gi = _lane_first(gl == gmax, lane)
    e_lo = N_GROUPS + gi * EXPERTS_PER_GROUP
    in_group = (lane >= e_lo) & (lane < e_lo + EXPERTS_PER_GROUP)
    el = jnp.where(in_group, logits, neg)
    emax = jnp.max(el, axis=-1, keepdims=True)
    ee = jnp.exp(el - emax)
    pe = ee / jnp.sum(ee, axis=-1, keepdims=True)
    pe = jnp.where(in_group, pe, -1.0)
    p1 = jnp.max(pe, axis=-1, keepdims=True)
    l1 = _lane_first(pe == p1, lane)
    pe2 = jnp.where(lane == l1, -1.0, pe)
    p2 = jnp.max(pe2, axis=-1, keepdims=True)
    l2 = _lane_first(pe2 == p2, lane)
    tot = p1 + p2
    c1 = p_g * (p1 / tot)
    c2 = p_g * (p2 / tot)
    return jnp.where(lane == l1, c1, 0.0) + jnp.where(lane == l2, c2, 0.0)


def _mix_kernel(x_ref, mconv_ref, attn_ref, ga_ref, wao_ref, wo_ref, gffn_ref, wr_ref, br_ref,
                x1_ref, h2_ref, gw_ref):
    ya = jnp.dot(attn_ref[...], wao_ref[...], preferred_element_type=F32)
    m = mconv_ref[...] + _sigmoid(ga_ref[...]) * ya
    x1 = x_ref[...] + jnp.dot(m.astype(BF16), wo_ref[...], preferred_element_type=F32)
    x1_ref[...] = x1
    h2 = _rms_rows(x1, gffn_ref[...]).astype(BF16)
    h2_ref[...] = h2
    logits = jnp.dot(h2, wr_ref[...], preferred_element_type=F32) + br_ref[...]
    gw_ref[...] = _router(logits)


def _mix(x, mconv, attn, ga, w_ao, w_o, g_ffn, w_r, b_r):
    T = x.shape[0]
    rows = MIX_ROWS
    row = lambda n: pl.BlockSpec((rows, n), lambda i: (i, 0))
    const = lambda a: pl.BlockSpec(a.shape, lambda i: (0, 0))
    return pl.pallas_call(
        _mix_kernel, grid=(T // rows,),
        in_specs=[row(D_MODEL), row(D_MODEL), row(ATTN_WIDTH), row(D_MODEL),
                  const(w_ao), const(w_o), const(g_ffn), const(w_r), const(b_r)],
        out_specs=(row(D_MODEL), row(D_MODEL), row(ROUTER_LANES)),
        out_shape=(jax.ShapeDtypeStruct((T, D_MODEL), F32),
                   jax.ShapeDtypeStruct((T, D_MODEL), BF16),
                   jax.ShapeDtypeStruct((T, ROUTER_LANES), F32)),
        compiler_params=pltpu.CompilerParams(
            dimension_semantics=("arbitrary",), vmem_limit_bytes=VMEM_LIMIT_BYTES),
        name="mix_router",
    )(x, mconv, attn, ga, w_ao, w_o, g_ffn, w_r, b_r)


def _moe_kernel(h2_ref, gw_ref, x1_ref, wg_ref, wu_ref, wd_ref, gfin_ref, o_ref, acc_ref):
    e = pl.program_id(1)

    @pl.when(e == 0)
    def _():
        acc_ref[...] = jnp.zeros(acc_ref.shape, F32)

    h2 = h2_ref[...]
    gw = gw_ref[...]
    lane = lax.broadcasted_iota(jnp.int32, gw.shape, 1)
    y = None
    for j in range(MOE_EXPERTS_PER_STEP):
        gate = jnp.dot(h2, wg_ref[j].astype(BF16), preferred_element_type=F32)
        up = jnp.dot(h2, wu_ref[j].astype(BF16), preferred_element_type=F32)
        expert_lane = e * MOE_EXPERTS_PER_STEP + j + N_GROUPS
        gwe = jnp.sum(jnp.where(lane == expert_lane, gw, 0.0), axis=-1, keepdims=True)
        hid = (gate * _sigmoid(gate)) * up * gwe
        yj = jnp.dot(hid.astype(BF16), wd_ref[j].astype(BF16), preferred_element_type=F32)
        y = yj if y is None else y + yj
    acc_ref[...] += y

    @pl.when(e == N_EXPERTS // MOE_EXPERTS_PER_STEP - 1)
    def _():
        o_ref[...] = _rms_rows(x1_ref[...] + acc_ref[...], gfin_ref[...])


def _moe(h2, gw, x1, w_gate, w_up, w_down, g_final):
    T = h2.shape[0]
    rows = MOE_ROWS
    row = lambda n: pl.BlockSpec((rows, n), lambda i, e: (i, 0))
    return pl.pallas_call(
        _moe_kernel, grid=(T // rows, N_EXPERTS // MOE_EXPERTS_PER_STEP),
        in_specs=[row(D_MODEL), row(ROUTER_LANES), row(D_MODEL),
                  pl.BlockSpec((MOE_EXPERTS_PER_STEP, D_MODEL, D_EXPERT), lambda i, e: (e, 0, 0)),
                  pl.BlockSpec((MOE_EXPERTS_PER_STEP, D_MODEL, D_EXPERT), lambda i, e: (e, 0, 0)),
                  pl.BlockSpec((MOE_EXPERTS_PER_STEP, D_EXPERT, D_MODEL), lambda i, e: (e, 0, 0)),
                  pl.BlockSpec((1, D_MODEL), lambda i, e: (0, 0))],
        out_specs=row(D_MODEL),
        out_shape=jax.ShapeDtypeStruct((T, D_MODEL), F32),
        scratch_shapes=[pltpu.VMEM((rows, D_MODEL), F32)],
        compiler_params=pltpu.CompilerParams(
            dimension_semantics=("arbitrary", "arbitrary"), vmem_limit_bytes=VMEM_LIMIT_BYTES),
        name="moe_final",
    )(h2, gw, x1, w_gate, w_up, w_down, g_final)


def _rope_inputs(positions):
    half = HEAD_DIM // 2
    inv = 1.0 / (ROPE_THETA ** (jnp.arange(0, HEAD_DIM, 2, dtype=F32) / HEAD_DIM))
    return positions.astype(F32)[..., None], inv[jnp.arange(LANES) % half].reshape(1, LANES)


def _layer(x, pos, inv_freq, g_mix, w_in, w_dw, b_dw, ln_g, ln_b, w_conv_out, w_attn_out, w_o,
           g_ffn, w_rg, b_rg, w_re, b_re, w_gate, w_up, w_down, g_final):
    B, S, D = x.shape
    T = B * S
    offs = [0]
    for n in IN_SIZES:
        offs.append(offs[-1] + n)
    col = lambda j: w_in[:, offs[j]:offs[j + 1]]
    w_ki, w_wi = col(5), col(6)
    ws = (
        col(0), col(1), col(2), col(3), col(4),
        jnp.concatenate([w_ki, w_ki, jnp.pad(w_wi, ((0, 0), (0, LANES - IDX_HEADS)))], axis=1),
        col(7), col(8),
    )
    ws = tuple(w.astype(BF16) for w in ws)
    u, q, k, vt, qi, kk, wi, gc, ga = _inproj(x, g_mix.reshape(1, D), pos, inv_freq, ws)

    mconv = _conv_branch(u, gc, w_dw.reshape(CONV_KERNEL, CONV_WIDTH), b_dw.reshape(1, -1),
                         ln_g.reshape(1, -1), ln_b.reshape(1, -1), w_conv_out.astype(BF16))
    attn = _attention(qi, q, wi, kk, k, vt)

    n_r = N_GROUPS + N_EXPERTS
    w_r = jnp.concatenate([w_rg, w_re.reshape(D, N_EXPERTS)], axis=1)
    w_r = jnp.pad(w_r, ((0, 0), (0, ROUTER_LANES - n_r))).astype(BF16)
    b_r = jnp.pad(jnp.concatenate([b_rg, b_re.reshape(N_EXPERTS)]), (0, ROUTER_LANES - n_r)).reshape(1, -1)
    x1, h2, gw = _mix(x.reshape(T, D), mconv.reshape(T, D), attn.reshape(T, ATTN_WIDTH), ga.reshape(T, D),
                      w_attn_out.astype(BF16), w_o.astype(BF16), g_ffn.reshape(1, D), w_r, b_r)
    out = _moe(h2, gw, x1, w_gate, w_up, w_down, g_final.reshape(1, D))
    return out.reshape(B, S, D)


def kernel(x, positions, g_mix, w_in, w_dw, b_dw, ln_g, ln_b, w_conv_out, w_attn_out, w_o, g_ffn,
           w_rg, b_rg, w_re, b_re, w_gate, w_up, w_down, g_final):
    depth = g_mix.shape[0]
    assert depth == 1, "final norm is fused into the single layer's MoE call"
    pos, inv_freq = _rope_inputs(positions)
    return _layer(x, pos, inv_freq, g_mix[0], w_in[0], w_dw[0], b_dw[0], ln_g[0], ln_b[0],
                  w_conv_out[0], w_attn_out[0], w_o[0], g_ffn[0], w_rg[0], b_rg[0], w_re[0],
                  b_re[0], w_gate[0], w_up[0], w_down[0], g_final)
```

```python
import jax
import jax.numpy as jnp
from jax import lax
from jax.experimental import pallas as pl
from jax.experimental.pallas import tpu as pltpu

F32 = jnp.float32
BF16 = jnp.bfloat16

D_MODEL = 1024
CONV_WIDTH = 512
CONV_KERNEL = 31
N_HEADS = 8
HEAD_DIM = 64
ATTN_WIDTH = N_HEADS * HEAD_DIM
IDX_HEADS = 8
IDX_DIM = 64
TOPK_MAX = 256
ROPE_THETA = 10000.0
N_GROUPS = 4
EXPERTS_PER_GROUP = 4
N_EXPERTS = N_GROUPS * EXPERTS_PER_GROUP
D_EXPERT = 256
EPS = 1e-6
IN_SIZES = (2 * CONV_WIDTH, ATTN_WIDTH, ATTN_WIDTH, ATTN_WIDTH,
            IDX_HEADS * IDX_DIM, IDX_DIM, IDX_HEADS, D_MODEL, D_MODEL)

LANES = 128
SUBLANES = 8
VMEM_LIMIT_BYTES = 56 * 1024 * 1024

PROJ_ROWS = 512
CONV_ROWS = 512
CONV_HALO = 32
CONV_SUB = 64
ATTN_Q = 256
HEAD_GROUP = 4
ONES_ROWS = 16
V_ROWS = HEAD_DIM + ONES_ROWS
KEY_CHUNK = 512
COUNT_ROWS = 256
COUNT_ACC_ROWS = 32
HEAD_ACC_ROWS = 8
MIX_ROWS = 1024
MOE_ROWS = 1024
MOE_EXPERTS_PER_STEP = 2
ROUTER_LANES = 128
MASK_VALUE = -1e30
KEY16_MIN = -2 ** 15
PACK_ROWS = 16
STAGE1_BITS = 15
STAGE2_BITS = 17
STAGE2_BELOW = 36000


def _rms_rows(x, g):
    ms = jnp.mean(x * x, axis=-1, keepdims=True)
    return x * lax.rsqrt(ms + EPS) * g


def _sigmoid(x):
    return jax.nn.sigmoid(x)


def _rope128(z, cos, sin, first_half):
    rot = jnp.where(first_half, pltpu.roll(z, LANES - HEAD_DIM // 2, 1),
                    pltpu.roll(z, HEAD_DIM // 2, 1))
    return z * cos + rot * sin


def _inproj_kernel(x_ref, g_ref, pos_ref, inv_ref, wu_ref, wq_ref, wk_ref, wv_ref, wqi_ref,
                   wkw_ref, wgc_ref, wga_ref,
                   u_ref, q_ref, k_ref, vt_ref, qi_ref, kk_ref, wi_ref, gc_ref, ga_ref):
    h = _rms_rows(x_ref[0], g_ref[...]).astype(BF16)
    ang = pos_ref[0] * inv_ref[...]
    lane = lax.broadcasted_iota(jnp.int32, ang.shape, 1)
    first_half = (lane % HEAD_DIM) < (HEAD_DIM // 2)
    cos = jnp.cos(ang)
    sin = jnp.sin(ang)
    sin = jnp.where(first_half, -sin, sin)

    def proj(w_ref):
        return jnp.dot(h, w_ref[...], preferred_element_type=F32)

    u_ref[0] = proj(wu_ref)
    gc_ref[0] = proj(wgc_ref)
    ga_ref[0] = proj(wga_ref)
    for w_ref, o_ref in ((wq_ref, q_ref), (wk_ref, k_ref), (wqi_ref, qi_ref)):
        z = proj(w_ref)
        for c in range(ATTN_WIDTH // LANES):
            sl = slice(c * LANES, (c + 1) * LANES)
            o_ref[0, :, sl] = _rope128(z[:, sl], cos, sin, first_half).astype(BF16)
    kw = proj(wkw_ref)
    kk_ref[0] = _rope128(kw[:, :LANES], cos, sin, first_half).astype(BF16)
    idx_scale = (IDX_HEADS ** -0.5) * (IDX_DIM ** -0.5)
    wi_ref[0] = kw[:, LANES:] * idx_scale
    vt = proj(wv_ref).T.astype(BF16)
    ones = jnp.ones((ONES_ROWS, vt.shape[1]), BF16)
    vt_ref[0] = jnp.concatenate(
        [piece for hd in range(N_HEADS) for piece in (vt[hd * HEAD_DIM:(hd + 1) * HEAD_DIM], ones)], axis=0)


def _inproj(x, g_mix, pos, inv_freq, ws):
    B, S, D = x.shape
    rows = PROJ_ROWS
    grid = (B, S // rows)
    row_spec = lambda n: pl.BlockSpec((1, rows, n), lambda b, i: (b, i, 0))
    w_spec = lambda w: pl.BlockSpec(w.shape, lambda b, i: (0, 0), pipeline_mode=pl.Buffered(1))
    out_shape = (
        jax.ShapeDtypeStruct((B, S, 2 * CONV_WIDTH), F32),
        jax.ShapeDtypeStruct((B, S, ATTN_WIDTH), BF16),
        jax.ShapeDtypeStruct((B, S, ATTN_WIDTH), BF16),
        jax.ShapeDtypeStruct((B, N_HEADS * V_ROWS, S), BF16),
        jax.ShapeDtypeStruct((B, S, ATTN_WIDTH), BF16),
        jax.ShapeDtypeStruct((B, S, LANES), BF16),
        jax.ShapeDtypeStruct((B, S, LANES), F32),
        jax.ShapeDtypeStruct((B, S, D_MODEL), F32),
        jax.ShapeDtypeStruct((B, S, D_MODEL), F32),
    )
    out_specs = (
        row_spec(2 * CONV_WIDTH), row_spec(ATTN_WIDTH), row_spec(ATTN_WIDTH),
        pl.BlockSpec((1, N_HEADS * V_ROWS, rows), lambda b, i: (b, 0, i)),
        row_spec(ATTN_WIDTH), row_spec(LANES), row_spec(LANES), row_spec(D_MODEL), row_spec(D_MODEL),
    )
    in_specs = [row_spec(D), pl.BlockSpec((1, D), lambda b, i: (0, 0)), row_spec(1),
                pl.BlockSpec((1, LANES), lambda b, i: (0, 0))]
    in_specs += [w_spec(w) for w in ws]
    return pl.pallas_call(
        _inproj_kernel, grid=grid, in_specs=in_specs, out_specs=out_specs, out_shape=out_shape,
        compiler_params=pltpu.CompilerParams(
            dimension_semantics=("arbitrary", "arbitrary"), vmem_limit_bytes=VMEM_LIMIT_BYTES),
        name="inproj",
    )(x, g_mix, pos, inv_freq, *ws)


def _conv_kernel(u_ref, uh_ref, gc_ref, wdw_ref, bdw_ref, lng_ref, lnb_ref, wout_ref,
                 o_ref, g_buf, s_buf):
    i = pl.program_id(1)
    rows = u_ref.shape[1]
    uh = uh_ref[0]
    gh = uh[:, :CONV_WIDTH] * _sigmoid(uh[:, CONV_WIDTH:])
    g_buf[0, 0:CONV_HALO, :] = jnp.where(i > 0, gh, 0.0)
    um = u_ref[0]
    g_buf[0, CONV_HALO:CONV_HALO + rows, :] = um[:, :CONV_WIDTH] * _sigmoid(um[:, CONV_WIDTH:])
    span = CONV_HALO + rows - SUBLANES
    for r in range(1, SUBLANES):
        for base in range(0, span, CONV_SUB):
            n = min(CONV_SUB, span - base)
            g_buf[r, base:base + n, :] = g_buf[0, pl.ds(base + r, n), :]
    first = CONV_HALO - (CONV_KERNEL - 1)
    for rr in range(rows // CONV_SUB):
        acc = jnp.zeros((CONV_SUB, CONV_WIDTH), F32)
        for t in range(CONV_KERNEL):
            shift = (first + t) % SUBLANES
            row0 = rr * CONV_SUB + (first + t) - shift
            acc = acc + wdw_ref[t:t + 1, :] * g_buf[shift, row0:row0 + CONV_SUB, :]
        c = acc + bdw_ref[...]
        mu = jnp.mean(c, axis=-1, keepdims=True)
        d = c - mu
        var = jnp.mean(d * d, axis=-1, keepdims=True)
        n = d * lax.rsqrt(var + EPS) * lng_ref[...] + lnb_ref[...]
        s_buf[rr * CONV_SUB:(rr + 1) * CONV_SUB, :] = (n * _sigmoid(n)).astype(BF16)
    y = jnp.dot(s_buf[...], wout_ref[...], preferred_element_type=F32)
    o_ref[0] = _sigmoid(gc_ref[0]) * y


def _conv_branch(u, gc, w_dw, b_dw, ln_g, ln_b, w_out):
    B, S, _ = u.shape
    rows = CONV_ROWS
    halo_per_tile = rows // CONV_HALO
    grid = (B, S // rows)
    vec = lambda n: pl.BlockSpec((1, n), lambda b, i: (0, 0))
    return pl.pallas_call(
        _conv_kernel, grid=grid,
        in_specs=[
            pl.BlockSpec((1, rows, 2 * CONV_WIDTH), lambda b, i: (b, i, 0)),
            pl.BlockSpec((1, CONV_HALO, 2 * CONV_WIDTH),
                         lambda b, i: (b, jnp.maximum(i * halo_per_tile - 1, 0), 0)),
            pl.BlockSpec((1, rows, D_MODEL), lambda b, i: (b, i, 0)),
            pl.BlockSpec((CONV_KERNEL, CONV_WIDTH), lambda b, i: (0, 0)),
            vec(CONV_WIDTH), vec(CONV_WIDTH), vec(CONV_WIDTH),
            pl.BlockSpec((CONV_WIDTH, D_MODEL), lambda b, i: (0, 0)),
        ],
        out_specs=pl.BlockSpec((1, rows, D_MODEL), lambda b, i: (b, i, 0)),
        out_shape=jax.ShapeDtypeStruct((B, S, D_MODEL), F32),
        scratch_shapes=[pltpu.VMEM((SUBLANES, CONV_HALO + rows, CONV_WIDTH), F32),
                        pltpu.VMEM((rows, CONV_WIDTH), BF16)],
        compiler_params=pltpu.CompilerParams(
            dimension_semantics=("arbitrary", "arbitrary"), vmem_limit_bytes=VMEM_LIMIT_BYTES),
        name="conv_branch",
    )(u, u, gc, w_dw, b_dw, ln_g, ln_b, w_out)


def _pair_rhs(x_bf16, scale=None):
    x = x_bf16.astype(F32)
    if scale is not None:
        x = x * scale
    lane = lax.broadcasted_iota(jnp.int32, x.shape, 1)
    lo = jnp.where(lane < HEAD_DIM, x, 0.0)
    hi = jnp.where(lane >= HEAD_DIM, x, 0.0)
    return jnp.concatenate([lo, hi], axis=0).astype(BF16)


def _fold_rows(x, op, rows):
    n = x.shape[0] // rows
    return op(x.reshape(n, rows, x.shape[1]), axis=0)


def _finish_rows(x, op):
    y = op(x.reshape(x.shape[0] // SUBLANES, SUBLANES, x.shape[1]), axis=0)
    return op(y, axis=0, keepdims=True)


def _chunk_loop(n, body, init):
    def quad(i, cr):
        for k in range(4):
            cr = body(4 * i + k, cr)
        return cr
    carry = lax.fori_loop(0, n // 4, quad, init)
    base = (n // 4) * 4
    carry = lax.cond(n % 4 >= 2, lambda cr: body(base + 1, body(base, cr)), lambda cr: cr, carry)
    return lax.cond(n % 2 == 1, lambda cr: body(n - 1, cr), lambda cr: cr, carry)


def _nt_dot(a, b):
    return lax.dot_general(a, b, (((1,), (1,)), ((), ())), preferred_element_type=F32)


def _key16_to_key32(k16):
    bits = lax.shift_left(jnp.where(k16 >= 0, k16, k16 ^ jnp.int32(0x7FFF)), 16)
    return jnp.where(bits >= 0, bits, bits ^ jnp.int32(0x7FFFFFFF))


def _key_to_f32(key):
    bits = jnp.where(key >= 0, key, key ^ jnp.int32(0x7FFFFFFF))
    return lax.bitcast_convert_type(bits, F32)


def _attn_kernel(qi_ref, q_ref, wi_ref, kk_ref, k_ref, vt_ref, o_ref,
                 sc_ref, sb_ref, xs_ref, acc_ref, out_ref):
    i = pl.program_id(1)
    kc = KEY_CHUNK
    n_chunks = ((i + 1) * ATTN_Q + kc - 1) // kc
    topk = float(TOPK_MAX)
    t_idx = i * ATTN_Q + lax.broadcasted_iota(jnp.int32, (1, ATTN_Q), 1)
    s_iota = lax.broadcasted_iota(jnp.int32, (kc, ATTN_Q), 0)
    n_pairs = N_HEADS // 2

    qi = qi_ref[0]
    idx_rhs = [_pair_rhs(qi[:, p * LANES:(p + 1) * LANES]) for p in range(IDX_HEADS // 2)]
    wi_t = wi_ref[0].T

    def score_body(c, carry):
        start = pl.multiple_of(c * kc, kc)
        kk = kk_ref[0, pl.ds(start, kc), :]
        s = jnp.zeros((kc, ATTN_Q), F32)
        for p in range(IDX_HEADS // 2):
            r = _nt_dot(kk, idx_rhs[p])
            s = s + wi_t[2 * p:2 * p + 1, :] * jnp.maximum(r[:, :ATTN_Q], 0.0)
            s = s + wi_t[2 * p + 1:2 * p + 2, :] * jnp.maximum(r[:, ATTN_Q:], 0.0)
        causal = (start + s_iota) <= t_idx
        s = jnp.where(causal, s, -jnp.inf)
        sc_ref[pl.ds(start, kc), :] = s
        sb_ref[pl.ds(start, kc), :] = s.astype(BF16)
        return carry

    _chunk_loop(n_chunks, score_body, 0)

    n_count = n_chunks * (kc // COUNT_ROWS)

    def count_ge(cand):
        def body(c, acc):
            start = pl.multiple_of(c * COUNT_ROWS, COUNT_ROWS)
            hit = jnp.where(sc_ref[pl.ds(start, COUNT_ROWS), :] >= cand, 1.0, 0.0)
            return acc + _fold_rows(hit, jnp.sum, COUNT_ACC_ROWS)
        acc = lax.fori_loop(0, n_count, body, jnp.zeros((COUNT_ACC_ROWS, ATTN_Q), F32))
        return _finish_rows(acc, jnp.sum)

    def count_ge_bf16(cand):
        def body(c, acc):
            start = pl.multiple_of(c * kc, kc)
            hit = jnp.where(sb_ref[pl.ds(start, kc), :] >= cand, jnp.ones((), BF16), jnp.zeros((), BF16))
            parts = [hit[r * PACK_ROWS:(r + 1) * PACK_ROWS] for r in range(kc // PACK_ROWS)]
            while len(parts) > 1:
                parts = [a + b for a, b in zip(parts[::2], parts[1::2])]
            return acc + parts[0].astype(F32)
        acc = lax.fori_loop(0, n_chunks, body, jnp.zeros((PACK_ROWS, ATTN_Q), F32))
        return _finish_rows(acc, jnp.sum)

    c0 = count_ge_bf16(jnp.zeros((1, ATTN_Q), BF16))
    k16_0 = jnp.where(c0 >= topk, jnp.int32(0), jnp.int32(KEY16_MIN))

    def bit16_body(j, k16):
        cand = k16 + lax.shift_left(jnp.int32(1), STAGE1_BITS - 1 - j)
        cnt = count_ge_bf16(_key_to_f32(_key16_to_key32(cand)).astype(BF16))
        return jnp.where(cnt >= topk, cand, k16)

    k16 = lax.fori_loop(0, STAGE1_BITS, bit16_body, k16_0)
    low = _key16_to_key32(k16) - jnp.int32(STAGE2_BELOW)

    def bit_body(j, carry):
        key, cnt_key = carry
        cand = key + lax.shift_left(jnp.int32(1), STAGE2_BITS - 1 - j)
        cnt = count_ge(_key_to_f32(cand))
        take = cnt >= topk
        return jnp.where(take, cand, key), jnp.where(take, cnt, cnt_key)

    key, cnt_ge = lax.fori_loop(0, STAGE2_BITS, bit_body, (low, jnp.full((1, ATTN_Q), jnp.inf, F32)))
    thr = _key_to_f32(key)
    thr = jnp.where((k16 == KEY16_MIN) | (thr != thr), -jnp.inf, thr)
    excess = jnp.where((cnt_ge > topk) & (thr > -jnp.inf), 1.0, 0.0)
    has_excess = jnp.sum(excess) > 0.0

    @pl.when(has_excess)
    def _():
        r_i = lax.broadcasted_iota(jnp.int32, (kc, kc), 0)
        c_i = lax.broadcasted_iota(jnp.int32, (kc, kc), 1)
        strict_lower = jnp.where(c_i < r_i, 1.0, 0.0).astype(BF16)

        def count_gt(c, acc):
            start = pl.multiple_of(c * COUNT_ROWS, COUNT_ROWS)
            hit = jnp.where(sc_ref[pl.ds(start, COUNT_ROWS), :] > thr, 1.0, 0.0)
            return acc + _fold_rows(hit, jnp.sum, COUNT_ACC_ROWS)

        gt_acc = lax.fori_loop(0, n_count, count_gt, jnp.zeros((COUNT_ACC_ROWS, ATTN_Q), F32))
        need = topk - _finish_rows(gt_acc, jnp.sum)

        def body(c, before):
            start = pl.multiple_of(c * kc, kc)
            s = sc_ref[pl.ds(start, kc), :]
            tie = jnp.where(s == thr, 1.0, 0.0)
            rank = jnp.dot(strict_lower, tie.astype(BF16), preferred_element_type=F32) + before
            keep = (s > thr) | ((s == thr) & (rank < need))
            sc_ref[pl.ds(start, kc), :] = jnp.where(keep, jnp.inf, -jnp.inf)
            return before + jnp.sum(tie, axis=0, keepdims=True)
        lax.fori_loop(0, n_chunks, body, jnp.zeros((1, ATTN_Q), F32))

    q = q_ref[0]
    acc_ref[...] = jnp.zeros(acc_ref.shape, F32)
    neg = jnp.full((HEAD_ACC_ROWS, ATTN_Q), -jnp.inf, F32)
    n_groups = N_HEADS // HEAD_GROUP
    att_rhs = [_pair_rhs(q[:, p * LANES:(p + 1) * LANES], HEAD_DIM ** -0.5) for p in range(n_pairs)]

    def qk_part(start, g, mx):
        if g == 0:
            keep = (sc_ref[pl.ds(start, kc), :] >= thr) & ((start + s_iota) <= t_idx)
            bias = jnp.where(keep, 0.0, MASK_VALUE)
            sc_ref[pl.ds(start, kc), :] = bias
        else:
            bias = sc_ref[pl.ds(start, kc), :]
        out = []
        for p in range(g * HEAD_GROUP // 2, (g + 1) * HEAD_GROUP // 2):
            kp = k_ref[0, pl.ds(start, kc), p * LANES:(p + 1) * LANES]
            lg = _nt_dot(kp, att_rhs[p])
            for hh in range(2):
                hl = 2 * p + hh - g * HEAD_GROUP
                x = lg[:, hh * ATTN_Q:(hh + 1) * ATTN_Q] + bias
                xs_ref[g % 2, pl.ds(start, kc), hl * ATTN_Q:(hl + 1) * ATTN_Q] = x
                out.append(jnp.maximum(mx[hl], _fold_rows(x, jnp.max, HEAD_ACC_ROWS)))
        return tuple(out)

    def pv_part(start, g, m):
        for hl in range(HEAD_GROUP):
            h = g * HEAD_GROUP + hl
            pm = jnp.exp(xs_ref[g % 2, pl.ds(start, kc), hl * ATTN_Q:(hl + 1) * ATTN_Q] - m[hl])
            rows = slice(h * V_ROWS, (h + 1) * V_ROWS)
            acc_ref[rows, :] += jnp.dot(vt_ref[0, rows, pl.ds(start, kc)], pm.astype(BF16),
                                        preferred_element_type=F32)

    m = None
    for s in range(n_groups + 1):
        def body(c, mx, s=s, m=m):
            start = pl.multiple_of(c * kc, kc)
            if s < n_groups:
                mx = qk_part(start, s, mx)
            if s > 0:
                pv_part(start, s - 1, m)
            return mx

        mx = _chunk_loop(n_chunks, body, (neg,) * HEAD_GROUP)
        if s < n_groups:
            m = [_finish_rows(mx[hl], jnp.max) for hl in range(HEAD_GROUP)]
    for h in range(N_HEADS):
        pv = acc_ref[h * V_ROWS:h * V_ROWS + HEAD_DIM, :]
        denom = acc_ref[h * V_ROWS + HEAD_DIM:h * V_ROWS + HEAD_DIM + 1, :]
        out_ref[h * HEAD_DIM:(h + 1) * HEAD_DIM, :] = pv / denom
    o_ref[0] = out_ref[...].T.astype(BF16)


def _attention(qi, q, wi, kk, k, vt):
    B, S, _ = q.shape
    grid = (B, S // ATTN_Q)
    blk = lambda n: pl.BlockSpec((1, ATTN_Q, n), lambda b, i: (b, i, 0))
    full = lambda n: pl.BlockSpec((1, S, n), lambda b, i: (b, 0, 0), pipeline_mode=pl.Buffered(1))
    return pl.pallas_call(
        _attn_kernel, grid=grid,
        in_specs=[blk(ATTN_WIDTH), blk(ATTN_WIDTH), blk(LANES), full(LANES), full(ATTN_WIDTH),
                  pl.BlockSpec((1, N_HEADS * V_ROWS, S), lambda b, i: (b, 0, 0), pipeline_mode=pl.Buffered(1))],
        out_specs=blk(ATTN_WIDTH),
        out_shape=jax.ShapeDtypeStruct((B, S, ATTN_WIDTH), BF16),
        scratch_shapes=[pltpu.VMEM((S, ATTN_Q), F32),
                        pltpu.VMEM((S, ATTN_Q), BF16),
                        pltpu.VMEM((2, S, HEAD_GROUP * ATTN_Q), F32),
                        pltpu.VMEM((N_HEADS * V_ROWS, ATTN_Q), F32),
                        pltpu.VMEM((ATTN_WIDTH, ATTN_Q), F32)],
        compiler_params=pltpu.CompilerParams(
            dimension_semantics=("arbitrary", "arbitrary"), vmem_limit_bytes=VMEM_LIMIT_BYTES),
        name="sparse_attn",
    )(qi, q, wi, kk, k, vt)


def _lane_first(mask, lane):
    return jnp.min(jnp.where(mask, lane, ROUTER_LANES), axis=-1, keepdims=True)


def _router(logits):
    lane = lax.broadcasted_iota(jnp.int32, logits.shape, 1)
    neg = -jnp.inf
    gl = jnp.where(lane < N_GROUPS, logits, neg)
    gmax = jnp.max(gl, axis=-1, keepdims=True)
    gsum = jnp.sum(jnp.exp(gl - gmax), axis=-1, keepdims=True)
    p_g = 1.0 / gsum
    gi = _lane_first(gl == gmax, lane)
    e_lo = N_GROUPS + gi * EXPERTS_PER_GROUP
    in_group = (lane >= e_lo) & (lane < e_lo + EXPERTS_PER_GROUP)
    el = jnp.where(in_group, logits, neg)
    emax = jnp.max(el, axis=-1, keepdims=True)
    ee = jnp.exp(el - emax)
    pe = ee / jnp.sum(ee, axis=-1, keepdims=True)
    pe = jnp.where(in_group, pe, -1.0)
    p1 = jnp.max(pe, axis=-1, keepdims=True)
    l1 = _lane_first(pe == p1, lane)
    pe2 = jnp.where(lane == l1, -1.0, pe)
    p2 = jnp.max(pe2, axis=-1, keepdims=True)
    l2 = _lane_first(pe2 == p2, lane)
    tot = p1 + p2
    c1 = p_g * (p1 / tot)
    c2 = p_g * (p2 / tot)
    return jnp.where(lane == l1, c1, 0.0) + jnp.where(lane == l2, c2, 0.0)


def _mix_kernel(x_ref, mconv_ref, attn_ref, ga_ref, wao_ref, wo_ref, gffn_ref, wr_ref, br_ref,
                x1_ref, h2_ref, gw_ref):
    ya = jnp.dot(attn_ref[...], wao_ref[...], preferred_element_type=F32)
    m = mconv_ref[...] + _sigmoid(ga_ref[...]) * ya
    x1 = x_ref[...] + jnp.dot(m.astype(BF16), wo_ref[...], preferred_element_type=F32)
    x1_ref[...] = x1
    h2 = _rms_rows(x1, gffn_ref[...]).astype(BF16)
    h2_ref[...] = h2
    logits = jnp.dot(h2, wr_ref[...], preferred_element_type=F32) + br_ref[...]
    gw_ref[...] = _router(logits)


def _mix(x, mconv, attn, ga, w_ao, w_o, g_ffn, w_r, b_r):
    T = x.shape[0]
    rows = MIX_ROWS
    row = lambda n: pl.BlockSpec((rows, n), lambda i: (i, 0))
    const = lambda a: pl.BlockSpec(a.shape, lambda i: (0, 0))
    return pl.pallas_call(
        _mix_kernel, grid=(T // rows,),
        in_specs=[row(D_MODEL), row(D_MODEL), row(ATTN_WIDTH), row(D_MODEL),
                  const(w_ao), const(w_o), const(g_ffn), const(w_r), const(b_r)],
        out_specs=(row(D_MODEL), row(D_MODEL), row(ROUTER_LANES)),
        out_shape=(jax.ShapeDtypeStruct((T, D_MODEL), F32),
                   jax.ShapeDtypeStruct((T, D_MODEL), BF16),
                   jax.ShapeDtypeStruct((T, ROUTER_LANES), F32)),
        compiler_params=pltpu.CompilerParams(
            dimension_semantics=("arbitrary",), vmem_limit_bytes=VMEM_LIMIT_BYTES),
        name="mix_router",
    )(x, mconv, attn, ga, w_ao, w_o, g_ffn, w_r, b_r)


def _moe_kernel(h2_ref, gw_ref, x1_ref, wg_ref, wu_ref, wd_ref, gfin_ref, o_ref, acc_ref):
    e = pl.program_id(1)

    @pl.when(e == 0)
    def _():
        acc_ref[...] = jnp.zeros(acc_ref.shape, F32)

    h2 = h2_ref[...]
    gw = gw_ref[...]
    lane = lax.broadcasted_iota(jnp.int32, gw.shape, 1)
    y = None
    for j in range(MOE_EXPERTS_PER_STEP):
        gate = jnp.dot(h2, wg_ref[j].astype(BF16), preferred_element_type=F32)
        up = jnp.dot(h2, wu_ref[j].astype(BF16), preferred_element_type=F32)
        expert_lane = e * MOE_EXPERTS_PER_STEP + j + N_GROUPS
        gwe = jnp.sum(jnp.where(lane == expert_lane, gw, 0.0), axis=-1, keepdims=True)
        hid = (gate * _sigmoid(gate)) * up * gwe
        yj = jnp.dot(hid.astype(BF16), wd_ref[j].astype(BF16), preferred_element_type=F32)
        y = yj if y is None else y + yj
    acc_ref[...] += y

    @pl.when(e == N_EXPERTS // MOE_EXPERTS_PER_STEP - 1)
    def _():
        o_ref[...] = _rms_rows(x1_ref[...] + acc_ref[...], gfin_ref[...])


def _moe(h2, gw, x1, w_gate, w_up, w_down, g_final):
    T = h2.shape[0]
    rows = MOE_ROWS
    row = lambda n: pl.BlockSpec((rows, n), lambda i, e: (i, 0))
    return pl.pallas_call(
        _moe_kernel, grid=(T // rows, N_EXPERTS // MOE_EXPERTS_PER_STEP),
        in_specs=[row(D_MODEL), row(ROUTER_LANES), row(D_MODEL),
                  pl.BlockSpec((MOE_EXPERTS_PER_STEP, D_MODEL, D_EXPERT), lambda i, e: (e, 0, 0)),
                  pl.BlockSpec((MOE_EXPERTS_PER_STEP, D_MODEL, D_EXPERT), lambda i, e: (e, 0, 0)),
                  pl.BlockSpec((MOE_EXPERTS_PER_STEP, D_EXPERT, D_MODEL), lambda i, e: (e, 0, 0)),
                  pl.BlockSpec((1, D_MODEL), lambda i, e: (0, 0))],
        out_specs=row(D_MODEL),
        out_shape=jax.ShapeDtypeStruct((T, D_MODEL), F32),
        scratch_shapes=[pltpu.VMEM((rows, D_MODEL), F32)],
        compiler_params=pltpu.CompilerParams(
            dimension_semantics=("arbitrary", "arbitrary"), vmem_limit_bytes=VMEM_LIMIT_BYTES),
        name="moe_final",
    )(h2, gw, x1, w_gate, w_up, w_down, g_final)


def _rope_inputs(positions):
    half = HEAD_DIM // 2
    inv = 1.0 / (ROPE_THETA ** (jnp.arange(0, HEAD_DIM, 2, dtype=F32) / HEAD_DIM))
    return positions.astype(F32)[..., None], inv[jnp.arange(LANES) % half].reshape(1, LANES)


def _layer(x, pos, inv_freq, g_mix, w_in, w_dw, b_dw, ln_g, ln_b, w_conv_out, w_attn_out, w_o,
           g_ffn, w_rg, b_rg, w_re, b_re, w_gate, w_up, w_down, g_final):
    B, S, D = x.shape
    T = B * S
    offs = [0]
    for n in IN_SIZES:
        offs.append(offs[-1] + n)
    col = lambda j: w_in[:, offs[j]:offs[j + 1]]
    w_ki, w_wi = col(5), col(6)
    ws = (
        col(0), col(1), col(2), col(3), col(4),
        jnp.concatenate([w_ki, w_ki, jnp.pad(w_wi, ((0, 0), (0, LANES - IDX_HEADS)))], axis=1),
        col(7), col(8),
    )
    ws = tuple(w.astype(BF16) for w in ws)
    u, q, k, vt, qi, kk, wi, gc, ga = _inproj(x, g_mix.reshape(1, D), pos, inv_freq, ws)

    mconv = _conv_branch(u, gc, w_dw.reshape(CONV_KERNEL, CONV_WIDTH), b_dw.reshape(1, -1),
                         ln_g.reshape(1, -1), ln_b.reshape(1, -1), w_conv_out.astype(BF16))
    attn = _attention(qi, q, wi, kk, k, vt)

    n_r = N_GROUPS + N_EXPERTS
    w_r = jnp.concatenate([w_rg, w_re.reshape(D, N_EXPERTS)], axis=1)
    w_r = jnp.pad(w_r, ((0, 0), (0, ROUTER_LANES - n_r))).astype(BF16)
    b_r = jnp.pad(jnp.concatenate([b_rg, b_re.reshape(N_EXPERTS)]), (0, ROUTER_LANES - n_r)).reshape(1, -1)
    x1, h2, gw = _mix(x.reshape(T, D), mconv.reshape(T, D), attn.reshape(T, ATTN_WIDTH), ga.reshape(T, D),
                      w_attn_out.astype(BF16), w_o.astype(BF16), g_ffn.reshape(1, D), w_r, b_r)
    out = _moe(h2, gw, x1, w_gate, w_up, w_down, g_final.reshape(1, D))
    return out.reshape(B, S, D)


def kernel(x, positions, g_mix, w_in, w_dw, b_dw, ln_g, ln_b, w_conv_out, w_attn_out, w_o, g_ffn,
           w_rg, b_rg, w_re, b_re, w_gate, w_up, w_down, g_final):
    depth = g_mix.shape[0]
    assert depth == 1, "final norm is fused into the single layer's MoE call"
    pos, inv_freq = _rope_inputs(positions)
    return _layer(x, pos, inv_freq, g_mix[0], w_in[0], w_dw[0], b_dw[0], ln_g[0], ln_b[0],
                  w_conv_out[0], w_attn_out[0], w_o[0], g_ffn[0], w_rg[0], b_rg[0], w_re[0],
                  b_re[0], w_gate[0], w_up[0], w_down[0], g_final)
```

```python
import jax
import jax.numpy as jnp
from jax import lax
from jax.experimental import pallas as pl
from jax.experimental.pallas import tpu as pltpu

F32 = jnp.float32
BF16 = jnp.bfloat16

D_MODEL = 1024
CONV_WIDTH = 512
CONV_KERNEL = 31
N_HEADS = 8
HEAD_DIM = 64
ATTN_WIDTH = N_HEADS * HEAD_DIM
IDX_HEADS = 8
IDX_DIM = 64
TOPK_MAX = 256
ROPE_THETA = 10000.0
N_GROUPS = 4
EXPERTS_PER_GROUP = 4
N_EXPERTS = N_GROUPS * EXPERTS_PER_GROUP
D_EXPERT = 256
EPS = 1e-6
IN_SIZES = (2 * CONV_WIDTH, ATTN_WIDTH, ATTN_WIDTH, ATTN_WIDTH,
            IDX_HEADS * IDX_DIM, IDX_DIM, IDX_HEADS, D_MODEL, D_MODEL)

LANES = 128
SUBLANES = 8
VMEM_LIMIT_BYTES = 56 * 1024 * 1024

PROJ_ROWS = 512
CONV_ROWS = 512
CONV_HALO = 32
CONV_SUB = 64
ATTN_Q = 256
HEAD_GROUP = 4
ONES_ROWS = 16
V_ROWS = HEAD_DIM + ONES_ROWS
KEY_CHUNK = 512
COUNT_ROWS = 256
COUNT_ACC_ROWS = 32
HEAD_ACC_ROWS = 8
MIX_ROWS = 1024
MOE_ROWS = 1024
MOE_SUB = 256
MOE_CAP = 96
ROUTER_LANES = 128
MASK_VALUE = -1e30
KEY16_MIN = -2 ** 15
PACK_ROWS = 16
STAGE1_BITS = 15
STAGE2_BITS = 17
STAGE2_BELOW = 36000


def _rms_rows(x, g):
    ms = jnp.mean(x * x, axis=-1, keepdims=True)
    return x * lax.rsqrt(ms + EPS) * g


def _sigmoid(x):
    return jax.nn.sigmoid(x)


def _rope128(z, cos, sin, first_half):
    rot = jnp.where(first_half, pltpu.roll(z, LANES - HEAD_DIM // 2, 1),
                    pltpu.roll(z, HEAD_DIM // 2, 1))
    return z * cos + rot * sin


def _inproj_kernel(x_ref, g_ref, pos_ref, inv_ref, wu_ref, wq_ref, wk_ref, wv_ref, wqi_ref,
                   wkw_ref, wgc_ref, wga_ref,
                   u_ref, q_ref, k_ref, vt_ref, qi_ref, kk_ref, wi_ref, gc_ref, ga_ref):
    h = _rms_rows(x_ref[0], g_ref[...]).astype(BF16)
    ang = pos_ref[0] * inv_ref[...]
    lane = lax.broadcasted_iota(jnp.int32, ang.shape, 1)
    first_half = (lane % HEAD_DIM) < (HEAD_DIM // 2)
    cos = jnp.cos(ang)
    sin = jnp.sin(ang)
    sin = jnp.where(first_half, -sin, sin)

    def proj(w_ref):
        return jnp.dot(h, w_ref[...], preferred_element_type=F32)

    u_ref[0] = proj(wu_ref)
    gc_ref[0] = proj(wgc_ref)
    ga_ref[0] = proj(wga_ref)
    for w_ref, o_ref in ((wq_ref, q_ref), (wk_ref, k_ref), (wqi_ref, qi_ref)):
        z = proj(w_ref)
        for c in range(ATTN_WIDTH // LANES):
            sl = slice(c * LANES, (c + 1) * LANES)
            o_ref[0, :, sl] = _rope128(z[:, sl], cos, sin, first_half).astype(BF16)
    kw = proj(wkw_ref)
    kk_ref[0] = _rope128(kw[:, :LANES], cos, sin, first_half).astype(BF16)
    idx_scale = (IDX_HEADS ** -0.5) * (IDX_DIM ** -0.5)
    wi_ref[0] = kw[:, LANES:] * idx_scale
    vt = proj(wv_ref).T.astype(BF16)
    ones = jnp.ones((ONES_ROWS, vt.shape[1]), BF16)
    vt_ref[0] = jnp.concatenate(
        [piece for hd in range(N_HEADS) for piece in (vt[hd * HEAD_DIM:(hd + 1) * HEAD_DIM], ones)], axis=0)


def _inproj(x, g_mix, pos, inv_freq, ws):
    B, S, D = x.shape
    rows = PROJ_ROWS
    grid = (B, S // rows)
    row_spec = lambda n: pl.BlockSpec((1, rows, n), lambda b, i: (b, i, 0))
    w_spec = lambda w: pl.BlockSpec(w.shape, lambda b, i: (0, 0), pipeline_mode=pl.Buffered(1))
    out_shape = (
        jax.ShapeDtypeStruct((B, S, 2 * CONV_WIDTH), F32),
        jax.ShapeDtypeStruct((B, S, ATTN_WIDTH), BF16),
        jax.ShapeDtypeStruct((B, S, ATTN_WIDTH), BF16),
        jax.ShapeDtypeStruct((B, N_HEADS * V_ROWS, S), BF16),
        jax.ShapeDtypeStruct((B, S, ATTN_WIDTH), BF16),
        jax.ShapeDtypeStruct((B, S, LANES), BF16),
        jax.ShapeDtypeStruct((B, S, LANES), F32),
        jax.ShapeDtypeStruct((B, S, D_MODEL), F32),
        jax.ShapeDtypeStruct((B, S, D_MODEL), F32),
    )
    out_specs = (
        row_spec(2 * CONV_WIDTH), row_spec(ATTN_WIDTH), row_spec(ATTN_WIDTH),
        pl.BlockSpec((1, N_HEADS * V_ROWS, rows), lambda b, i: (b, 0, i)),
        row_spec(ATTN_WIDTH), row_spec(LANES), row_spec(LANES), row_spec(D_MODEL), row_spec(D_MODEL),
    )
    in_specs = [row_spec(D), pl.BlockSpec((1, D), lambda b, i: (0, 0)), row_spec(1),
                pl.BlockSpec((1, LANES), lambda b, i: (0, 0))]
    in_specs += [w_spec(w) for w in ws]
    return pl.pallas_call(
        _inproj_kernel, grid=grid, in_specs=in_specs, out_specs=out_specs, out_shape=out_shape,
        compiler_params=pltpu.CompilerParams(
            dimension_semantics=("arbitrary", "arbitrary"), vmem_limit_bytes=VMEM_LIMIT_BYTES),
        name="inproj",
    )(x, g_mix, pos, inv_freq, *ws)


def _conv_kernel(u_ref, uh_ref, gc_ref, wdw_ref, bdw_ref, lng_ref, lnb_ref, wout_ref,
                 o_ref, g_buf, s_buf):
    i = pl.program_id(1)
    rows = u_ref.shape[1]
    uh = uh_ref[0]
    gh = uh[:, :CONV_WIDTH] * _sigmoid(uh[:, CONV_WIDTH:])
    g_buf[0, 0:CONV_HALO, :] = jnp.where(i > 0, gh, 0.0)
    um = u_ref[0]
    g_buf[0, CONV_HALO:CONV_HALO + rows, :] = um[:, :CONV_WIDTH] * _sigmoid(um[:, CONV_WIDTH:])
    span = CONV_HALO + rows - SUBLANES
    for r in range(1, SUBLANES):
        for base in range(0, span, CONV_SUB):
            n = min(CONV_SUB, span - base)
            g_buf[r, base:base + n, :] = g_buf[0, pl.ds(base + r, n), :]
    first = CONV_HALO - (CONV_KERNEL - 1)
    for rr in range(rows // CONV_SUB):
        acc = jnp.zeros((CONV_SUB, CONV_WIDTH), F32)
        for t in range(CONV_KERNEL):
            shift = (first + t) % SUBLANES
            row0 = rr * CONV_SUB + (first + t) - shift
            acc = acc + wdw_ref[t:t + 1, :] * g_buf[shift, row0:row0 + CONV_SUB, :]
        c = acc + bdw_ref[...]
        mu = jnp.mean(c, axis=-1, keepdims=True)
        d = c - mu
        var = jnp.mean(d * d, axis=-1, keepdims=True)
        n = d * lax.rsqrt(var + EPS) * lng_ref[...] + lnb_ref[...]
        s_buf[rr * CONV_SUB:(rr + 1) * CONV_SUB, :] = (n * _sigmoid(n)).astype(BF16)
    y = jnp.dot(s_buf[...], wout_ref[...], preferred_element_type=F32)
    o_ref[0] = _sigmoid(gc_ref[0]) * y


def _conv_branch(u, gc, w_dw, b_dw, ln_g, ln_b, w_out):
    B, S, _ = u.shape
    rows = CONV_ROWS
    halo_per_tile = rows // CONV_HALO
    grid = (B, S // rows)
    vec = lambda n: pl.BlockSpec((1, n), lambda b, i: (0, 0))
    return pl.pallas_call(
        _conv_kernel, grid=grid,
        in_specs=[
            pl.BlockSpec((1, rows, 2 * CONV_WIDTH), lambda b, i: (b, i, 0)),
            pl.BlockSpec((1, CONV_HALO, 2 * CONV_WIDTH),
                         lambda b, i: (b, jnp.maximum(i * halo_per_tile - 1, 0), 0)),
            pl.BlockSpec((1, rows, D_MODEL), lambda b, i: (b, i, 0)),
            pl.BlockSpec((CONV_KERNEL, CONV_WIDTH), lambda b, i: (0, 0)),
            vec(CONV_WIDTH), vec(CONV_WIDTH), vec(CONV_WIDTH),
            pl.BlockSpec((CONV_WIDTH, D_MODEL), lambda b, i: (0, 0)),
        ],
        out_specs=pl.BlockSpec((1, rows, D_MODEL), lambda b, i: (b, i, 0)),
        out_shape=jax.ShapeDtypeStruct((B, S, D_MODEL), F32),
        scratch_shapes=[pltpu.VMEM((SUBLANES, CONV_HALO + rows, CONV_WIDTH), F32),
                        pltpu.VMEM((rows, CONV_WIDTH), BF16)],
        compiler_params=pltpu.CompilerParams(
            dimension_semantics=("arbitrary", "arbitrary"), vmem_limit_bytes=VMEM_LIMIT_BYTES),
        name="conv_branch",
    )(u, u, gc, w_dw, b_dw, ln_g, ln_b, w_out)


def _pair_rhs(x_bf16, scale=None):
    x = x_bf16.astype(F32)
    if scale is not None:
        x = x * scale
    lane = lax.broadcasted_iota(jnp.int32, x.shape, 1)
    lo = jnp.where(lane < HEAD_DIM, x, 0.0)
    hi = jnp.where(lane >= HEAD_DIM, x, 0.0)
    return jnp.concatenate([lo, hi], axis=0).astype(BF16)


def _fold_rows(x, op, rows):
    n = x.shape[0] // rows
    return op(x.reshape(n, rows, x.shape[1]), axis=0)


def _finish_rows(x, op):
    y = op(x.reshape(x.shape[0] // SUBLANES, SUBLANES, x.shape[1]), axis=0)
    return op(y, axis=0, keepdims=True)


def _chunk_loop(n, body, init):
    def quad(i, cr):
        for k in range(4):
            cr = body(4 * i + k, cr)
        return cr
    carry = lax.fori_loop(0, n // 4, quad, init)
    base = (n // 4) * 4
    carry = lax.cond(n % 4 >= 2, lambda cr: body(base + 1, body(base, cr)), lambda cr: cr, carry)
    return lax.cond(n % 2 == 1, lambda cr: body(n - 1, cr), lambda cr: cr, carry)


def _nt_dot(a, b):
    return lax.dot_general(a, b, (((1,), (1,)), ((), ())), preferred_element_type=F32)


def _key16_to_key32(k16):
    bits = lax.shift_left(jnp.where(k16 >= 0, k16, k16 ^ jnp.int32(0x7FFF)), 16)
    return jnp.where(bits >= 0, bits, bits ^ jnp.int32(0x7FFFFFFF))


def _key_to_f32(key):
    bits = jnp.where(key >= 0, key, key ^ jnp.int32(0x7FFFFFFF))
    return lax.bitcast_convert_type(bits, F32)


def _attn_kernel(qi_ref, q_ref, wi_ref, kk_ref, k_ref, vt_ref, o_ref,
                 sc_ref, sb_ref, xs_ref, acc_ref, out_ref):
    i = pl.program_id(1)
    kc = KEY_CHUNK
    n_chunks = ((i + 1) * ATTN_Q + kc - 1) // kc
    topk = float(TOPK_MAX)
    t_idx = i * ATTN_Q + lax.broadcasted_iota(jnp.int32, (1, ATTN_Q), 1)
    s_iota = lax.broadcasted_iota(jnp.int32, (kc, ATTN_Q), 0)
    n_pairs = N_HEADS // 2

    qi = qi_ref[0]
    idx_rhs = [_pair_rhs(qi[:, p * LANES:(p + 1) * LANES]) for p in range(IDX_HEADS // 2)]
    wi_t = wi_ref[0].T

    def score_body(c, carry):
        start = pl.multiple_of(c * kc, kc)
        kk = kk_ref[0, pl.ds(start, kc), :]
        s = jnp.zeros((kc, ATTN_Q), F32)
        for p in range(IDX_HEADS // 2):
            r = _nt_dot(kk, idx_rhs[p])
            s = s + wi_t[2 * p:2 * p + 1, :] * jnp.maximum(r[:, :ATTN_Q], 0.0)
            s = s + wi_t[2 * p + 1:2 * p + 2, :] * jnp.maximum(r[:, ATTN_Q:], 0.0)
        causal = (start + s_iota) <= t_idx
        s = jnp.where(causal, s, -jnp.inf)
        sc_ref[pl.ds(start, kc), :] = s
        sb_ref[pl.ds(start, kc), :] = s.astype(BF16)
        return carry

    _chunk_loop(n_chunks, score_body, 0)

    n_count = n_chunks * (kc // COUNT_ROWS)

    def count_ge(cand):
        def body(c, acc):
            start = pl.multiple_of(c * COUNT_ROWS, COUNT_ROWS)
            hit = jnp.where(sc_ref[pl.ds(start, COUNT_ROWS), :] >= cand, 1.0, 0.0)
            return acc + _fold_rows(hit, jnp.sum, COUNT_ACC_ROWS)
        acc = lax.fori_loop(0, n_count, body, jnp.zeros((COUNT_ACC_ROWS, ATTN_Q), F32))
        return _finish_rows(acc, jnp.sum)

    def count_ge_bf16(cand):
        def body(c, acc):
            start = pl.multiple_of(c * kc, kc)
            hit = jnp.where(sb_ref[pl.ds(start, kc), :] >= cand, jnp.ones((), BF16), jnp.zeros((), BF16))
            parts = [hit[r * PACK_ROWS:(r + 1) * PACK_ROWS] for r in range(kc // PACK_ROWS)]
            while len(parts) > 1:
                parts = [a + b for a, b in zip(parts[::2], parts[1::2])]
            return acc + parts[0].astype(F32)
        acc = lax.fori_loop(0, n_chunks, body, jnp.zeros((PACK_ROWS, ATTN_Q), F32))
        return _finish_rows(acc, jnp.sum)

    c0 = count_ge_bf16(jnp.zeros((1, ATTN_Q), BF16))
    k16_0 = jnp.where(c0 >= topk, jnp.int32(0), jnp.int32(KEY16_MIN))

    def bit16_body(j, k16):
        cand = k16 + lax.shift_left(jnp.int32(1), STAGE1_BITS - 1 - j)
        cnt = count_ge_bf16(_key_to_f32(_key16_to_key32(cand)).astype(BF16))
        return jnp.where(cnt >= topk, cand, k16)

    k16 = lax.fori_loop(0, STAGE1_BITS, bit16_body, k16_0)
    low = _key16_to_key32(k16) - jnp.int32(STAGE2_BELOW)

    def bit_body(j, carry):
        key, cnt_key = carry
        cand = key + lax.shift_left(jnp.int32(1), STAGE2_BITS - 1 - j)
        cnt = count_ge(_key_to_f32(cand))
        take = cnt >= topk
        return jnp.where(take, cand, key), jnp.where(take, cnt, cnt_key)

    key, cnt_ge = lax.fori_loop(0, STAGE2_BITS, bit_body, (low, jnp.full((1, ATTN_Q), jnp.inf, F32)))
    thr = _key_to_f32(key)
    thr = jnp.where((k16 == KEY16_MIN) | (thr != thr), -jnp.inf, thr)
    excess = jnp.where((cnt_ge > topk) & (thr > -jnp.inf), 1.0, 0.0)
    has_excess = jnp.sum(excess) > 0.0

    @pl.when(has_excess)
    def _():
        r_i = lax.broadcasted_iota(jnp.int32, (kc, kc), 0)
        c_i = lax.broadcasted_iota(jnp.int32, (kc, kc), 1)
        strict_lower = jnp.where(c_i < r_i, 1.0, 0.0).astype(BF16)

        def count_gt(c, acc):
            start = pl.multiple_of(c * COUNT_ROWS, COUNT_ROWS)
            hit = jnp.where(sc_ref[pl.ds(start, COUNT_ROWS), :] > thr, 1.0, 0.0)
            return acc + _fold_rows(hit, jnp.sum, COUNT_ACC_ROWS)

        gt_acc = lax.fori_loop(0, n_count, count_gt, jnp.zeros((COUNT_ACC_ROWS, ATTN_Q), F32))
        need = topk - _finish_rows(gt_acc, jnp.sum)

        def body(c, before):
            start = pl.multiple_of(c * kc, kc)
            s = sc_ref[pl.ds(start, kc), :]
            tie = jnp.where(s == thr, 1.0, 0.0)
            rank = jnp.dot(strict_lower, tie.astype(BF16), preferred_element_type=F32) + before
            keep = (s > thr) | ((s == thr) & (rank < need))
            sc_ref[pl.ds(start, kc), :] = jnp.where(keep, jnp.inf, -jnp.inf)
            return before + jnp.sum(tie, axis=0, keepdims=True)
        lax.fori_loop(0, n_chunks, body, jnp.zeros((1, ATTN_Q), F32))

    q = q_ref[0]
    acc_ref[...] = jnp.zeros(acc_ref.shape, F32)
    neg = jnp.full((HEAD_ACC_ROWS, ATTN_Q), -jnp.inf, F32)
    n_groups = N_HEADS // HEAD_GROUP
    att_rhs = [_pair_rhs(q[:, p * LANES:(p + 1) * LANES], HEAD_DIM ** -0.5) for p in range(n_pairs)]

    def qk_part(start, g, mx):
        if g == 0:
            keep = (sc_ref[pl.ds(start, kc), :] >= thr) & ((start + s_iota) <= t_idx)
            bias = jnp.where(keep, 0.0, MASK_VALUE)
            sc_ref[pl.ds(start, kc), :] = bias
        else:
            bias = sc_ref[pl.ds(start, kc), :]
        out = []
        for p in range(g * HEAD_GROUP // 2, (g + 1) * HEAD_GROUP // 2):
            kp = k_ref[0, pl.ds(start, kc), p * LANES:(p + 1) * LANES]
            lg = _nt_dot(kp, att_rhs[p])
            for hh in range(2):
                hl = 2 * p + hh - g * HEAD_GROUP
                x = lg[:, hh * ATTN_Q:(hh + 1) * ATTN_Q] + bias
                xs_ref[g % 2, pl.ds(start, kc), hl * ATTN_Q:(hl + 1) * ATTN_Q] = x
                out.append(jnp.maximum(mx[hl], _fold_rows(x, jnp.max, HEAD_ACC_ROWS)))
        return tuple(out)

    def pv_part(start, g, m):
        for hl in range(HEAD_GROUP):
            h = g * HEAD_GROUP + hl
            pm = jnp.exp(xs_ref[g % 2, pl.ds(start, kc), hl * ATTN_Q:(hl + 1) * ATTN_Q] - m[hl])
            rows = slice(h * V_ROWS, (h + 1) * V_ROWS)
            acc_ref[rows, :] += jnp.dot(vt_ref[0, rows, pl.ds(start, kc)], pm.astype(BF16),
                                        preferred_element_type=F32)

    m = None
    for s in range(n_groups + 1):
        def body(c, mx, s=s, m=m):
            start = pl.multiple_of(c * kc, kc)
            if s < n_groups:
                mx = qk_part(start, s, mx)
            if s > 0:
                pv_part(start, s - 1, m)
            return mx

        mx = _chunk_loop(n_chunks, body, (neg,) * HEAD_GROUP)
        if s < n_groups:
            m = [_finish_rows(mx[hl], jnp.max) for hl in range(HEAD_GROUP)]
    for h in range(N_HEADS):
        pv = acc_ref[h * V_ROWS:h * V_ROWS + HEAD_DIM, :]
        denom = acc_ref[h * V_ROWS + HEAD_DIM:h * V_ROWS + HEAD_DIM + 1, :]
        out_ref[h * HEAD_DIM:(h + 1) * HEAD_DIM, :] = pv / denom
    o_ref[0] = out_ref[...].T.astype(BF16)


def _attention(qi, q, wi, kk, k, vt):
    B, S, _ = q.shape
    grid = (B, S // ATTN_Q)
    blk = lambda n: pl.BlockSpec((1, ATTN_Q, n), lambda b, i: (b, i, 0))
    full = lambda n: pl.BlockSpec((1, S, n), lambda b, i: (b, 0, 0), pipeline_mode=pl.Buffered(1))
    return pl.pallas_call(
        _attn_kernel, grid=grid,
        in_specs=[blk(ATTN_WIDTH), blk(ATTN_WIDTH), blk(LANES), full(LANES), full(ATTN_WIDTH),
                  pl.BlockSpec((1, N_HEADS * V_ROWS, S), lambda b, i: (b, 0, 0), pipeline_mode=pl.Buffered(1))],
        out_specs=blk(ATTN_WIDTH),
        out_shape=jax.ShapeDtypeStruct((B, S, ATTN_WIDTH), BF16),
        scratch_shapes=[pltpu.VMEM((S, ATTN_Q), F32),
                        pltpu.VMEM((S, ATTN_Q), BF16),
                        pltpu.VMEM((2, S, HEAD_GROUP * ATTN_Q), F32),
                        pltpu.VMEM((N_HEADS * V_ROWS, ATTN_Q), F32),
                        pltpu.VMEM((ATTN_WIDTH, ATTN_Q), F32)],
        compiler_params=pltpu.CompilerParams(
            dimension_semantics=("arbitrary", "arbitrary"), vmem_limit_bytes=VMEM_LIMIT_BYTES),
        name="sparse_attn",
    )(qi, q, wi, kk, k, vt)


def _lane_first(mask, lane):
    return jnp.min(jnp.where(mask, lane, ROUTER_LANES), axis=-1, keepdims=True)


def _router(logits):
    lane = lax.broadcasted_iota(jnp.int32, logits.shape, 1)
    neg = -jnp.inf
    gl = jnp.where(lane < N_GROUPS, logits, neg)
    gmax = jnp.max(gl, axis=-1, keepdims=True)
    gsum = jnp.sum(jnp.exp(gl - gmax), axis=-1, keepdims=True)
    p_g = 1.0 / gsum
    gi = _lane_first(gl == gmax, lane)
    e_lo = N_GROUPS + gi * EXPERTS_PER_GROUP
    in_group = (lane >= e_lo) & (lane < e_lo + EXPERTS_PER_GROUP)
    el = jnp.where(in_group, logits, neg)
    emax = jnp.max(el, axis=-1, keepdims=True)
    ee = jnp.exp(el - emax)
    pe = ee / jnp.sum(ee, axis=-1, keepdims=True)
    pe = jnp.where(in_group, pe, -1.0)
    p1 = jnp.max(pe, axis=-1, keepdims=True)
    l1 = _lane_first(pe == p1, lane)
    pe2 = jnp.where(lane == l1, -1.0, pe)
    p2 = jnp.max(pe2, axis=-1, keepdims=True)
    l2 = _lane_first(pe2 == p2, lane)
    tot = p1 + p2
    c1 = p_g * (p1 / tot)
    c2 = p_g * (p2 / tot)
    return (jnp.where(lane == l1, c1, 0.0) + jnp.where(lane == l2, c2, 0.0)
            + jnp.where(lane == 0, gi.astype(F32), 0.0))


def _mix_kernel(x_ref, mconv_ref, attn_ref, ga_ref, wao_ref, wo_ref, gffn_ref, wr_ref, br_ref,
                x1_ref, h2_ref, gw_ref):
    ya = jnp.dot(attn_ref[...], wao_ref[...], preferred_element_type=F32)
    m = mconv_ref[...] + _sigmoid(ga_ref[...]) * ya
    x1 = x_ref[...] + jnp.dot(m.astype(BF16), wo_ref[...], preferred_element_type=F32)
    x1_ref[...] = x1
    h2 = _rms_rows(x1, gffn_ref[...]).astype(BF16)
    h2_ref[...] = h2
    logits = jnp.dot(h2, wr_ref[...], preferred_element_type=F32) + br_ref[...]
    gw_ref[...] = _router(logits)


def _mix(x, mconv, attn, ga, w_ao, w_o, g_ffn, w_r, b_r):
    T = x.shape[0]
    rows = MIX_ROWS
    row = lambda n: pl.BlockSpec((rows, n), lambda i: (i, 0))
    const = lambda a: pl.BlockSpec(a.shape, lambda i: (0, 0))
    return pl.pallas_call(
        _mix_kernel, grid=(T // rows,),
        in_specs=[row(D_MODEL), row(D_MODEL), row(ATTN_WIDTH), row(D_MODEL),
                  const(w_ao), const(w_o), const(g_ffn), const(w_r), const(b_r)],
        out_specs=(row(D_MODEL), row(D_MODEL), row(ROUTER_LANES)),
        out_shape=(jax.ShapeDtypeStruct((T, D_MODEL), F32),
                   jax.ShapeDtypeStruct((T, D_MODEL), BF16),
                   jax.ShapeDtypeStruct((T, ROUTER_LANES), F32)),
        compiler_params=pltpu.CompilerParams(
            dimension_semantics=("arbitrary",), vmem_limit_bytes=VMEM_LIMIT_BYTES),
        name="mix_router",
    )(x, mconv, attn, ga, w_ao, w_o, g_ffn, w_r, b_r)


def _split3(x):
    a = x.astype(BF16)
    r = x - a.astype(F32)
    b = r.astype(BF16)
    return a, b, (r - b.astype(F32)).astype(BF16)


def _group_experts(h, gwv, g, wg_ref, wu_ref, wd_ref):
    lane = lax.broadcasted_iota(jnp.int32, gwv.shape, 1)
    y = None
    for j in range(EXPERTS_PER_GROUP):
        gate = jnp.dot(h, wg_ref[j], preferred_element_type=F32)
        up = jnp.dot(h, wu_ref[j], preferred_element_type=F32)
        expert_lane = N_GROUPS + g * EXPERTS_PER_GROUP + j
        gwe = jnp.sum(jnp.where(lane == expert_lane, gwv, 0.0), axis=-1, keepdims=True)
        hid = (gate * _sigmoid(gate)) * up * gwe
        yj = jnp.dot(hid.astype(BF16), wd_ref[j], preferred_element_type=F32)
        y = yj if y is None else y + yj
    return y


def _moe_kernel(h2_ref, gw_ref, x1_ref, wg_ref, wu_ref, wd_ref, gfin_ref, o_ref,
                hs_ref, gws_ref, ys_ref, acc_ref, dest_ref, flag_ref):
    g = pl.program_id(1)
    n_sub = MOE_ROWS // MOE_SUB
    seg = N_GROUPS * MOE_CAP

    @pl.when(g == 0)
    def _():
        r_i = lax.broadcasted_iota(jnp.int32, (MOE_SUB, MOE_SUB), 0)
        c_i = lax.broadcasted_iota(jnp.int32, (MOE_SUB, MOE_SUB), 1)
        tri = jnp.where(c_i < r_i, 1.0, 0.0).astype(BF16)
        over = jnp.zeros((1, 1), F32)
        for sub in range(n_sub):
            rows = slice(sub * MOE_SUB, (sub + 1) * MOE_SUB)
            gw = gw_ref[rows, :]
            lane = lax.broadcasted_iota(jnp.int32, gw.shape, 1)
            gi = jnp.sum(jnp.where(lane == 0, gw, 0.0), axis=-1, keepdims=True)
            onehot = jnp.where((lane < N_GROUPS) & (lane.astype(F32) == gi), 1.0, 0.0)
            before = jnp.dot(tri, onehot.astype(BF16), preferred_element_type=F32)
            rank = jnp.sum(onehot * before, axis=-1, keepdims=True)
            over = jnp.maximum(over, jnp.max(rank, axis=0, keepdims=True))
            dest = jnp.where(rank < MOE_CAP, gi * MOE_CAP + rank, -1.0)
            dest_col = jnp.where(lane == 0, dest, 0.0)
            dest_ref[rows, :] = dest_col
            dest_row = dest_col.T[0:1, :]
            slot = lax.broadcasted_iota(jnp.int32, (seg, 1), 0).astype(F32)
            gather = jnp.where(dest_row == slot, 1.0, 0.0).astype(BF16)
            hs_ref[sub] = jnp.dot(gather, h2_ref[rows, :], preferred_element_type=F32).astype(BF16)
            gws_ref[sub] = sum(jnp.dot(gather, p, preferred_element_type=F32) for p in _split3(gw))
        flag_ref[...] = jnp.broadcast_to(over, flag_ref.shape)
        acc_ref[...] = jnp.zeros(acc_ref.shape, F32)

    overflow = jnp.max(flag_ref[0:1, 0:1]) >= MOE_CAP

    @pl.when(jnp.logical_not(overflow))
    def _():
        lo = g * MOE_CAP
        h = jnp.concatenate([hs_ref[sub, pl.ds(lo, MOE_CAP), :] for sub in range(n_sub)], axis=0)
        gwv = jnp.concatenate([gws_ref[sub, pl.ds(lo, MOE_CAP), :] for sub in range(n_sub)], axis=0)
        y = _group_experts(h, gwv, g, wg_ref, wu_ref, wd_ref)
        for sub in range(n_sub):
            ys_ref[sub, pl.ds(lo, MOE_CAP), :] = y[sub * MOE_CAP:(sub + 1) * MOE_CAP]

    @pl.when(overflow)
    def _():
        acc_ref[...] += _group_experts(h2_ref[...], gw_ref[...], g, wg_ref, wu_ref, wd_ref)

    @pl.when(g == N_GROUPS - 1)
    def _():
        @pl.when(jnp.logical_not(overflow))
        def _():
            for sub in range(n_sub):
                rows = slice(sub * MOE_SUB, (sub + 1) * MOE_SUB)
                slot = lax.broadcasted_iota(jnp.int32, (1, seg), 1).astype(F32)
                scatter = jnp.where(dest_ref[rows, 0:1] == slot, 1.0, 0.0).astype(BF16)
                ys = ys_ref[sub]
                y_hi = ys.astype(BF16)
                y_lo = (ys - y_hi.astype(F32)).astype(BF16)
                acc_ref[rows, :] = (jnp.dot(scatter, y_hi, preferred_element_type=F32)
                                    + jnp.dot(scatter, y_lo, preferred_element_type=F32))
        o_ref[...] = _rms_rows(x1_ref[...] + acc_ref[...], gfin_ref[...])


def _moe(h2, gw, x1, w_gate, w_up, w_down, g_final):
    T = h2.shape[0]
    rows = MOE_ROWS
    n_sub = rows // MOE_SUB
    seg = N_GROUPS * MOE_CAP
    row = lambda n: pl.BlockSpec((rows, n), lambda i, g: (i, 0))
    return pl.pallas_call(
        _moe_kernel, grid=(T // rows, N_GROUPS),
        in_specs=[row(D_MODEL), row(ROUTER_LANES),
                  pl.BlockSpec((rows, D_MODEL), lambda i, g: (i, 0), pipeline_mode=pl.Buffered(1)),
                  pl.BlockSpec((EXPERTS_PER_GROUP, D_MODEL, D_EXPERT), lambda i, g: (g, 0, 0)),
                  pl.BlockSpec((EXPERTS_PER_GROUP, D_MODEL, D_EXPERT), lambda i, g: (g, 0, 0)),
                  pl.BlockSpec((EXPERTS_PER_GROUP, D_EXPERT, D_MODEL), lambda i, g: (g, 0, 0)),
                  pl.BlockSpec((1, D_MODEL), lambda i, g: (0, 0))],
        out_specs=row(D_MODEL),
        out_shape=jax.ShapeDtypeStruct((T, D_MODEL), F32),
        scratch_shapes=[pltpu.VMEM((n_sub, seg, D_MODEL), BF16),
                        pltpu.VMEM((n_sub, seg, ROUTER_LANES), F32),
                        pltpu.VMEM((n_sub, seg, D_MODEL), F32),
                        pltpu.VMEM((rows, D_MODEL), F32),
                        pltpu.VMEM((rows, ROUTER_LANES), F32),
                        pltpu.VMEM((SUBLANES, ROUTER_LANES), F32)],
        compiler_params=pltpu.CompilerParams(
            dimension_semantics=("arbitrary", "arbitrary"), vmem_limit_bytes=VMEM_LIMIT_BYTES),
        name="moe_final",
    )(h2, gw, x1, w_gate, w_up, w_down, g_final)


def _rope_inputs(positions):
    half = HEAD_DIM // 2
    inv = 1.0 / (ROPE_THETA ** (jnp.arange(0, HEAD_DIM, 2, dtype=F32) / HEAD_DIM))
    return positions.astype(F32)[..., None], inv[jnp.arange(LANES) % half].reshape(1, LANES)


def _layer(x, pos, inv_freq, g_mix, w_in, w_dw, b_dw, ln_g, ln_b, w_conv_out, w_attn_out, w_o,
           g_ffn, w_rg, b_rg, w_re, b_re, w_gate, w_up, w_down, g_final):
    B, S, D = x.shape
    T = B * S
    offs = [0]
    for n in IN_SIZES:
        offs.append(offs[-1] + n)
    col = lambda j: w_in[:, offs[j]:offs[j + 1]]
    w_ki, w_wi = col(5), col(6)
    ws = (
        col(0), col(1), col(2), col(3), col(4),
        jnp.concatenate([w_ki, w_ki, jnp.pad(w_wi, ((0, 0), (0, LANES - IDX_HEADS)))], axis=1),
        col(7), col(8),
    )
    ws = tuple(w.astype(BF16) for w in ws)
    u, q, k, vt, qi, kk, wi, gc, ga = _inproj(x, g_mix.reshape(1, D), pos, inv_freq, ws)

    mconv = _conv_branch(u, gc, w_dw.reshape(CONV_KERNEL, CONV_WIDTH), b_dw.reshape(1, -1),
                         ln_g.reshape(1, -1), ln_b.reshape(1, -1), w_conv_out.astype(BF16))
    attn = _attention(qi, q, wi, kk, k, vt)

    n_r = N_GROUPS + N_EXPERTS
    w_r = jnp.concatenate([w_rg, w_re.reshape(D, N_EXPERTS)], axis=1)
    w_r = jnp.pad(w_r, ((0, 0), (0, ROUTER_LANES - n_r))).astype(BF16)
    b_r = jnp.pad(jnp.concatenate([b_rg, b_re.reshape(N_EXPERTS)]), (0, ROUTER_LANES - n_r)).reshape(1, -1)
    x1, h2, gw = _mix(x.reshape(T, D), mconv.reshape(T, D), attn.reshape(T, ATTN_WIDTH), ga.reshape(T, D),
                      w_attn_out.astype(BF16), w_o.astype(BF16), g_ffn.reshape(1, D), w_r, b_r)
    out = _moe(h2, gw, x1, w_gate.astype(BF16), w_up.astype(BF16), w_down.astype(BF16), g_final.reshape(1, D))
    return out.reshape(B, S, D)


def kernel(x, positions, g_mix, w_in, w_dw, b_dw, ln_g, ln_b, w_conv_out, w_attn_out, w_o, g_ffn,
           w_rg, b_rg, w_re, b_re, w_gate, w_up, w_down, g_final):
    depth = g_mix.shape[0]
    assert depth == 1, "final norm is fused into the single layer's MoE call"
    pos, inv_freq = _rope_inputs(positions)
    return _layer(x, pos, inv_freq, g_mix[0], w_in[0], w_dw[0], b_dw[0], ln_g[0], ln_b[0],
                  w_conv_out[0], w_attn_out[0], w_o[0], g_ffn[0], w_rg[0], b_rg[0], w_re[0],
                  b_re[0], w_gate[0], w_up[0], w_down[0], g_final)
```

```python
import jax
import jax.numpy as jnp
from jax import lax
from jax.experimental import pallas as pl
from jax.experimental.pallas import tpu as pltpu

F32 = jnp.float32
BF16 = jnp.bfloat16

D_MODEL = 1024
CONV_WIDTH = 512
CONV_KERNEL = 31
N_HEADS = 8
HEAD_DIM = 64
ATTN_WIDTH = N_HEADS * HEAD_DIM
IDX_HEADS = 8
IDX_DIM = 64
TOPK_MAX = 256
ROPE_THETA = 10000.0
N_GROUPS = 4
EXPERTS_PER_GROUP = 4
N_EXPERTS = N_GROUPS * EXPERTS_PER_GROUP
D_EXPERT = 256
EPS = 1e-6
IN_SIZES = (2 * CONV_WIDTH, ATTN_WIDTH, ATTN_WIDTH, ATTN_WIDTH,
            IDX_HEADS * IDX_DIM, IDX_DIM, IDX_HEADS, D_MODEL, D_MODEL)

LANES = 128
SUBLANES = 8
VMEM_LIMIT_BYTES = 56 * 1024 * 1024

PROJ_ROWS = 512
CONV_ROWS = 512
CONV_HALO = 32
CONV_SUB = 64
ATTN_Q = 256
HEAD_GROUP = 4
ONES_ROWS = 16
V_ROWS = HEAD_DIM + ONES_ROWS
KEY_CHUNK = 512
COUNT_ROWS = 256
COUNT_ACC_ROWS = 32
HEAD_ACC_ROWS = 8
MIX_ROWS = 1024
MOE_ROWS = 1024
MOE_SUB = 256
MOE_CAP = 96
ROUTER_LANES = 128
MASK_VALUE = -1e30
KEY16_MIN = -2 ** 15
PACK_ROWS = 16
STAGE1_BITS = 15
STAGE2_BITS = 17
STAGE2_BELOW = 36000


def _rms_rows(x, g):
    ms = jnp.mean(x * x, axis=-1, keepdims=True)
    return x * lax.rsqrt(ms + EPS) * g


def _sigmoid(x):
    return jax.nn.sigmoid(x)


def _rope128(z, cos, sin, first_half):
    rot = jnp.where(first_half, pltpu.roll(z, LANES - HEAD_DIM // 2, 1),
                    pltpu.roll(z, HEAD_DIM // 2, 1))
    return z * cos + rot * sin


def _inproj_kernel(x_ref, g_ref, pos_ref, inv_ref, wu_ref, wq_ref, wk_ref, wv_ref, wqi_ref,
                   wkw_ref, wgc_ref, wga_ref,
                   u_ref, q_ref, k_ref, vt_ref, qi_ref, kk_ref, wi_ref, gc_ref, ga_ref):
    h = _rms_rows(x_ref[0], g_ref[...]).astype(BF16)
    ang = pos_ref[0] * inv_ref[...]
    lane = lax.broadcasted_iota(jnp.int32, ang.shape, 1)
    first_half = (lane % HEAD_DIM) < (HEAD_DIM // 2)
    cos = jnp.cos(ang)
    sin = jnp.sin(ang)
    sin = jnp.where(first_half, -sin, sin)

    def proj(w_ref):
        return jnp.dot(h, w_ref[...], preferred_element_type=F32)

    u_ref[0] = proj(wu_ref)
    gc_ref[0] = proj(wgc_ref)
    ga_ref[0] = proj(wga_ref)
    for w_ref, o_ref in ((wq_ref, q_ref), (wk_ref, k_ref), (wqi_ref, qi_ref)):
        z = proj(w_ref)
        for c in range(ATTN_WIDTH // LANES):
            sl = slice(c * LANES, (c + 1) * LANES)
            o_ref[0, :, sl] = _rope128(z[:, sl], cos, sin, first_half).astype(BF16)
    kw = proj(wkw_ref)
    kk_ref[0] = _rope128(kw[:, :LANES], cos, sin, first_half).astype(BF16)
    idx_scale = (IDX_HEADS ** -0.5) * (IDX_DIM ** -0.5)
    wi_ref[0] = kw[:, LANES:] * idx_scale
    vt = proj(wv_ref).T.astype(BF16)
    ones = jnp.ones((ONES_ROWS, vt.shape[1]), BF16)
    vt_ref[0] = jnp.concatenate(
        [piece for hd in range(N_HEADS) for piece in (vt[hd * HEAD_DIM:(hd + 1) * HEAD_DIM], ones)], axis=0)


def _inproj(x, g_mix, pos, inv_freq, ws):
    B, S, D = x.shape
    rows = PROJ_ROWS
    grid = (B, S // rows)
    row_spec = lambda n: pl.BlockSpec((1, rows, n), lambda b, i: (b, i, 0))
    w_spec = lambda w: pl.BlockSpec(w.shape, lambda b, i: (0, 0), pipeline_mode=pl.Buffered(1))
    out_shape = (
        jax.ShapeDtypeStruct((B, S, 2 * CONV_WIDTH), F32),
        jax.ShapeDtypeStruct((B, S, ATTN_WIDTH), BF16),
        jax.ShapeDtypeStruct((B, S, ATTN_WIDTH), BF16),
        jax.ShapeDtypeStruct((B, N_HEADS * V_ROWS, S), BF16),
        jax.ShapeDtypeStruct((B, S, ATTN_WIDTH), BF16),
        jax.ShapeDtypeStruct((B, S, LANES), BF16),
        jax.ShapeDtypeStruct((B, S, LANES), F32),
        jax.ShapeDtypeStruct((B, S, D_MODEL), F32),
        jax.ShapeDtypeStruct((B, S, D_MODEL), F32),
    )
    out_specs = (
        row_spec(2 * CONV_WIDTH), row_spec(ATTN_WIDTH), row_spec(ATTN_WIDTH),
        pl.BlockSpec((1, N_HEADS * V_ROWS, rows), lambda b, i: (b, 0, i)),
        row_spec(ATTN_WIDTH), row_spec(LANES), row_spec(LANES), row_spec(D_MODEL), row_spec(D_MODEL),
    )
    in_specs = [row_spec(D), pl.BlockSpec((1, D), lambda b, i: (0, 0)), row_spec(1),
                pl.BlockSpec((1, LANES), lambda b, i: (0, 0))]
    in_specs += [w_spec(w) for w in ws]
    return pl.pallas_call(
        _inproj_kernel, grid=grid, in_specs=in_specs, out_specs=out_specs, out_shape=out_shape,
        compiler_params=pltpu.CompilerParams(
            dimension_semantics=("arbitrary", "arbitrary"), vmem_limit_bytes=VMEM_LIMIT_BYTES),
        name="inproj",
    )(x, g_mix, pos, inv_freq, *ws)


def _conv_kernel(u_ref, uh_ref, gc_ref, wdw_ref, bdw_ref, lng_ref, lnb_ref, wout_ref,
                 o_ref, g_buf, s_buf):
    i = pl.program_id(1)
    rows = u_ref.shape[1]
    uh = uh_ref[0]
    gh = uh[:, :CONV_WIDTH] * _sigmoid(uh[:, CONV_WIDTH:])
    g_buf[0, 0:CONV_HALO, :] = jnp.where(i > 0, gh, 0.0)
    um = u_ref[0]
    g_buf[0, CONV_HALO:CONV_HALO + rows, :] = um[:, :CONV_WIDTH] * _sigmoid(um[:, CONV_WIDTH:])
    span = CONV_HALO + rows - SUBLANES
    for r in range(1, SUBLANES):
        for base in range(0, span, CONV_SUB):
            n = min(CONV_SUB, span - base)
            g_buf[r, base:base + n, :] = g_buf[0, pl.ds(base + r, n), :]
    first = CONV_HALO - (CONV_KERNEL - 1)
    for rr in range(rows // CONV_SUB):
        acc = jnp.zeros((CONV_SUB, CONV_WIDTH), F32)
        for t in range(CONV_KERNEL):
            shift = (first + t) % SUBLANES
            row0 = rr * CONV_SUB + (first + t) - shift
            acc = acc + wdw_ref[t:t + 1, :] * g_buf[shift, row0:row0 + CONV_SUB, :]
        c = acc + bdw_ref[...]
        mu = jnp.mean(c, axis=-1, keepdims=True)
        d = c - mu
        var = jnp.mean(d * d, axis=-1, keepdims=True)
        n = d * lax.rsqrt(var + EPS) * lng_ref[...] + lnb_ref[...]
        s_buf[rr * CONV_SUB:(rr + 1) * CONV_SUB, :] = (n * _sigmoid(n)).astype(BF16)
    y = jnp.dot(s_buf[...], wout_ref[...], preferred_element_type=F32)
    o_ref[0] = _sigmoid(gc_ref[0]) * y


def _conv_branch(u, gc, w_dw, b_dw, ln_g, ln_b, w_out):
    B, S, _ = u.shape
    rows = CONV_ROWS
    halo_per_tile = rows // CONV_HALO
    grid = (B, S // rows)
    vec = lambda n: pl.BlockSpec((1, n), lambda b, i: (0, 0))
    return pl.pallas_call(
        _conv_kernel, grid=grid,
        in_specs=[
            pl.BlockSpec((1, rows, 2 * CONV_WIDTH), lambda b, i: (b, i, 0)),
            pl.BlockSpec((1, CONV_HALO, 2 * CONV_WIDTH),
                         lambda b, i: (b, jnp.maximum(i * halo_per_tile - 1, 0), 0)),
            pl.BlockSpec((1, rows, D_MODEL), lambda b, i: (b, i, 0)),
            pl.BlockSpec((CONV_KERNEL, CONV_WIDTH), lambda b, i: (0, 0)),
            vec(CONV_WIDTH), vec(CONV_WIDTH), vec(CONV_WIDTH),
            pl.BlockSpec((CONV_WIDTH, D_MODEL), lambda b, i: (0, 0)),
        ],
        out_specs=pl.BlockSpec((1, rows, D_MODEL), lambda b, i: (b, i, 0)),
        out_shape=jax.ShapeDtypeStruct((B, S, D_MODEL), F32),
        scratch_shapes=[pltpu.VMEM((SUBLANES, CONV_HALO + rows, CONV_WIDTH), F32),
                        pltpu.VMEM((rows, CONV_WIDTH), BF16)],
        compiler_params=pltpu.CompilerParams(
            dimension_semantics=("arbitrary", "arbitrary"), vmem_limit_bytes=VMEM_LIMIT_BYTES),
        name="conv_branch",
    )(u, u, gc, w_dw, b_dw, ln_g, ln_b, w_out)


def _pair_rhs(x_bf16, scale=None):
    x = x_bf16.astype(F32)
    if scale is not None:
        x = x * scale
    lane = lax.broadcasted_iota(jnp.int32, x.shape, 1)
    lo = jnp.where(lane < HEAD_DIM, x, 0.0)
    hi = jnp.where(lane >= HEAD_DIM, x, 0.0)
    return jnp.concatenate([lo, hi], axis=0).astype(BF16)


def _fold_rows(x, op, rows):
    n = x.shape[0] // rows
    return op(x.reshape(n, rows, x.shape[1]), axis=0)


def _finish_rows(x, op):
    y = op(x.reshape(x.shape[0] // SUBLANES, SUBLANES, x.shape[1]), axis=0)
    return op(y, axis=0, keepdims=True)


def _chunk_loop(n, body, init):
    def quad(i, cr):
        for k in range(4):
            cr = body(4 * i + k, cr)
        return cr
    carry = lax.fori_loop(0, n // 4, quad, init)
    base = (n // 4) * 4
    carry = lax.cond(n % 4 >= 2, lambda cr: body(base + 1, body(base, cr)), lambda cr: cr, carry)
    return lax.cond(n % 2 == 1, lambda cr: body(n - 1, cr), lambda cr: cr, carry)


def _nt_dot(a, b):
    return lax.dot_general(a, b, (((1,), (1,)), ((), ())), preferred_element_type=F32)


def _key16_to_key32(k16):
    bits = lax.shift_left(jnp.where(k16 >= 0, k16, k16 ^ jnp.int32(0x7FFF)), 16)
    return jnp.where(bits >= 0, bits, bits ^ jnp.int32(0x7FFFFFFF))


def _key_to_f32(key):
    bits = jnp.where(key >= 0, key, key ^ jnp.int32(0x7FFFFFFF))
    return lax.bitcast_convert_type(bits, F32)


def _attn_kernel(qi_ref, q_ref, wi_ref, kk_ref, k_ref, vt_ref, o_ref,
                 sc_ref, sb_ref, xs_ref, acc_ref, out_ref):
    i = pl.program_id(1)
    kc = KEY_CHUNK
    n_chunks = ((i + 1) * ATTN_Q + kc - 1) // kc
    topk = float(TOPK_MAX)
    t_idx = i * ATTN_Q + lax.broadcasted_iota(jnp.int32, (1, ATTN_Q), 1)
    s_iota = lax.broadcasted_iota(jnp.int32, (kc, ATTN_Q), 0)
    n_pairs = N_HEADS // 2

    qi = qi_ref[0]
    idx_rhs = [_pair_rhs(qi[:, p * LANES:(p + 1) * LANES]) for p in range(IDX_HEADS // 2)]
    wi_t = wi_ref[0].T

    def score_body(c, carry):
        start = pl.multiple_of(c * kc, kc)
        kk = kk_ref[0, pl.ds(start, kc), :]
        s = jnp.zeros((kc, ATTN_Q), F32)
        for p in range(IDX_HEADS // 2):
            r = _nt_dot(kk, idx_rhs[p])
            s = s + wi_t[2 * p:2 * p + 1, :] * jnp.maximum(r[:, :ATTN_Q], 0.0)
            s = s + wi_t[2 * p + 1:2 * p + 2, :] * jnp.maximum(r[:, ATTN_Q:], 0.0)
        causal = (start + s_iota) <= t_idx
        s = jnp.where(causal, s, -jnp.inf)
        sc_ref[pl.ds(start, kc), :] = s
        sb_ref[pl.ds(start, kc), :] = s.astype(BF16)
        return carry

    _chunk_loop(n_chunks, score_body, 0)

    n_count = n_chunks * (kc // COUNT_ROWS)

    def count_ge(cand):
        def body(c, acc):
            for half in range(kc // COUNT_ROWS):
                start = pl.multiple_of(c * kc + half * COUNT_ROWS, COUNT_ROWS)
                hit = jnp.where(sc_ref[pl.ds(start, COUNT_ROWS), :] >= cand, 1.0, 0.0)
                acc = acc + _fold_rows(hit, jnp.sum, COUNT_ACC_ROWS)
            return acc
        acc = lax.fori_loop(0, n_chunks, body, jnp.zeros((COUNT_ACC_ROWS, ATTN_Q), F32))
        return _finish_rows(acc, jnp.sum)

    def count_ge_bf16(cand):
        def body(c, acc):
            for half in range(kc // COUNT_ROWS):
                start = pl.multiple_of(c * kc + half * COUNT_ROWS, COUNT_ROWS)
                hit = jnp.where(sb_ref[pl.ds(start, COUNT_ROWS), :] >= cand, jnp.ones((), BF16), jnp.zeros((), BF16))
                parts = [hit[r * PACK_ROWS:(r + 1) * PACK_ROWS] for r in range(COUNT_ROWS // PACK_ROWS)]
                while len(parts) > 1:
                    parts = [a + b for a, b in zip(parts[::2], parts[1::2])]
                acc = acc + parts[0].astype(F32)
            return acc
        acc = lax.fori_loop(0, n_chunks, body, jnp.zeros((PACK_ROWS, ATTN_Q), F32))
        return _finish_rows(acc, jnp.sum)

    c0 = count_ge_bf16(jnp.zeros((1, ATTN_Q), BF16))
    k16_0 = jnp.where(c0 >= topk, jnp.int32(0), jnp.int32(KEY16_MIN))

    def bit16_body(j, k16):
        cand = k16 + lax.shift_left(jnp.int32(1), STAGE1_BITS - 1 - j)
        cnt = count_ge_bf16(_key_to_f32(_key16_to_key32(cand)).astype(BF16))
        return jnp.where(cnt >= topk, cand, k16)

    k16 = lax.fori_loop(0, STAGE1_BITS, bit16_body, k16_0)
    low = _key16_to_key32(k16) - jnp.int32(STAGE2_BELOW)

    def bit_body(j, carry):
        key, cnt_key = carry
        cand = key + lax.shift_left(jnp.int32(1), STAGE2_BITS - 1 - j)
        cnt = count_ge(_key_to_f32(cand))
        take = cnt >= topk
        return jnp.where(take, cand, key), jnp.where(take, cnt, cnt_key)

    key, cnt_ge = lax.fori_loop(0, STAGE2_BITS, bit_body, (low, jnp.full((1, ATTN_Q), jnp.inf, F32)))
    thr = _key_to_f32(key)
    thr = jnp.where((k16 == KEY16_MIN) | (thr != thr), -jnp.inf, thr)
    excess = jnp.where((cnt_ge > topk) & (thr > -jnp.inf), 1.0, 0.0)
    has_excess = jnp.sum(excess) > 0.0

    @pl.when(has_excess)
    def _():
        r_i = lax.broadcasted_iota(jnp.int32, (kc, kc), 0)
        c_i = lax.broadcasted_iota(jnp.int32, (kc, kc), 1)
        strict_lower = jnp.where(c_i < r_i, 1.0, 0.0).astype(BF16)

        def count_gt(c, acc):
            start = pl.multiple_of(c * COUNT_ROWS, COUNT_ROWS)
            hit = jnp.where(sc_ref[pl.ds(start, COUNT_ROWS), :] > thr, 1.0, 0.0)
            return acc + _fold_rows(hit, jnp.sum, COUNT_ACC_ROWS)

        gt_acc = lax.fori_loop(0, n_count, count_gt, jnp.zeros((COUNT_ACC_ROWS, ATTN_Q), F32))
        need = topk - _finish_rows(gt_acc, jnp.sum)

        def body(c, before):
            start = pl.multiple_of(c * kc, kc)
            s = sc_ref[pl.ds(start, kc), :]
            tie = jnp.where(s == thr, 1.0, 0.0)
            rank = jnp.dot(strict_lower, tie.astype(BF16), preferred_element_type=F32) + before
            keep = (s > thr) | ((s == thr) & (rank < need))
            sc_ref[pl.ds(start, kc), :] = jnp.where(keep, jnp.inf, -jnp.inf)
            return before + jnp.sum(tie, axis=0, keepdims=True)
        lax.fori_loop(0, n_chunks, body, jnp.zeros((1, ATTN_Q), F32))

    q = q_ref[0]
    acc_ref[...] = jnp.zeros(acc_ref.shape, F32)
    neg = jnp.full((HEAD_ACC_ROWS, ATTN_Q), -jnp.inf, F32)
    n_groups = N_HEADS // HEAD_GROUP
    att_rhs = [_pair_rhs(q[:, p * LANES:(p + 1) * LANES], HEAD_DIM ** -0.5) for p in range(n_pairs)]

    def qk_part(start, g, mx):
        if g == 0:
            keep = (sc_ref[pl.ds(start, kc), :] >= thr) & ((start + s_iota) <= t_idx)
            bias = jnp.where(keep, 0.0, MASK_VALUE)
            sc_ref[pl.ds(start, kc), :] = bias
        else:
            bias = sc_ref[pl.ds(start, kc), :]
        out = []
        for p in range(g * HEAD_GROUP // 2, (g + 1) * HEAD_GROUP // 2):
            kp = k_ref[0, pl.ds(start, kc), p * LANES:(p + 1) * LANES]
            lg = _nt_dot(kp, att_rhs[p])
            for hh in range(2):
                hl = 2 * p + hh - g * HEAD_GROUP
                x = lg[:, hh * ATTN_Q:(hh + 1) * ATTN_Q] + bias
                xs_ref[g % 2, pl.ds(start, kc), hl * ATTN_Q:(hl + 1) * ATTN_Q] = x
                out.append(jnp.maximum(mx[hl], _fold_rows(x, jnp.max, HEAD_ACC_ROWS)))
        return tuple(out)

    def pv_part(start, g, m):
        for hl in range(HEAD_GROUP):
            h = g * HEAD_GROUP + hl
            pm = jnp.exp(xs_ref[g % 2, pl.ds(start, kc), hl * ATTN_Q:(hl + 1) * ATTN_Q] - m[hl])
            rows = slice(h * V_ROWS, (h + 1) * V_ROWS)
            acc_ref[rows, :] += jnp.dot(vt_ref[0, rows, pl.ds(start, kc)], pm.astype(BF16),
                                        preferred_element_type=F32)

    m = None
    for s in range(n_groups + 1):
        def body(c, mx, s=s, m=m):
            start = pl.multiple_of(c * kc, kc)
            if s < n_groups:
                mx = qk_part(start, s, mx)
            if s > 0:
                pv_part(start, s - 1, m)
            return mx

        mx = _chunk_loop(n_chunks, body, (neg,) * HEAD_GROUP)
        if s < n_groups:
            m = [_finish_rows(mx[hl], jnp.max) for hl in range(HEAD_GROUP)]
    for h in range(N_HEADS):
        pv = acc_ref[h * V_ROWS:h * V_ROWS + HEAD_DIM, :]
        denom = acc_ref[h * V_ROWS + HEAD_DIM:h * V_ROWS + HEAD_DIM + 1, :]
        out_ref[h * HEAD_DIM:(h + 1) * HEAD_DIM, :] = pv / denom
    o_ref[0] = out_ref[...].T.astype(BF16)


def _attention(qi, q, wi, kk, k, vt):
    B, S, _ = q.shape
    grid = (B, S // ATTN_Q)
    blk = lambda n: pl.BlockSpec((1, ATTN_Q, n), lambda b, i: (b, i, 0))
    full = lambda n: pl.BlockSpec((1, S, n), lambda b, i: (b, 0, 0), pipeline_mode=pl.Buffered(1))
    return pl.pallas_call(
        _attn_kernel, grid=grid,
        in_specs=[blk(ATTN_WIDTH), blk(ATTN_WIDTH), blk(LANES), full(LANES), full(ATTN_WIDTH),
                  pl.BlockSpec((1, N_HEADS * V_ROWS, S), lambda b, i: (b, 0, 0), pipeline_mode=pl.Buffered(1))],
        out_specs=blk(ATTN_WIDTH),
        out_shape=jax.ShapeDtypeStruct((B, S, ATTN_WIDTH), BF16),
        scratch_shapes=[pltpu.VMEM((S, ATTN_Q), F32),
                        pltpu.VMEM((S, ATTN_Q), BF16),
                        pltpu.VMEM((2, S, HEAD_GROUP * ATTN_Q), F32),
                        pltpu.VMEM((N_HEADS * V_ROWS, ATTN_Q), F32),
                        pltpu.VMEM((ATTN_WIDTH, ATTN_Q), F32)],
        compiler_params=pltpu.CompilerParams(
            dimension_semantics=("arbitrary", "arbitrary"), vmem_limit_bytes=VMEM_LIMIT_BYTES),
        name="sparse_attn",
    )(qi, q, wi, kk, k, vt)


def _lane_first(mask, lane):
    return jnp.min(jnp.where(mask, lane, ROUTER_LANES), axis=-1, keepdims=True)


def _router(logits):
    lane = lax.broadcasted_iota(jnp.int32, logits.shape, 1)
    neg = -jnp.inf
    gl = jnp.where(lane < N_GROUPS, logits, neg)
    gmax = jnp.max(gl, axis=-1, keepdims=True)
    gsum = jnp.sum(jnp.exp(gl - gmax), axis=-1, keepdims=True)
    p_g = 1.0 / gsum
    gi = _lane_first(gl == gmax, lane)
    e_lo = N_GROUPS + gi * EXPERTS_PER_GROUP
    in_group = (lane >= e_lo) & (lane < e_lo + EXPERTS_PER_GROUP)
    el = jnp.where(in_group, logits, neg)
    emax = jnp.max(el, axis=-1, keepdims=True)
    ee = jnp.exp(el - emax)
    pe = ee / jnp.sum(ee, axis=-1, keepdims=True)
    pe = jnp.where(in_group, pe, -1.0)
    p1 = jnp.max(pe, axis=-1, keepdims=True)
    l1 = _lane_first(pe == p1, lane)
    pe2 = jnp.where(lane == l1, -1.0, pe)
    p2 = jnp.max(pe2, axis=-1, keepdims=True)
    l2 = _lane_first(pe2 == p2, lane)
    tot = p1 + p2
    c1 = p_g * (p1 / tot)
    c2 = p_g * (p2 / tot)
    return (jnp.where(lane == l1, c1, 0.0) + jnp.where(lane == l2, c2, 0.0)
            + jnp.where(lane == 0, gi.astype(F32), 0.0))


def _mix_kernel(x_ref, mconv_ref, attn_ref, ga_ref, wao_ref, wo_ref, gffn_ref, wr_ref, br_ref,
                x1_ref, h2_ref, gw_ref):
    ya = jnp.dot(attn_ref[...], wao_ref[...], preferred_element_type=F32)
    m = mconv_ref[...] + _sigmoid(ga_ref[...]) * ya
    x1 = x_ref[...] + jnp.dot(m.astype(BF16), wo_ref[...], preferred_element_type=F32)
    x1_ref[...] = x1
    h2 = _rms_rows(x1, gffn_ref[...]).astype(BF16)
    h2_ref[...] = h2
    logits = jnp.dot(h2, wr_ref[...], preferred_element_type=F32) + br_ref[...]
    gw_ref[...] = _router(logits)


def _mix(x, mconv, attn, ga, w_ao, w_o, g_ffn, w_r, b_r):
    T = x.shape[0]
    rows = MIX_ROWS
    row = lambda n: pl.BlockSpec((rows, n), lambda i: (i, 0))
    const = lambda a: pl.BlockSpec(a.shape, lambda i: (0, 0))
    return pl.pallas_call(
        _mix_kernel, grid=(T // rows,),
        in_specs=[row(D_MODEL), row(D_MODEL), row(ATTN_WIDTH), row(D_MODEL),
                  const(w_ao), const(w_o), const(g_ffn), const(w_r), const(b_r)],
        out_specs=(row(D_MODEL), row(D_MODEL), row(ROUTER_LANES)),
        out_shape=(jax.ShapeDtypeStruct((T, D_MODEL), F32),
                   jax.ShapeDtypeStruct((T, D_MODEL), BF16),
                   jax.ShapeDtypeStruct((T, ROUTER_LANES), F32)),
        compiler_params=pltpu.CompilerParams(
            dimension_semantics=("arbitrary",), vmem_limit_bytes=VMEM_LIMIT_BYTES),
        name="mix_router",
    )(x, mconv, attn, ga, w_ao, w_o, g_ffn, w_r, b_r)


def _split3(x):
    a = x.astype(BF16)
    r = x - a.astype(F32)
    b = r.astype(BF16)
    return a, b, (r - b.astype(F32)).astype(BF16)


def _group_experts(h, gwv, g, wg_ref, wu_ref, wd_ref):
    lane = lax.broadcasted_iota(jnp.int32, gwv.shape, 1)
    y = None
    for j in range(EXPERTS_PER_GROUP):
        gate = jnp.dot(h, wg_ref[j], preferred_element_type=F32)
        up = jnp.dot(h, wu_ref[j], preferred_element_type=F32)
        expert_lane = N_GROUPS + g * EXPERTS_PER_GROUP + j
        gwe = jnp.sum(jnp.where(lane == expert_lane, gwv, 0.0), axis=-1, keepdims=True)
        hid = (gate * _sigmoid(gate)) * up * gwe
        yj = jnp.dot(hid.astype(BF16), wd_ref[j], preferred_element_type=F32)
        y = yj if y is None else y + yj
    return y


def _moe_kernel(h2_ref, gw_ref, x1_ref, wg_ref, wu_ref, wd_ref, gfin_ref, o_ref,
                hs_ref, gws_ref, ys_ref, acc_ref, dest_ref, flag_ref):
    g = pl.program_id(1)
    n_sub = MOE_ROWS // MOE_SUB
    seg = N_GROUPS * MOE_CAP

    @pl.when(g == 0)
    def _():
        r_i = lax.broadcasted_iota(jnp.int32, (MOE_SUB, MOE_SUB), 0)
        c_i = lax.broadcasted_iota(jnp.int32, (MOE_SUB, MOE_SUB), 1)
        tri = jnp.where(c_i < r_i, 1.0, 0.0).astype(BF16)
        over = jnp.zeros((1, 1), F32)
        for sub in range(n_sub):
            rows = slice(sub * MOE_SUB, (sub + 1) * MOE_SUB)
            gw = gw_ref[rows, :]
            lane = lax.broadcasted_iota(jnp.int32, gw.shape, 1)
            gi = jnp.sum(jnp.where(lane == 0, gw, 0.0), axis=-1, keepdims=True)
            onehot = jnp.where((lane < N_GROUPS) & (lane.astype(F32) == gi), 1.0, 0.0)
            before = jnp.dot(tri, onehot.astype(BF16), preferred_element_type=F32)
            rank = jnp.sum(onehot * before, axis=-1, keepdims=True)
            over = jnp.maximum(over, jnp.max(rank, axis=0, keepdims=True))
            dest = jnp.where(rank < MOE_CAP, gi * MOE_CAP + rank, -1.0)
            dest_col = jnp.where(lane == 0, dest, 0.0)
            dest_ref[rows, :] = dest_col
            dest_row = dest_col.T[0:1, :]
            slot = lax.broadcasted_iota(jnp.int32, (seg, 1), 0).astype(F32)
            gather = jnp.where(dest_row == slot, 1.0, 0.0).astype(BF16)
            hs_ref[sub] = jnp.dot(gather, h2_ref[rows, :], preferred_element_type=F32).astype(BF16)
            gws_ref[sub] = sum(jnp.dot(gather, p, preferred_element_type=F32) for p in _split3(gw))
        flag_ref[...] = jnp.broadcast_to(over, flag_ref.shape)
        acc_ref[...] = jnp.zeros(acc_ref.shape, F32)

    overflow = jnp.max(flag_ref[0:1, 0:1]) >= MOE_CAP

    @pl.when(jnp.logical_not(overflow))
    def _():
        lo = g * MOE_CAP
        h = jnp.concatenate([hs_ref[sub, pl.ds(lo, MOE_CAP), :] for sub in range(n_sub)], axis=0)
        gwv = jnp.concatenate([gws_ref[sub, pl.ds(lo, MOE_CAP), :] for sub in range(n_sub)], axis=0)
        y = _group_experts(h, gwv, g, wg_ref, wu_ref, wd_ref)
        for sub in range(n_sub):
            ys_ref[sub, pl.ds(lo, MOE_CAP), :] = y[sub * MOE_CAP:(sub + 1) * MOE_CAP]

    @pl.when(overflow)
    def _():
        acc_ref[...] += _group_experts(h2_ref[...], gw_ref[...], g, wg_ref, wu_ref, wd_ref)

    @pl.when(g == N_GROUPS - 1)
    def _():
        @pl.when(jnp.logical_not(overflow))
        def _():
            for sub in range(n_sub):
                rows = slice(sub * MOE_SUB, (sub + 1) * MOE_SUB)
                slot = lax.broadcasted_iota(jnp.int32, (1, seg), 1).astype(F32)
                scatter = jnp.where(dest_ref[rows, 0:1] == slot, 1.0, 0.0).astype(BF16)
                ys = ys_ref[sub]
                y_hi = ys.astype(BF16)
                y_lo = (ys - y_hi.astype(F32)).astype(BF16)
                acc_ref[rows, :] = (jnp.dot(scatter, y_hi, preferred_element_type=F32)
                                    + jnp.dot(scatter, y_lo, preferred_element_type=F32))
        o_ref[...] = _rms_rows(x1_ref[...] + acc_ref[...], gfin_ref[...])


def _moe(h2, gw, x1, w_gate, w_up, w_down, g_final):
    T = h2.shape[0]
    rows = MOE_ROWS
    n_sub = rows // MOE_SUB
    seg = N_GROUPS * MOE_CAP
    row = lambda n: pl.BlockSpec((rows, n), lambda i, g: (i, 0))
    return pl.pallas_call(
        _moe_kernel, grid=(T // rows, N_GROUPS),
        in_specs=[row(D_MODEL), row(ROUTER_LANES),
                  pl.BlockSpec((rows, D_MODEL), lambda i, g: (i, 0), pipeline_mode=pl.Buffered(1)),
                  pl.BlockSpec((EXPERTS_PER_GROUP, D_MODEL, D_EXPERT), lambda i, g: (g, 0, 0)),
                  pl.BlockSpec((EXPERTS_PER_GROUP, D_MODEL, D_EXPERT), lambda i, g: (g, 0, 0)),
                  pl.BlockSpec((EXPERTS_PER_GROUP, D_EXPERT, D_MODEL), lambda i, g: (g, 0, 0)),
                  pl.BlockSpec((1, D_MODEL), lambda i, g: (0, 0))],
        out_specs=row(D_MODEL),
        out_shape=jax.ShapeDtypeStruct((T, D_MODEL), F32),
        scratch_shapes=[pltpu.VMEM((n_sub, seg, D_MODEL), BF16),
                        pltpu.VMEM((n_sub, seg, ROUTER_LANES), F32),
                        pltpu.VMEM((n_sub, seg, D_MODEL), F32),
                        pltpu.VMEM((rows, D_MODEL), F32),
                        pltpu.VMEM((rows, ROUTER_LANES), F32),
                        pltpu.VMEM((SUBLANES, ROUTER_LANES), F32)],
        compiler_params=pltpu.CompilerParams(
            dimension_semantics=("arbitrary", "arbitrary"), vmem_limit_bytes=VMEM_LIMIT_BYTES),
        name="moe_final",
    )(h2, gw, x1, w_gate, w_up, w_down, g_final)


def _rope_inputs(positions):
    half = HEAD_DIM // 2
    inv = 1.0 / (ROPE_THETA ** (jnp.arange(0, HEAD_DIM, 2, dtype=F32) / HEAD_DIM))
    return positions.astype(F32)[..., None], inv[jnp.arange(LANES) % half].reshape(1, LANES)


def _layer(x, pos, inv_freq, g_mix, w_in, w_dw, b_dw, ln_g, ln_b, w_conv_out, w_attn_out, w_o,
           g_ffn, w_rg, b_rg, w_re, b_re, w_gate, w_up, w_down, g_final):
    B, S, D = x.shape
    T = B * S
    offs = [0]
    for n in IN_SIZES:
        offs.append(offs[-1] + n)
    col = lambda j: w_in[:, offs[j]:offs[j + 1]]
    w_ki, w_wi = col(5), col(6)
    ws = (
        col(0), col(1), col(2), col(3), col(4),
        jnp.concatenate([w_ki, w_ki, jnp.pad(w_wi, ((0, 0), (0, LANES - IDX_HEADS)))], axis=1),
        col(7), col(8),
    )
    ws = tuple(w.astype(BF16) for w in ws)
    u, q, k, vt, qi, kk, wi, gc, ga = _inproj(x, g_mix.reshape(1, D), pos, inv_freq, ws)

    mconv = _conv_branch(u, gc, w_dw.reshape(CONV_KERNEL, CONV_WIDTH), b_dw.reshape(1, -1),
                         ln_g.reshape(1, -1), ln_b.reshape(1, -1), w_conv_out.astype(BF16))
    attn = _attention(qi, q, wi, kk, k, vt)

    n_r = N_GROUPS + N_EXPERTS
    w_r = jnp.concatenate([w_rg, w_re.reshape(D, N_EXPERTS)], axis=1)
    w_r = jnp.pad(w_r, ((0, 0), (0, ROUTER_LANES - n_r))).astype(BF16)
    b_r = jnp.pad(jnp.concatenate([b_rg, b_re.reshape(N_EXPERTS)]), (0, ROUTER_LANES - n_r)).reshape(1, -1)
    x1, h2, gw = _mix(x.reshape(T, D), mconv.reshape(T, D), attn.reshape(T, ATTN_WIDTH), ga.reshape(T, D),
                      w_attn_out.astype(BF16), w_o.astype(BF16), g_ffn.reshape(1, D), w_r, b_r)
    out = _moe(h2, gw, x1, w_gate.astype(BF16), w_up.astype(BF16), w_down.astype(BF16), g_final.reshape(1, D))
    return out.reshape(B, S, D)


def kernel(x, positions, g_mix, w_in, w_dw, b_dw, ln_g, ln_b, w_conv_out, w_attn_out, w_o, g_ffn,
           w_rg, b_rg, w_re, b_re, w_gate, w_up, w_down, g_final):
    depth = g_mix.shape[0]
    assert depth == 1, "final norm is fused into the single layer's MoE call"
    pos, inv_freq = _rope_inputs(positions)
    return _layer(x, pos, inv_freq, g_mix[0], w_in[0], w_dw[0], b_dw[0], ln_g[0], ln_b[0],
                  w_conv_out[0], w_attn_out[0], w_o[0], g_ffn[0], w_rg[0], b_rg[0], w_re[0],
                  b_re[0], w_gate[0], w_up[0], w_down[0], g_final)
```

```python
import jax
import jax.numpy as jnp
from jax import lax
from jax.experimental import pallas as pl
from jax.experimental.pallas import tpu as pltpu

F32 = jnp.float32
BF16 = jnp.bfloat16

D_MODEL = 1024
CONV_WIDTH = 512
CONV_KERNEL = 31
N_HEADS = 8
HEAD_DIM = 64
ATTN_WIDTH = N_HEADS * HEAD_DIM
IDX_HEADS = 8
IDX_DIM = 64
TOPK_MAX = 256
ROPE_THETA = 10000.0
N_GROUPS = 4
EXPERTS_PER_GROUP = 4
N_EXPERTS = N_GROUPS * EXPERTS_PER_GROUP
D_EXPERT = 256
EPS = 1e-6
IN_SIZES = (2 * CONV_WIDTH, ATTN_WIDTH, ATTN_WIDTH, ATTN_WIDTH,
            IDX_HEADS * IDX_DIM, IDX_DIM, IDX_HEADS, D_MODEL, D_MODEL)

LANES = 128
SUBLANES = 8
VMEM_LIMIT_BYTES = 56 * 1024 * 1024

PROJ_ROWS = 512
CONV_ROWS = 512
CONV_HALO = 32
CONV_SUB = 64
ATTN_Q = 256
HEAD_GROUP = 4
ONES_ROWS = 16
V_ROWS = HEAD_DIM + ONES_ROWS
KEY_CHUNK = 512
COUNT_ROWS = 256
COUNT_ACC_ROWS = 32
HEAD_ACC_ROWS = 8
MIX_ROWS = 1024
MOE_ROWS = 1024
MOE_SUB = 256
MOE_CAP = 96
ROUTER_LANES = 128
MASK_VALUE = -1e30
KEY16_MIN = -2 ** 15
BF16_MIN_NORMAL_KEY = 1 << 7
PACK_ROWS = 16
STAGE1_BITS = 15
STAGE2_BITS = 17
STAGE2_BELOW = 36000


def _rms_rows(x, g):
    ms = jnp.mean(x * x, axis=-1, keepdims=True)
    return x * lax.rsqrt(ms + EPS) * g


def _sigmoid(x):
    return jax.nn.sigmoid(x)


def _rope128(z, cos, sin, first_half):
    rot = jnp.where(first_half, pltpu.roll(z, LANES - HEAD_DIM // 2, 1),
                    pltpu.roll(z, HEAD_DIM // 2, 1))
    return z * cos + rot * sin


def _inproj_kernel(x_ref, g_ref, pos_ref, inv_ref, wu_ref, wq_ref, wk_ref, wv_ref, wqi_ref,
                   wkw_ref, wgc_ref, wga_ref,
                   u_ref, q_ref, k_ref, vt_ref, qi_ref, kk_ref, wi_ref, gc_ref, ga_ref):
    h = _rms_rows(x_ref[0], g_ref[...]).astype(BF16)
    ang = pos_ref[0] * inv_ref[...]
    lane = lax.broadcasted_iota(jnp.int32, ang.shape, 1)
    first_half = (lane % HEAD_DIM) < (HEAD_DIM // 2)
    cos = jnp.cos(ang)
    sin = jnp.sin(ang)
    sin = jnp.where(first_half, -sin, sin)

    def proj(w_ref):
        return jnp.dot(h, w_ref[...], preferred_element_type=F32)

    u_ref[0] = proj(wu_ref)
    gc_ref[0] = proj(wgc_ref)
    ga_ref[0] = proj(wga_ref)
    for w_ref, o_ref in ((wq_ref, q_ref), (wk_ref, k_ref), (wqi_ref, qi_ref)):
        z = proj(w_ref)
        for c in range(ATTN_WIDTH // LANES):
            sl = slice(c * LANES, (c + 1) * LANES)
            o_ref[0, :, sl] = _rope128(z[:, sl], cos, sin, first_half).astype(BF16)
    kw = proj(wkw_ref)
    kk_ref[0] = _rope128(kw[:, :LANES], cos, sin, first_half).astype(BF16)
    idx_scale = (IDX_HEADS ** -0.5) * (IDX_DIM ** -0.5)
    wi_ref[0] = kw[:, LANES:] * idx_scale
    vt = proj(wv_ref).T.astype(BF16)
    ones = jnp.ones((ONES_ROWS, vt.shape[1]), BF16)
    vt_ref[0] = jnp.concatenate(
        [piece for hd in range(N_HEADS) for piece in (vt[hd * HEAD_DIM:(hd + 1) * HEAD_DIM], ones)], axis=0)


def _inproj(x, g_mix, pos, inv_freq, ws):
    B, S, D = x.shape
    rows = PROJ_ROWS
    grid = (B, S // rows)
    row_spec = lambda n: pl.BlockSpec((1, rows, n), lambda b, i: (b, i, 0))
    w_spec = lambda w: pl.BlockSpec(w.shape, lambda b, i: (0, 0), pipeline_mode=pl.Buffered(1))
    out_shape = (
        jax.ShapeDtypeStruct((B, S, 2 * CONV_WIDTH), F32),
        jax.ShapeDtypeStruct((B, S, ATTN_WIDTH), BF16),
        jax.ShapeDtypeStruct((B, S, ATTN_WIDTH), BF16),
        jax.ShapeDtypeStruct((B, N_HEADS * V_ROWS, S), BF16),
        jax.ShapeDtypeStruct((B, S, ATTN_WIDTH), BF16),
        jax.ShapeDtypeStruct((B, S, LANES), BF16),
        jax.ShapeDtypeStruct((B, S, LANES), F32),
        jax.ShapeDtypeStruct((B, S, D_MODEL), F32),
        jax.ShapeDtypeStruct((B, S, D_MODEL), F32),
    )
    out_specs = (
        row_spec(2 * CONV_WIDTH), row_spec(ATTN_WIDTH), row_spec(ATTN_WIDTH),
        pl.BlockSpec((1, N_HEADS * V_ROWS, rows), lambda b, i: (b, 0, i)),
        row_spec(ATTN_WIDTH), row_spec(LANES), row_spec(LANES), row_spec(D_MODEL), row_spec(D_MODEL),
    )
    in_specs = [row_spec(D), pl.BlockSpec((1, D), lambda b, i: (0, 0)), row_spec(1),
                pl.BlockSpec((1, LANES), lambda b, i: (0, 0))]
    in_specs += [w_spec(w) for w in ws]
    return pl.pallas_call(
        _inproj_kernel, grid=grid, in_specs=in_specs, out_specs=out_specs, out_shape=out_shape,
        compiler_params=pltpu.CompilerParams(
            dimension_semantics=("arbitrary", "arbitrary"), vmem_limit_bytes=VMEM_LIMIT_BYTES),
        name="inproj",
    )(x, g_mix, pos, inv_freq, *ws)


def _conv_kernel(u_ref, uh_ref, gc_ref, wdw_ref, bdw_ref, lng_ref, lnb_ref, wout_ref,
                 o_ref, g_buf, s_buf):
    i = pl.program_id(1)
    rows = u_ref.shape[1]
    uh = uh_ref[0]
    gh = uh[:, :CONV_WIDTH] * _sigmoid(uh[:, CONV_WIDTH:])
    g_buf[0, 0:CONV_HALO, :] = jnp.where(i > 0, gh, 0.0)
    um = u_ref[0]
    g_buf[0, CONV_HALO:CONV_HALO + rows, :] = um[:, :CONV_WIDTH] * _sigmoid(um[:, CONV_WIDTH:])
    span = CONV_HALO + rows - SUBLANES
    for r in range(1, SUBLANES):
        for base in range(0, span, CONV_SUB):
            n = min(CONV_SUB, span - base)
            g_buf[r, base:base + n, :] = g_buf[0, pl.ds(base + r, n), :]
    first = CONV_HALO - (CONV_KERNEL - 1)
    for rr in range(rows // CONV_SUB):
        acc = jnp.zeros((CONV_SUB, CONV_WIDTH), F32)
        for t in range(CONV_KERNEL):
            shift = (first + t) % SUBLANES
            row0 = rr * CONV_SUB + (first + t) - shift
            acc = acc + wdw_ref[t:t + 1, :] * g_buf[shift, row0:row0 + CONV_SUB, :]
        c = acc + bdw_ref[...]
        mu = jnp.mean(c, axis=-1, keepdims=True)
        d = c - mu
        var = jnp.mean(d * d, axis=-1, keepdims=True)
        n = d * lax.rsqrt(var + EPS) * lng_ref[...] + lnb_ref[...]
        s_buf[rr * CONV_SUB:(rr + 1) * CONV_SUB, :] = (n * _sigmoid(n)).astype(BF16)
    y = jnp.dot(s_buf[...], wout_ref[...], preferred_element_type=F32)
    o_ref[0] = _sigmoid(gc_ref[0]) * y


def _conv_branch(u, gc, w_dw, b_dw, ln_g, ln_b, w_out):
    B, S, _ = u.shape
    rows = CONV_ROWS
    halo_per_tile = rows // CONV_HALO
    grid = (B, S // rows)
    vec = lambda n: pl.BlockSpec((1, n), lambda b, i: (0, 0))
    return pl.pallas_call(
        _conv_kernel, grid=grid,
        in_specs=[
            pl.BlockSpec((1, rows, 2 * CONV_WIDTH), lambda b, i: (b, i, 0)),
            pl.BlockSpec((1, CONV_HALO, 2 * CONV_WIDTH),
                         lambda b, i: (b, jnp.maximum(i * halo_per_tile - 1, 0), 0)),
            pl.BlockSpec((1, rows, D_MODEL), lambda b, i: (b, i, 0)),
            pl.BlockSpec((CONV_KERNEL, CONV_WIDTH), lambda b, i: (0, 0)),
            vec(CONV_WIDTH), vec(CONV_WIDTH), vec(CONV_WIDTH),
            pl.BlockSpec((CONV_WIDTH, D_MODEL), lambda b, i: (0, 0)),
        ],
        out_specs=pl.BlockSpec((1, rows, D_MODEL), lambda b, i: (b, i, 0)),
        out_shape=jax.ShapeDtypeStruct((B, S, D_MODEL), F32),
        scratch_shapes=[pltpu.VMEM((SUBLANES, CONV_HALO + rows, CONV_WIDTH), F32),
                        pltpu.VMEM((rows, CONV_WIDTH), BF16)],
        compiler_params=pltpu.CompilerParams(
            dimension_semantics=("arbitrary", "arbitrary"), vmem_limit_bytes=VMEM_LIMIT_BYTES),
        name="conv_branch",
    )(u, u, gc, w_dw, b_dw, ln_g, ln_b, w_out)


def _pair_rhs(x_bf16, scale=None):
    x = x_bf16.astype(F32)
    if scale is not None:
        x = x * scale
    lane = lax.broadcasted_iota(jnp.int32, x.shape, 1)
    lo = jnp.where(lane < HEAD_DIM, x, 0.0)
    hi = jnp.where(lane >= HEAD_DIM, x, 0.0)
    return jnp.concatenate([lo, hi], axis=0).astype(BF16)


def _fold_rows(x, op, rows):
    n = x.shape[0] // rows
    return op(x.reshape(n, rows, x.shape[1]), axis=0)


def _finish_rows(x, op):
    y = op(x.reshape(x.shape[0] // SUBLANES, SUBLANES, x.shape[1]), axis=0)
    return op(y, axis=0, keepdims=True)


def _chunk_loop(n, body, init):
    def quad(i, cr):
        for k in range(4):
            cr = body(4 * i + k, cr)
        return cr
    carry = lax.fori_loop(0, n // 4, quad, init)
    base = (n // 4) * 4
    carry = lax.cond(n % 4 >= 2, lambda cr: body(base + 1, body(base, cr)), lambda cr: cr, carry)
    return lax.cond(n % 2 == 1, lambda cr: body(n - 1, cr), lambda cr: cr, carry)


def _nt_dot(a, b):
    return lax.dot_general(a, b, (((1,), (1,)), ((), ())), preferred_element_type=F32)


def _key16_to_key32(k16):
    bits = lax.shift_left(jnp.where(k16 >= 0, k16, k16 ^ jnp.int32(0x7FFF)), 16)
    return jnp.where(bits >= 0, bits, bits ^ jnp.int32(0x7FFFFFFF))


def _key_to_f32(key):
    bits = jnp.where(key >= 0, key, key ^ jnp.int32(0x7FFFFFFF))
    return lax.bitcast_convert_type(bits, F32)


def _attn_kernel(qi_ref, q_ref, wi_ref, kk_ref, k_ref, vt_ref, o_ref,
                 sc_ref, sb_ref, xs_ref, acc_ref, out_ref):
    i = pl.program_id(1)
    kc = KEY_CHUNK
    n_chunks = ((i + 1) * ATTN_Q + kc - 1) // kc
    topk = float(TOPK_MAX)
    t_idx = i * ATTN_Q + lax.broadcasted_iota(jnp.int32, (1, ATTN_Q), 1)
    s_iota = lax.broadcasted_iota(jnp.int32, (kc, ATTN_Q), 0)
    n_pairs = N_HEADS // 2

    qi = qi_ref[0]
    idx_rhs = [_pair_rhs(qi[:, p * LANES:(p + 1) * LANES]) for p in range(IDX_HEADS // 2)]
    wi_t = wi_ref[0].T

    def score_body(c, carry):
        start = pl.multiple_of(c * kc, kc)
        kk = kk_ref[0, pl.ds(start, kc), :]
        s = jnp.zeros((kc, ATTN_Q), F32)
        for p in range(IDX_HEADS // 2):
            r = _nt_dot(kk, idx_rhs[p])
            s = s + wi_t[2 * p:2 * p + 1, :] * jnp.maximum(r[:, :ATTN_Q], 0.0)
            s = s + wi_t[2 * p + 1:2 * p + 2, :] * jnp.maximum(r[:, ATTN_Q:], 0.0)
        causal = (start + s_iota) <= t_idx
        s = jnp.where(causal, s, -jnp.inf)
        sc_ref[pl.ds(start, kc), :] = s
        sb_ref[pl.ds(start, kc), :] = s.astype(BF16)
        return carry

    _chunk_loop(n_chunks, score_body, 0)

    n_count = n_chunks * (kc // COUNT_ROWS)

    def count_ge(cand):
        def body(c, acc):
            for half in range(kc // COUNT_ROWS):
                start = pl.multiple_of(c * kc + half * COUNT_ROWS, COUNT_ROWS)
                hit = jnp.where(sc_ref[pl.ds(start, COUNT_ROWS), :] >= cand, 1.0, 0.0)
                acc = acc + _fold_rows(hit, jnp.sum, COUNT_ACC_ROWS)
            return acc
        acc = lax.fori_loop(0, n_chunks, body, jnp.zeros((COUNT_ACC_ROWS, ATTN_Q), F32))
        return _finish_rows(acc, jnp.sum)

    def count_ge_bf16(cand):
        def body(c, acc):
            for half in range(kc // COUNT_ROWS):
                start = pl.multiple_of(c * kc + half * COUNT_ROWS, COUNT_ROWS)
                hit = jnp.where(sb_ref[pl.ds(start, COUNT_ROWS), :] >= cand, jnp.ones((), BF16), jnp.zeros((), BF16))
                parts = [hit[r * PACK_ROWS:(r + 1) * PACK_ROWS] for r in range(COUNT_ROWS // PACK_ROWS)]
                while len(parts) > 1:
                    parts = [a + b for a, b in zip(parts[::2], parts[1::2])]
                acc = acc + parts[0].astype(F32)
            return acc
        acc = lax.fori_loop(0, n_chunks, body, jnp.zeros((PACK_ROWS, ATTN_Q), F32))
        return _finish_rows(acc, jnp.sum)

    c0 = count_ge_bf16(jnp.zeros((1, ATTN_Q), BF16))
    k16_0 = jnp.where(c0 >= topk, jnp.int32(0), jnp.int32(KEY16_MIN))

    def bit16_body(j, k16):
        cand = k16 + lax.shift_left(jnp.int32(1), STAGE1_BITS - 1 - j)
        cnt = count_ge_bf16(_key_to_f32(_key16_to_key32(cand)).astype(BF16))
        return jnp.where(cnt >= topk, cand, k16)

    k16 = lax.fori_loop(0, STAGE1_BITS, bit16_body, k16_0)
    k16 = jnp.where((k16 > 0) & (k16 < BF16_MIN_NORMAL_KEY), 0, k16)
    low = _key16_to_key32(k16) - jnp.int32(STAGE2_BELOW)

    def bit_body(j, carry):
        key, cnt_key = carry
        cand = key + lax.shift_left(jnp.int32(1), STAGE2_BITS - 1 - j)
        cnt = count_ge(_key_to_f32(cand))
        take = cnt >= topk
        return jnp.where(take, cand, key), jnp.where(take, cnt, cnt_key)

    key, cnt_ge = lax.fori_loop(0, STAGE2_BITS, bit_body, (low, jnp.full((1, ATTN_Q), jnp.inf, F32)))
    thr = _key_to_f32(key)
    thr = jnp.where((k16 == KEY16_MIN) | (thr != thr), -jnp.inf, thr)
    excess = jnp.where((cnt_ge > topk) & (thr > -jnp.inf), 1.0, 0.0)
    has_excess = jnp.sum(excess) > 0.0

    @pl.when(has_excess)
    def _():
        r_i = lax.broadcasted_iota(jnp.int32, (kc, kc), 0)
        c_i = lax.broadcasted_iota(jnp.int32, (kc, kc), 1)
        strict_lower = jnp.where(c_i < r_i, 1.0, 0.0).astype(BF16)

        def count_gt(c, acc):
            start = pl.multiple_of(c * COUNT_ROWS, COUNT_ROWS)
            hit = jnp.where(sc_ref[pl.ds(start, COUNT_ROWS), :] > thr, 1.0, 0.0)
            return acc + _fold_rows(hit, jnp.sum, COUNT_ACC_ROWS)

        gt_acc = lax.fori_loop(0, n_count, count_gt, jnp.zeros((COUNT_ACC_ROWS, ATTN_Q), F32))
        need = topk - _finish_rows(gt_acc, jnp.sum)

        def body(c, before):
            start = pl.multiple_of(c * kc, kc)
            s = sc_ref[pl.ds(start, kc), :]
            tie = jnp.where(s == thr, 1.0, 0.0)
            rank = jnp.dot(strict_lower, tie.astype(BF16), preferred_element_type=F32) + before
            keep = (s > thr) | ((s == thr) & (rank < need))
            sc_ref[pl.ds(start, kc), :] = jnp.where(keep, jnp.inf, -jnp.inf)
            return before + jnp.sum(tie, axis=0, keepdims=True)
        lax.fori_loop(0, n_chunks, body, jnp.zeros((1, ATTN_Q), F32))

    q = q_ref[0]
    acc_ref[...] = jnp.zeros(acc_ref.shape, F32)
    neg = jnp.full((HEAD_ACC_ROWS, ATTN_Q), -jnp.inf, F32)
    n_groups = N_HEADS // HEAD_GROUP
    att_rhs = [_pair_rhs(q[:, p * LANES:(p + 1) * LANES], HEAD_DIM ** -0.5) for p in range(n_pairs)]

    def qk_part(start, g, mx):
        if g == 0:
            keep = (sc_ref[pl.ds(start, kc), :] >= thr) & ((start + s_iota) <= t_idx)
            bias = jnp.where(keep, 0.0, MASK_VALUE)
            sc_ref[pl.ds(start, kc), :] = bias
        else:
            bias = sc_ref[pl.ds(start, kc), :]
        out = []
        for p in range(g * HEAD_GROUP // 2, (g + 1) * HEAD_GROUP // 2):
            kp = k_ref[0, pl.ds(start, kc), p * LANES:(p + 1) * LANES]
            lg = _nt_dot(kp, att_rhs[p])
            for hh in range(2):
                hl = 2 * p + hh - g * HEAD_GROUP
                x = lg[:, hh * ATTN_Q:(hh + 1) * ATTN_Q] + bias
                xs_ref[g % 2, pl.ds(start, kc), hl * ATTN_Q:(hl + 1) * ATTN_Q] = x
                out.append(jnp.maximum(mx[hl], _fold_rows(x, jnp.max, HEAD_ACC_ROWS)))
        return tuple(out)

    def pv_part(start, g, m):
        for hl in range(HEAD_GROUP):
            h = g * HEAD_GROUP + hl
            pm = jnp.exp(xs_ref[g % 2, pl.ds(start, kc), hl * ATTN_Q:(hl + 1) * ATTN_Q] - m[hl])
            rows = slice(h * V_ROWS, (h + 1) * V_ROWS)
            acc_ref[rows, :] += jnp.dot(vt_ref[0, rows, pl.ds(start, kc)], pm.astype(BF16),
                                        preferred_element_type=F32)

    m = None
    for s in range(n_groups + 1):
        def body(c, mx, s=s, m=m):
            start = pl.multiple_of(c * kc, kc)
            if s < n_groups:
                mx = qk_part(start, s, mx)
            if s > 0:
                pv_part(start, s - 1, m)
            return mx

        mx = _chunk_loop(n_chunks, body, (neg,) * HEAD_GROUP)
        if s < n_groups:
            m = [_finish_rows(mx[hl], jnp.max) for hl in range(HEAD_GROUP)]
    for h in range(N_HEADS):
        pv = acc_ref[h * V_ROWS:h * V_ROWS + HEAD_DIM, :]
        denom = acc_ref[h * V_ROWS + HEAD_DIM:h * V_ROWS + HEAD_DIM + 1, :]
        out_ref[h * HEAD_DIM:(h + 1) * HEAD_DIM, :] = pv / denom
    o_ref[0] = out_ref[...].T.astype(BF16)


def _attention(qi, q, wi, kk, k, vt):
    B, S, _ = q.shape
    grid = (B, S // ATTN_Q)
    blk = lambda n: pl.BlockSpec((1, ATTN_Q, n), lambda b, i: (b, i, 0))
    full = lambda n: pl.BlockSpec((1, S, n), lambda b, i: (b, 0, 0), pipeline_mode=pl.Buffered(1))
    return pl.pallas_call(
        _attn_kernel, grid=grid,
        in_specs=[blk(ATTN_WIDTH), blk(ATTN_WIDTH), blk(LANES), full(LANES), full(ATTN_WIDTH),
                  pl.BlockSpec((1, N_HEADS * V_ROWS, S), lambda b, i: (b, 0, 0), pipeline_mode=pl.Buffered(1))],
        out_specs=blk(ATTN_WIDTH),
        out_shape=jax.ShapeDtypeStruct((B, S, ATTN_WIDTH), BF16),
        scratch_shapes=[pltpu.VMEM((S, ATTN_Q), F32),
                        pltpu.VMEM((S, ATTN_Q), BF16),
                        pltpu.VMEM((2, S, HEAD_GROUP * ATTN_Q), F32),
                        pltpu.VMEM((N_HEADS * V_ROWS, ATTN_Q), F32),
                        pltpu.VMEM((ATTN_WIDTH, ATTN_Q), F32)],
        compiler_params=pltpu.CompilerParams(
            dimension_semantics=("arbitrary", "arbitrary"), vmem_limit_bytes=VMEM_LIMIT_BYTES),
        name="sparse_attn",
    )(qi, q, wi, kk, k, vt)


def _lane_first(mask, lane):
    return jnp.min(jnp.where(mask, lane, ROUTER_LANES), axis=-1, keepdims=True)


def _router(logits):
    lane = lax.broadcasted_iota(jnp.int32, logits.shape, 1)
    neg = -jnp.inf
    gl = jnp.where(lane < N_GROUPS, logits, neg)
    gmax = jnp.max(gl, axis=-1, keepdims=True)
    gsum = jnp.sum(jnp.exp(gl - gmax), axis=-1, keepdims=True)
    p_g = 1.0 / gsum
    gi = _lane_first(gl == gmax, lane)
    e_lo = N_GROUPS + gi * EXPERTS_PER_GROUP
    in_group = (lane >= e_lo) & (lane < e_lo + EXPERTS_PER_GROUP)
    el = jnp.where(in_group, logits, neg)
    emax = jnp.max(el, axis=-1, keepdims=True)
    ee = jnp.exp(el - emax)
    pe = ee / jnp.sum(ee, axis=-1, keepdims=True)
    pe = jnp.where(in_group, pe, -1.0)
    p1 = jnp.max(pe, axis=-1, keepdims=True)
    l1 = _lane_first(pe == p1, lane)
    pe2 = jnp.where(lane == l1, -1.0, pe)
    p2 = jnp.max(pe2, axis=-1, keepdims=True)
    l2 = _lane_first(pe2 == p2, lane)
    tot = p1 + p2
    c1 = p_g * (p1 / tot)
    c2 = p_g * (p2 / tot)
    return (jnp.where(lane == l1, c1, 0.0) + jnp.where(lane == l2, c2, 0.0)
            + jnp.where(lane == 0, gi.astype(F32), 0.0))


def _mix_kernel(x_ref, mconv_ref, attn_ref, ga_ref, wao_ref, wo_ref, gffn_ref, wr_ref, br_ref,
                x1_ref, h2_ref, gw_ref):
    ya = jnp.dot(attn_ref[...], wao_ref[...], preferred_element_type=F32)
    m = mconv_ref[...] + _sigmoid(ga_ref[...]) * ya
    x1 = x_ref[...] + jnp.dot(m.astype(BF16), wo_ref[...], preferred_element_type=F32)
    x1_ref[...] = x1
    h2 = _rms_rows(x1, gffn_ref[...]).astype(BF16)
    h2_ref[...] = h2
    logits = jnp.dot(h2, wr_ref[...], preferred_element_type=F32) + br_ref[...]
    gw_ref[...] = _router(logits)


def _mix(x, mconv, attn, ga, w_ao, w_o, g_ffn, w_r, b_r):
    T = x.shape[0]
    rows = MIX_ROWS
    row = lambda n: pl.BlockSpec((rows, n), lambda i: (i, 0))
    const = lambda a: pl.BlockSpec(a.shape, lambda i: (0, 0))
    return pl.pallas_call(
        _mix_kernel, grid=(T // rows,),
        in_specs=[row(D_MODEL), row(D_MODEL), row(ATTN_WIDTH), row(D_MODEL),
                  const(w_ao), const(w_o), const(g_ffn), const(w_r), const(b_r)],
        out_specs=(row(D_MODEL), row(D_MODEL), row(ROUTER_LANES)),
        out_shape=(jax.ShapeDtypeStruct((T, D_MODEL), F32),
                   jax.ShapeDtypeStruct((T, D_MODEL), BF16),
                   jax.ShapeDtypeStruct((T, ROUTER_LANES), F32)),
        compiler_params=pltpu.CompilerParams(
            dimension_semantics=("arbitrary",), vmem_limit_bytes=VMEM_LIMIT_BYTES),
        name="mix_router",
    )(x, mconv, attn, ga, w_ao, w_o, g_ffn, w_r, b_r)


def _split3(x):
    a = x.astype(BF16)
    r = x - a.astype(F32)
    b = r.astype(BF16)
    return a, b, (r - b.astype(F32)).astype(BF16)


def _group_experts(h, gwv, g, wg_ref, wu_ref, wd_ref):
    lane = lax.broadcasted_iota(jnp.int32, gwv.shape, 1)
    y = None
    for j in range(EXPERTS_PER_GROUP):
        gate = jnp.dot(h, wg_ref[j], preferred_element_type=F32)
        up = jnp.dot(h, wu_ref[j], preferred_element_type=F32)
        expert_lane = N_GROUPS + g * EXPERTS_PER_GROUP + j
        gwe = jnp.sum(jnp.where(lane == expert_lane, gwv, 0.0), axis=-1, keepdims=True)
        hid = (gate * _sigmoid(gate)) * up * gwe
        yj = jnp.dot(hid.astype(BF16), wd_ref[j], preferred_element_type=F32)
        y = yj if y is None else y + yj
    return y


def _moe_kernel(h2_ref, gw_ref, x1_ref, wg_ref, wu_ref, wd_ref, gfin_ref, o_ref,
                hs_ref, gws_ref, ys_ref, acc_ref, dest_ref, flag_ref):
    g = pl.program_id(1)
    n_sub = MOE_ROWS // MOE_SUB
    seg = N_GROUPS * MOE_CAP

    @pl.when(g == 0)
    def _():
        r_i = lax.broadcasted_iota(jnp.int32, (MOE_SUB, MOE_SUB), 0)
        c_i = lax.broadcasted_iota(jnp.int32, (MOE_SUB, MOE_SUB), 1)
        tri = jnp.where(c_i < r_i, 1.0, 0.0).astype(BF16)
        over = jnp.zeros((1, 1), F32)
        for sub in range(n_sub):
            rows = slice(sub * MOE_SUB, (sub + 1) * MOE_SUB)
            gw = gw_ref[rows, :]
            lane = lax.broadcasted_iota(jnp.int32, gw.shape, 1)
            gi = jnp.sum(jnp.where(lane == 0, gw, 0.0), axis=-1, keepdims=True)
            onehot = jnp.where((lane < N_GROUPS) & (lane.astype(F32) == gi), 1.0, 0.0)
            before = jnp.dot(tri, onehot.astype(BF16), preferred_element_type=F32)
            rank = jnp.sum(onehot * before, axis=-1, keepdims=True)
            over = jnp.maximum(over, jnp.max(rank, axis=0, keepdims=True))
            dest = jnp.where(rank < MOE_CAP, gi * MOE_CAP + rank, -1.0)
            dest_col = jnp.where(lane == 0, dest, 0.0)
            dest_ref[rows, :] = dest_col
            dest_row = dest_col.T[0:1, :]
            slot = lax.broadcasted_iota(jnp.int32, (seg, 1), 0).astype(F32)
            gather = jnp.where(dest_row == slot, 1.0, 0.0).astype(BF16)
            hs_ref[sub] = jnp.dot(gather, h2_ref[rows, :], preferred_element_type=F32).astype(BF16)
            gws_ref[sub] = sum(jnp.dot(gather, p, preferred_element_type=F32) for p in _split3(gw))
        flag_ref[...] = jnp.broadcast_to(over, flag_ref.shape)
        acc_ref[...] = jnp.zeros(acc_ref.shape, F32)

    overflow = jnp.max(flag_ref[0:1, 0:1]) >= MOE_CAP

    @pl.when(jnp.logical_not(overflow))
    def _():
        lo = g * MOE_CAP
        h = jnp.concatenate([hs_ref[sub, pl.ds(lo, MOE_CAP), :] for sub in range(n_sub)], axis=0)
        gwv = jnp.concatenate([gws_ref[sub, pl.ds(lo, MOE_CAP), :] for sub in range(n_sub)], axis=0)
        y = _group_experts(h, gwv, g, wg_ref, wu_ref, wd_ref)
        for sub in range(n_sub):
            ys_ref[sub, pl.ds(lo, MOE_CAP), :] = y[sub * MOE_CAP:(sub + 1) * MOE_CAP]

    @pl.when(overflow)
    def _():
        acc_ref[...] += _group_experts(h2_ref[...], gw_ref[...], g, wg_ref, wu_ref, wd_ref)

    @pl.when(g == N_GROUPS - 1)
    def _():
        @pl.when(jnp.logical_not(overflow))
        def _():
            for sub in range(n_sub):
                rows = slice(sub * MOE_SUB, (sub + 1) * MOE_SUB)
                slot = lax.broadcasted_iota(jnp.int32, (1, seg), 1).astype(F32)
                scatter = jnp.where(dest_ref[rows, 0:1] == slot, 1.0, 0.0).astype(BF16)
                ys = ys_ref[sub]
                y_hi = ys.astype(BF16)
                y_lo = (ys - y_hi.astype(F32)).astype(BF16)
                acc_ref[rows, :] = (jnp.dot(scatter, y_hi, preferred_element_type=F32)
                                    + jnp.dot(scatter, y_lo, preferred_element_type=F32))
        o_ref[...] = _rms_rows(x1_ref[...] + acc_ref[...], gfin_ref[...])


def _moe(h2, gw, x1, w_gate, w_up, w_down, g_final):
    T = h2.shape[0]
    rows = MOE_ROWS
    n_sub = rows // MOE_SUB
    seg = N_GROUPS * MOE_CAP
    row = lambda n: pl.BlockSpec((rows, n), lambda i, g: (i, 0))
    return pl.pallas_call(
        _moe_kernel, grid=(T // rows, N_GROUPS),
        in_specs=[row(D_MODEL), row(ROUTER_LANES),
                  pl.BlockSpec((rows, D_MODEL), lambda i, g: (i, 0), pipeline_mode=pl.Buffered(1)),
                  pl.BlockSpec((EXPERTS_PER_GROUP, D_MODEL, D_EXPERT), lambda i, g: (g, 0, 0)),
                  pl.BlockSpec((EXPERTS_PER_GROUP, D_MODEL, D_EXPERT), lambda i, g: (g, 0, 0)),
                  pl.BlockSpec((EXPERTS_PER_GROUP, D_EXPERT, D_MODEL), lambda i, g: (g, 0, 0)),
                  pl.BlockSpec((1, D_MODEL), lambda i, g: (0, 0))],
        out_specs=row(D_MODEL),
        out_shape=jax.ShapeDtypeStruct((T, D_MODEL), F32),
        scratch_shapes=[pltpu.VMEM((n_sub, seg, D_MODEL), BF16),
                        pltpu.VMEM((n_sub, seg, ROUTER_LANES), F32),
                        pltpu.VMEM((n_sub, seg, D_MODEL), F32),
                        pltpu.VMEM((rows, D_MODEL), F32),
                        pltpu.VMEM((rows, ROUTER_LANES), F32),
                        pltpu.VMEM((SUBLANES, ROUTER_LANES), F32)],
        compiler_params=pltpu.CompilerParams(
            dimension_semantics=("arbitrary", "arbitrary"), vmem_limit_bytes=VMEM_LIMIT_BYTES),
        name="moe_final",
    )(h2, gw, x1, w_gate, w_up, w_down, g_final)


def _rope_inputs(positions):
    half = HEAD_DIM // 2
    inv = 1.0 / (ROPE_THETA ** (jnp.arange(0, HEAD_DIM, 2, dtype=F32) / HEAD_DIM))
    return positions.astype(F32)[..., None], inv[jnp.arange(LANES) % half].reshape(1, LANES)


def _layer(x, pos, inv_freq, g_mix, w_in, w_dw, b_dw, ln_g, ln_b, w_conv_out, w_attn_out, w_o,
           g_ffn, w_rg, b_rg, w_re, b_re, w_gate, w_up, w_down, g_final):
    B, S, D = x.shape
    T = B * S
    offs = [0]
    for n in IN_SIZES:
        offs.append(offs[-1] + n)
    col = lambda j: w_in[:, offs[j]:offs[j + 1]]
    w_ki, w_wi = col(5), col(6)
    ws = (
        col(0), col(1), col(2), col(3), col(4),
        jnp.concatenate([w_ki, w_ki, jnp.pad(w_wi, ((0, 0), (0, LANES - IDX_HEADS)))], axis=1),
        col(7), col(8),
    )
    ws = tuple(w.astype(BF16) for w in ws)
    u, q, k, vt, qi, kk, wi, gc, ga = _inproj(x, g_mix.reshape(1, D), pos, inv_freq, ws)

    mconv = _conv_branch(u, gc, w_dw.reshape(CONV_KERNEL, CONV_WIDTH), b_dw.reshape(1, -1),
                         ln_g.reshape(1, -1), ln_b.reshape(1, -1), w_conv_out.astype(BF16))
    attn = _attention(qi, q, wi, kk, k, vt)

    n_r = N_GROUPS + N_EXPERTS
    w_r = jnp.concatenate([w_rg, w_re.reshape(D, N_EXPERTS)], axis=1)
    w_r = jnp.pad(w_r, ((0, 0), (0, ROUTER_LANES - n_r))).astype(BF16)
    b_r = jnp.pad(jnp.concatenate([b_rg, b_re.reshape(N_EXPERTS)]), (0, ROUTER_LANES - n_r)).reshape(1, -1)
    x1, h2, gw = _mix(x.reshape(T, D), mconv.reshape(T, D), attn.reshape(T, ATTN_WIDTH), ga.reshape(T, D),
                      w_attn_out.astype(BF16), w_o.astype(BF16), g_ffn.reshape(1, D), w_r, b_r)
    out = _moe(h2, gw, x1, w_gate.astype(BF16), w_up.astype(BF16), w_down.astype(BF16), g_final.reshape(1, D))
    return out.reshape(B, S, D)


def kernel(x, positions, g_mix, w_in, w_dw, b_dw, ln_g, ln_b, w_conv_out, w_attn_out, w_o, g_ffn,
           w_rg, b_rg, w_re, b_re, w_gate, w_up, w_down, g_final):
    depth = g_mix.shape[0]
    assert depth == 1, "final norm is fused into the single layer's MoE call"
    pos, inv_freq = _rope_inputs(positions)
    return _layer(x, pos, inv_freq, g_mix[0], w_in[0], w_dw[0], b_dw[0], ln_g[0], ln_b[0],
                  w_conv_out[0], w_attn_out[0], w_o[0], g_ffn[0], w_rg[0], b_rg[0], w_re[0],
                  b_re[0], w_gate[0], w_up[0], w_down[0], g_final)
```

```python
import jax
import jax.numpy as jnp
from jax import lax
from jax.experimental import pallas as pl
from jax.experimental.pallas import tpu as pltpu

F32 = jnp.float32
BF16 = jnp.bfloat16

D_MODEL = 1024
CONV_WIDTH = 512
CONV_KERNEL = 31
N_HEADS = 8
HEAD_DIM = 64
ATTN_WIDTH = N_HEADS * HEAD_DIM
IDX_HEADS = 8
IDX_DIM = 64
TOPK_MAX = 256
ROPE_THETA = 10000.0
N_GROUPS = 4
EXPERTS_PER_GROUP = 4
N_EXPERTS = N_GROUPS * EXPERTS_PER_GROUP
D_EXPERT = 256
EPS = 1e-6
IN_SIZES = (2 * CONV_WIDTH, ATTN_WIDTH, ATTN_WIDTH, ATTN_WIDTH,
            IDX_HEADS * IDX_DIM, IDX_DIM, IDX_HEADS, D_MODEL, D_MODEL)

LANES = 128
SUBLANES = 8
VMEM_LIMIT_BYTES = 56 * 1024 * 1024

PROJ_ROWS = 512
CONV_ROWS = 512
CONV_HALO = 32
CONV_SUB = 64
ATTN_Q = 256
HEAD_GROUP = 4
ONES_ROWS = 16
V_ROWS = HEAD_DIM + ONES_ROWS
KEY_CHUNK = 512
COUNT_ROWS = 256
COUNT_ACC_ROWS = 32
HEAD_ACC_ROWS = 8
MIX_ROWS = 1024
MIX_SUB = 512
MOE_ROWS = 1024
MOE_SUB = 256
MOE_CAP = 96
ROUTER_LANES = 128
MASK_VALUE = -1e30
KEY16_MIN = -2 ** 15
BF16_MIN_NORMAL_KEY = 1 << 7
PACK_ROWS = 16
STAGE1_BITS = 15
STAGE2_BITS = 17
STAGE2_BELOW = 36000


def _rms_rows(x, g):
    ms = jnp.mean(x * x, axis=-1, keepdims=True)
    return x * lax.rsqrt(ms + EPS) * g


def _sigmoid(x):
    return jax.nn.sigmoid(x)


def _rope128(z, cos, sin, first_half):
    rot = jnp.where(first_half, pltpu.roll(z, LANES - HEAD_DIM // 2, 1),
                    pltpu.roll(z, HEAD_DIM // 2, 1))
    return z * cos + rot * sin


def _inproj_kernel(x_ref, g_ref, pos_ref, inv_ref, wu_ref, wq_ref, wk_ref, wv_ref, wqi_ref,
                   wkw_ref, wgc_ref, wga_ref,
                   u_ref, q_ref, k_ref, vt_ref, qi_ref, kk_ref, wi_ref, gc_ref, ga_ref):
    h = _rms_rows(x_ref[0], g_ref[...]).astype(BF16)
    ang = pos_ref[0] * inv_ref[...]
    lane = lax.broadcasted_iota(jnp.int32, ang.shape, 1)
    first_half = (lane % HEAD_DIM) < (HEAD_DIM // 2)
    cos = jnp.cos(ang)
    sin = jnp.sin(ang)
    sin = jnp.where(first_half, -sin, sin)

    def proj(w_ref):
        return jnp.dot(h, w_ref[...], preferred_element_type=F32)

    u_ref[0] = proj(wu_ref)
    gc_ref[0] = proj(wgc_ref)
    ga_ref[0] = proj(wga_ref)
    for w_ref, o_ref in ((wq_ref, q_ref), (wk_ref, k_ref), (wqi_ref, qi_ref)):
        z = proj(w_ref)
        for c in range(ATTN_WIDTH // LANES):
            sl = slice(c * LANES, (c + 1) * LANES)
            o_ref[0, :, sl] = _rope128(z[:, sl], cos, sin, first_half).astype(BF16)
    kw = proj(wkw_ref)
    kk_ref[0] = _rope128(kw[:, :LANES], cos, sin, first_half).astype(BF16)
    idx_scale = (IDX_HEADS ** -0.5) * (IDX_DIM ** -0.5)
    wi_ref[0] = kw[:, LANES:] * idx_scale
    vt = proj(wv_ref).T.astype(BF16)
    ones = jnp.ones((ONES_ROWS, vt.shape[1]), BF16)
    vt_ref[0] = jnp.concatenate(
        [piece for hd in range(N_HEADS) for piece in (vt[hd * HEAD_DIM:(hd + 1) * HEAD_DIM], ones)], axis=0)


def _inproj(x, g_mix, pos, inv_freq, ws):
    B, S, D = x.shape
    rows = PROJ_ROWS
    grid = (B, S // rows)
    row_spec = lambda n: pl.BlockSpec((1, rows, n), lambda b, i: (b, i, 0))
    w_spec = lambda w: pl.BlockSpec(w.shape, lambda b, i: (0, 0), pipeline_mode=pl.Buffered(1))
    out_shape = (
        jax.ShapeDtypeStruct((B, S, 2 * CONV_WIDTH), F32),
        jax.ShapeDtypeStruct((B, S, ATTN_WIDTH), BF16),
        jax.ShapeDtypeStruct((B, S, ATTN_WIDTH), BF16),
        jax.ShapeDtypeStruct((B, N_HEADS * V_ROWS, S), BF16),
        jax.ShapeDtypeStruct((B, S, ATTN_WIDTH), BF16),
        jax.ShapeDtypeStruct((B, S, LANES), BF16),
        jax.ShapeDtypeStruct((B, S, LANES), F32),
        jax.ShapeDtypeStruct((B, S, D_MODEL), F32),
        jax.ShapeDtypeStruct((B, S, D_MODEL), F32),
    )
    out_specs = (
        row_spec(2 * CONV_WIDTH), row_spec(ATTN_WIDTH), row_spec(ATTN_WIDTH),
        pl.BlockSpec((1, N_HEADS * V_ROWS, rows), lambda b, i: (b, 0, i)),
        row_spec(ATTN_WIDTH), row_spec(LANES), row_spec(LANES), row_spec(D_MODEL), row_spec(D_MODEL),
    )
    in_specs = [row_spec(D), pl.BlockSpec((1, D), lambda b, i: (0, 0)), row_spec(1),
                pl.BlockSpec((1, LANES), lambda b, i: (0, 0))]
    in_specs += [w_spec(w) for w in ws]
    return pl.pallas_call(
        _inproj_kernel, grid=grid, in_specs=in_specs, out_specs=out_specs, out_shape=out_shape,
        compiler_params=pltpu.CompilerParams(
            dimension_semantics=("arbitrary", "arbitrary"), vmem_limit_bytes=VMEM_LIMIT_BYTES),
        name="inproj",
    )(x, g_mix, pos, inv_freq, *ws)


def _conv_kernel(u_ref, uh_ref, gc_ref, wdw_ref, bdw_ref, lng_ref, lnb_ref, wout_ref,
                 o_ref, g_buf, s_buf):
    i = pl.program_id(1)
    rows = u_ref.shape[1]
    uh = uh_ref[0]
    gh = uh[:, :CONV_WIDTH] * _sigmoid(uh[:, CONV_WIDTH:])
    g_buf[0, 0:CONV_HALO, :] = jnp.where(i > 0, gh, 0.0)
    um = u_ref[0]
    g_buf[0, CONV_HALO:CONV_HALO + rows, :] = um[:, :CONV_WIDTH] * _sigmoid(um[:, CONV_WIDTH:])
    span = CONV_HALO + rows - SUBLANES
    for r in range(1, SUBLANES):
        for base in range(0, span, CONV_SUB):
            n = min(CONV_SUB, span - base)
            g_buf[r, base:base + n, :] = g_buf[0, pl.ds(base + r, n), :]
    first = CONV_HALO - (CONV_KERNEL - 1)
    for rr in range(rows // CONV_SUB):
        acc = jnp.zeros((CONV_SUB, CONV_WIDTH), F32)
        for t in range(CONV_KERNEL):
            shift = (first + t) % SUBLANES
            row0 = rr * CONV_SUB + (first + t) - shift
            acc = acc + wdw_ref[t:t + 1, :] * g_buf[shift, row0:row0 + CONV_SUB, :]
        c = acc + bdw_ref[...]
        mu = jnp.mean(c, axis=-1, keepdims=True)
        d = c - mu
        var = jnp.mean(d * d, axis=-1, keepdims=True)
        n = d * lax.rsqrt(var + EPS) * lng_ref[...] + lnb_ref[...]
        s_buf[rr * CONV_SUB:(rr + 1) * CONV_SUB, :] = (n * _sigmoid(n)).astype(BF16)
    y = jnp.dot(s_buf[...], wout_ref[...], preferred_element_type=F32)
    o_ref[0] = _sigmoid(gc_ref[0]) * y


def _conv_branch(u, gc, w_dw, b_dw, ln_g, ln_b, w_out):
    B, S, _ = u.shape
    rows = CONV_ROWS
    halo_per_tile = rows // CONV_HALO
    grid = (B, S // rows)
    vec = lambda n: pl.BlockSpec((1, n), lambda b, i: (0, 0))
    return pl.pallas_call(
        _conv_kernel, grid=grid,
        in_specs=[
            pl.BlockSpec((1, rows, 2 * CONV_WIDTH), lambda b, i: (b, i, 0)),
            pl.BlockSpec((1, CONV_HALO, 2 * CONV_WIDTH),
                         lambda b, i: (b, jnp.maximum(i * halo_per_tile - 1, 0), 0)),
            pl.BlockSpec((1, rows, D_MODEL), lambda b, i: (b, i, 0)),
            pl.BlockSpec((CONV_KERNEL, CONV_WIDTH), lambda b, i: (0, 0)),
            vec(CONV_WIDTH), vec(CONV_WIDTH), vec(CONV_WIDTH),
            pl.BlockSpec((CONV_WIDTH, D_MODEL), lambda b, i: (0, 0)),
        ],
        out_specs=pl.BlockSpec((1, rows, D_MODEL), lambda b, i: (b, i, 0)),
        out_shape=jax.ShapeDtypeStruct((B, S, D_MODEL), F32),
        scratch_shapes=[pltpu.VMEM((SUBLANES, CONV_HALO + rows, CONV_WIDTH), F32),
                        pltpu.VMEM((rows, CONV_WIDTH), BF16)],
        compiler_params=pltpu.CompilerParams(
            dimension_semantics=("arbitrary", "arbitrary"), vmem_limit_bytes=VMEM_LIMIT_BYTES),
        name="conv_branch",
    )(u, u, gc, w_dw, b_dw, ln_g, ln_b, w_out)


def _pair_rhs(x_bf16, scale=None):
    x = x_bf16.astype(F32)
    if scale is not None:
        x = x * scale
    lane = lax.broadcasted_iota(jnp.int32, x.shape, 1)
    lo = jnp.where(lane < HEAD_DIM, x, 0.0)
    hi = jnp.where(lane >= HEAD_DIM, x, 0.0)
    return jnp.concatenate([lo, hi], axis=0).astype(BF16)


def _fold_rows(x, op, rows):
    n = x.shape[0] // rows
    return op(x.reshape(n, rows, x.shape[1]), axis=0)


def _finish_rows(x, op):
    y = op(x.reshape(x.shape[0] // SUBLANES, SUBLANES, x.shape[1]), axis=0)
    return op(y, axis=0, keepdims=True)


def _chunk_loop(n, body, init):
    def quad(i, cr):
        for k in range(4):
            cr = body(4 * i + k, cr)
        return cr
    carry = lax.fori_loop(0, n // 4, quad, init)
    base = (n // 4) * 4
    carry = lax.cond(n % 4 >= 2, lambda cr: body(base + 1, body(base, cr)), lambda cr: cr, carry)
    return lax.cond(n % 2 == 1, lambda cr: body(n - 1, cr), lambda cr: cr, carry)


def _nt_dot(a, b):
    return lax.dot_general(a, b, (((1,), (1,)), ((), ())), preferred_element_type=F32)


def _key16_to_key32(k16):
    bits = lax.shift_left(jnp.where(k16 >= 0, k16, k16 ^ jnp.int32(0x7FFF)), 16)
    return jnp.where(bits >= 0, bits, bits ^ jnp.int32(0x7FFFFFFF))


def _key_to_f32(key):
    bits = jnp.where(key >= 0, key, key ^ jnp.int32(0x7FFFFFFF))
    return lax.bitcast_convert_type(bits, F32)


def _attn_kernel(qi_ref, q_ref, wi_ref, kk_ref, k_ref, vt_ref, o_ref,
                 sc_ref, sb_ref, xs_ref, acc_ref, out_ref):
    i = pl.program_id(1)
    kc = KEY_CHUNK
    n_chunks = ((i + 1) * ATTN_Q + kc - 1) // kc
    topk = float(TOPK_MAX)
    t_idx = i * ATTN_Q + lax.broadcasted_iota(jnp.int32, (1, ATTN_Q), 1)
    s_iota = lax.broadcasted_iota(jnp.int32, (kc, ATTN_Q), 0)
    n_pairs = N_HEADS // 2

    qi = qi_ref[0]
    idx_rhs = [_pair_rhs(qi[:, p * LANES:(p + 1) * LANES]) for p in range(IDX_HEADS // 2)]
    wi_t = wi_ref[0].T

    def score_body(c, carry):
        start = pl.multiple_of(c * kc, kc)
        kk = kk_ref[0, pl.ds(start, kc), :]
        s = jnp.zeros((kc, ATTN_Q), F32)
        for p in range(IDX_HEADS // 2):
            r = _nt_dot(kk, idx_rhs[p])
            s = s + wi_t[2 * p:2 * p + 1, :] * jnp.maximum(r[:, :ATTN_Q], 0.0)
            s = s + wi_t[2 * p + 1:2 * p + 2, :] * jnp.maximum(r[:, ATTN_Q:], 0.0)
        causal = (start + s_iota) <= t_idx
        s = jnp.where(causal, s, -jnp.inf)
        sc_ref[pl.ds(start, kc), :] = s
        sb_ref[pl.ds(start, kc), :] = s.astype(BF16)
        return carry

    _chunk_loop(n_chunks, score_body, 0)

    n_count = n_chunks * (kc // COUNT_ROWS)

    def count_ge(cand):
        def body(c, acc):
            for half in range(kc // COUNT_ROWS):
                start = pl.multiple_of(c * kc + half * COUNT_ROWS, COUNT_ROWS)
                hit = jnp.where(sc_ref[pl.ds(start, COUNT_ROWS), :] >= cand, 1.0, 0.0)
                acc = acc + _fold_rows(hit, jnp.sum, COUNT_ACC_ROWS)
            return acc
        acc = lax.fori_loop(0, n_chunks, body, jnp.zeros((COUNT_ACC_ROWS, ATTN_Q), F32))
        return _finish_rows(acc, jnp.sum)

    def count_ge_bf16(cand):
        def body(c, acc):
            for half in range(kc // COUNT_ROWS):
                start = pl.multiple_of(c * kc + half * COUNT_ROWS, COUNT_ROWS)
                hit = jnp.where(sb_ref[pl.ds(start, COUNT_ROWS), :] >= cand, jnp.ones((), BF16), jnp.zeros((), BF16))
                parts = [hit[r * PACK_ROWS:(r + 1) * PACK_ROWS] for r in range(COUNT_ROWS // PACK_ROWS)]
                while len(parts) > 1:
                    parts = [a + b for a, b in zip(parts[::2], parts[1::2])]
                acc = acc + parts[0].astype(F32)
            return acc
        acc = lax.fori_loop(0, n_chunks, body, jnp.zeros((PACK_ROWS, ATTN_Q), F32))
        return _finish_rows(acc, jnp.sum)

    c0 = count_ge_bf16(jnp.zeros((1, ATTN_Q), BF16))
    k16_0 = jnp.where(c0 >= topk, jnp.int32(0), jnp.int32(KEY16_MIN))

    def bit16_body(j, k16):
        cand = k16 + lax.shift_left(jnp.int32(1), STAGE1_BITS - 1 - j)
        cnt = count_ge_bf16(_key_to_f32(_key16_to_key32(cand)).astype(BF16))
        return jnp.where(cnt >= topk, cand, k16)

    k16 = lax.fori_loop(0, STAGE1_BITS, bit16_body, k16_0)
    k16 = jnp.where((k16 > 0) & (k16 < BF16_MIN_NORMAL_KEY), 0, k16)
    low = _key16_to_key32(k16) - jnp.int32(STAGE2_BELOW)

    def bit_body(j, carry):
        key, cnt_key = carry
        cand = key + lax.shift_left(jnp.int32(1), STAGE2_BITS - 1 - j)
        cnt = count_ge(_key_to_f32(cand))
        take = cnt >= topk
        return jnp.where(take, cand, key), jnp.where(take, cnt, cnt_key)

    key, cnt_ge = lax.fori_loop(0, STAGE2_BITS, bit_body, (low, jnp.full((1, ATTN_Q), jnp.inf, F32)))
    thr = _key_to_f32(key)
    thr = jnp.where((k16 == KEY16_MIN) | (thr != thr), -jnp.inf, thr)
    excess = jnp.where((cnt_ge > topk) & (thr > -jnp.inf), 1.0, 0.0)
    has_excess = jnp.sum(excess) > 0.0

    @pl.when(has_excess)
    def _():
        r_i = lax.broadcasted_iota(jnp.int32, (kc, kc), 0)
        c_i = lax.broadcasted_iota(jnp.int32, (kc, kc), 1)
        strict_lower = jnp.where(c_i < r_i, 1.0, 0.0).astype(BF16)

        def count_gt(c, acc):
            start = pl.multiple_of(c * COUNT_ROWS, COUNT_ROWS)
            hit = jnp.where(sc_ref[pl.ds(start, COUNT_ROWS), :] > thr, 1.0, 0.0)
            return acc + _fold_rows(hit, jnp.sum, COUNT_ACC_ROWS)

        gt_acc = lax.fori_loop(0, n_count, count_gt, jnp.zeros((COUNT_ACC_ROWS, ATTN_Q), F32))
        need = topk - _finish_rows(gt_acc, jnp.sum)

        def body(c, before):
            start = pl.multiple_of(c * kc, kc)
            s = sc_ref[pl.ds(start, kc), :]
            tie = jnp.where(s == thr, 1.0, 0.0)
            rank = jnp.dot(strict_lower, tie.astype(BF16), preferred_element_type=F32) + before
            keep = (s > thr) | ((s == thr) & (rank < need))
            sc_ref[pl.ds(start, kc), :] = jnp.where(keep, jnp.inf, -jnp.inf)
            return before + jnp.sum(tie, axis=0, keepdims=True)
        lax.fori_loop(0, n_chunks, body, jnp.zeros((1, ATTN_Q), F32))

    q = q_ref[0]
    acc_ref[...] = jnp.zeros(acc_ref.shape, F32)
    neg = jnp.full((HEAD_ACC_ROWS, ATTN_Q), -jnp.inf, F32)
    n_groups = N_HEADS // HEAD_GROUP
    att_rhs = [_pair_rhs(q[:, p * LANES:(p + 1) * LANES], HEAD_DIM ** -0.5) for p in range(n_pairs)]

    def qk_part(start, g, mx):
        if g == 0:
            keep = (sc_ref[pl.ds(start, kc), :] >= thr) & ((start + s_iota) <= t_idx)
            bias = jnp.where(keep, 0.0, MASK_VALUE)
            sc_ref[pl.ds(start, kc), :] = bias
        else:
            bias = sc_ref[pl.ds(start, kc), :]
        out = []
        for p in range(g * HEAD_GROUP // 2, (g + 1) * HEAD_GROUP // 2):
            kp = k_ref[0, pl.ds(start, kc), p * LANES:(p + 1) * LANES]
            lg = _nt_dot(kp, att_rhs[p])
            for hh in range(2):
                hl = 2 * p + hh - g * HEAD_GROUP
                x = lg[:, hh * ATTN_Q:(hh + 1) * ATTN_Q] + bias
                xs_ref[g % 2, pl.ds(start, kc), hl * ATTN_Q:(hl + 1) * ATTN_Q] = x
                out.append(jnp.maximum(mx[hl], _fold_rows(x, jnp.max, HEAD_ACC_ROWS)))
        return tuple(out)

    def pv_part(start, g, m):
        for hl in range(HEAD_GROUP):
            h = g * HEAD_GROUP + hl
            pm = jnp.exp(xs_ref[g % 2, pl.ds(start, kc), hl * ATTN_Q:(hl + 1) * ATTN_Q] - m[hl])
            rows = slice(h * V_ROWS, (h + 1) * V_ROWS)
            acc_ref[rows, :] += jnp.dot(vt_ref[0, rows, pl.ds(start, kc)], pm.astype(BF16),
                                        preferred_element_type=F32)

    m = None
    for s in range(n_groups + 1):
        def body(c, mx, s=s, m=m):
            start = pl.multiple_of(c * kc, kc)
            if s < n_groups:
                mx = qk_part(start, s, mx)
            if s > 0:
                pv_part(start, s - 1, m)
            return mx

        mx = _chunk_loop(n_chunks, body, (neg,) * HEAD_GROUP)
        if s < n_groups:
            m = [_finish_rows(mx[hl], jnp.max) for hl in range(HEAD_GROUP)]
    for h in range(N_HEADS):
        pv = acc_ref[h * V_ROWS:h * V_ROWS + HEAD_DIM, :]
        denom = acc_ref[h * V_ROWS + HEAD_DIM:h * V_ROWS + HEAD_DIM + 1, :]
        out_ref[h * HEAD_DIM:(h + 1) * HEAD_DIM, :] = pv / denom
    o_ref[0] = out_ref[...].T.astype(BF16)


def _attention(qi, q, wi, kk, k, vt):
    B, S, _ = q.shape
    grid = (B, S // ATTN_Q)
    blk = lambda n: pl.BlockSpec((1, ATTN_Q, n), lambda b, i: (b, i, 0))
    full = lambda n: pl.BlockSpec((1, S, n), lambda b, i: (b, 0, 0), pipeline_mode=pl.Buffered(1))
    return pl.pallas_call(
        _attn_kernel, grid=grid,
        in_specs=[blk(ATTN_WIDTH), blk(ATTN_WIDTH), blk(LANES), full(LANES), full(ATTN_WIDTH),
                  pl.BlockSpec((1, N_HEADS * V_ROWS, S), lambda b, i: (b, 0, 0), pipeline_mode=pl.Buffered(1))],
        out_specs=blk(ATTN_WIDTH),
        out_shape=jax.ShapeDtypeStruct((B, S, ATTN_WIDTH), BF16),
        scratch_shapes=[pltpu.VMEM((S, ATTN_Q), F32),
                        pltpu.VMEM((S, ATTN_Q), BF16),
                        pltpu.VMEM((2, S, HEAD_GROUP * ATTN_Q), F32),
                        pltpu.VMEM((N_HEADS * V_ROWS, ATTN_Q), F32),
                        pltpu.VMEM((ATTN_WIDTH, ATTN_Q), F32)],
        compiler_params=pltpu.CompilerParams(
            dimension_semantics=("arbitrary", "arbitrary"), vmem_limit_bytes=VMEM_LIMIT_BYTES),
        name="sparse_attn",
    )(qi, q, wi, kk, k, vt)


def _lane_first(mask, lane):
    return jnp.min(jnp.where(mask, lane, ROUTER_LANES), axis=-1, keepdims=True)


def _router(logits):
    lane = lax.broadcasted_iota(jnp.int32, logits.shape, 1)
    neg = -jnp.inf
    gl = jnp.where(lane < N_GROUPS, logits, neg)
    gmax = jnp.max(gl, axis=-1, keepdims=True)
    gsum = jnp.sum(jnp.exp(gl - gmax), axis=-1, keepdims=True)
    p_g = 1.0 / gsum
    gi = _lane_first(gl == gmax, lane)
    e_lo = N_GROUPS + gi * EXPERTS_PER_GROUP
    in_group = (lane >= e_lo) & (lane < e_lo + EXPERTS_PER_GROUP)
    el = jnp.where(in_group, logits, neg)
    emax = jnp.max(el, axis=-1, keepdims=True)
    ee = jnp.exp(el - emax)
    pe = ee / jnp.sum(ee, axis=-1, keepdims=True)
    pe = jnp.where(in_group, pe, -1.0)
    p1 = jnp.max(pe, axis=-1, keepdims=True)
    l1 = _lane_first(pe == p1, lane)
    pe2 = jnp.where(lane == l1, -1.0, pe)
    p2 = jnp.max(pe2, axis=-1, keepdims=True)
    l2 = _lane_first(pe2 == p2, lane)
    tot = p1 + p2
    c1 = p_g * (p1 / tot)
    c2 = p_g * (p2 / tot)
    return (jnp.where(lane == l1, c1, 0.0) + jnp.where(lane == l2, c2, 0.0)
            + jnp.where(lane == 0, gi.astype(F32), 0.0))


def _mix_kernel(x_ref, mconv_ref, attn_ref, ga_ref, wao_ref, wo_ref, gffn_ref, wr_ref, br_ref,
                x1_ref, h2_ref, gw_ref):
    for r0 in range(0, x_ref.shape[0], MIX_SUB):
        sl = slice(r0, r0 + MIX_SUB)
        ya = jnp.dot(attn_ref[sl, :], wao_ref[...], preferred_element_type=F32)
        m = mconv_ref[sl, :] + _sigmoid(ga_ref[sl, :]) * ya
        x1 = x_ref[sl, :] + jnp.dot(m.astype(BF16), wo_ref[...], preferred_element_type=F32)
        x1_ref[sl, :] = x1
        h2 = _rms_rows(x1, gffn_ref[...]).astype(BF16)
        h2_ref[sl, :] = h2
        logits = jnp.dot(h2, wr_ref[...], preferred_element_type=F32) + br_ref[...]
        gw_ref[sl, :] = _router(logits)


def _mix(x, mconv, attn, ga, w_ao, w_o, g_ffn, w_r, b_r):
    T = x.shape[0]
    rows = MIX_ROWS
    row = lambda n: pl.BlockSpec((rows, n), lambda i: (i, 0))
    const = lambda a: pl.BlockSpec(a.shape, lambda i: (0, 0))
    return pl.pallas_call(
        _mix_kernel, grid=(T // rows,),
        in_specs=[row(D_MODEL), row(D_MODEL), row(ATTN_WIDTH), row(D_MODEL),
                  const(w_ao), const(w_o), const(g_ffn), const(w_r), const(b_r)],
        out_specs=(row(D_MODEL), row(D_MODEL), row(ROUTER_LANES)),
        out_shape=(jax.ShapeDtypeStruct((T, D_MODEL), F32),
                   jax.ShapeDtypeStruct((T, D_MODEL), BF16),
                   jax.ShapeDtypeStruct((T, ROUTER_LANES), F32)),
        compiler_params=pltpu.CompilerParams(
            dimension_semantics=("arbitrary",), vmem_limit_bytes=VMEM_LIMIT_BYTES),
        name="mix_router",
    )(x, mconv, attn, ga, w_ao, w_o, g_ffn, w_r, b_r)


def _split3(x):
    a = x.astype(BF16)
    r = x - a.astype(F32)
    b = r.astype(BF16)
    return a, b, (r - b.astype(F32)).astype(BF16)


def _group_experts(h, gwv, g, wg_ref, wu_ref, wd_ref):
    lane = lax.broadcasted_iota(jnp.int32, gwv.shape, 1)
    y = None
    for j in range(EXPERTS_PER_GROUP):
        gate = jnp.dot(h, wg_ref[j], preferred_element_type=F32)
        up = jnp.dot(h, wu_ref[j], preferred_element_type=F32)
        expert_lane = N_GROUPS + g * EXPERTS_PER_GROUP + j
        gwe = jnp.sum(jnp.where(lane == expert_lane, gwv, 0.0), axis=-1, keepdims=True)
        hid = (gate * _sigmoid(gate)) * up * gwe
        yj = jnp.dot(hid.astype(BF16), wd_ref[j], preferred_element_type=F32)
        y = yj if y is None else y + yj
    return y


def _moe_kernel(h2_ref, gw_ref, x1_ref, wg_ref, wu_ref, wd_ref, gfin_ref, o_ref,
                hs_ref, gws_ref, ys_ref, acc_ref, dest_ref, flag_ref):
    g = pl.program_id(1)
    n_sub = MOE_ROWS // MOE_SUB
    seg = N_GROUPS * MOE_CAP

    @pl.when(g == 0)
    def _():
        r_i = lax.broadcasted_iota(jnp.int32, (MOE_SUB, MOE_SUB), 0)
        c_i = lax.broadcasted_iota(jnp.int32, (MOE_SUB, MOE_SUB), 1)
        tri = jnp.where(c_i < r_i, 1.0, 0.0).astype(BF16)
        over = jnp.zeros((1, 1), F32)
        for sub in range(n_sub):
            rows = slice(sub * MOE_SUB, (sub + 1) * MOE_SUB)
            gw = gw_ref[rows, :]
            lane = lax.broadcasted_iota(jnp.int32, gw.shape, 1)
            gi = jnp.sum(jnp.where(lane == 0, gw, 0.0), axis=-1, keepdims=True)
            onehot = jnp.where((lane < N_GROUPS) & (lane.astype(F32) == gi), 1.0, 0.0)
            before = jnp.dot(tri, onehot.astype(BF16), preferred_element_type=F32)
            rank = jnp.sum(onehot * before, axis=-1, keepdims=True)
            over = jnp.maximum(over, jnp.max(rank, axis=0, keepdims=True))
            dest = jnp.where(rank < MOE_CAP, gi * MOE_CAP + rank, -1.0)
            dest_col = jnp.where(lane == 0, dest, 0.0)
            dest_ref[rows, :] = dest_col
            dest_row = dest_col.T[0:1, :]
            slot = lax.broadcasted_iota(jnp.int32, (seg, 1), 0).astype(F32)
            gather = jnp.where(dest_row == slot, 1.0, 0.0).astype(BF16)
            hs_ref[sub] = jnp.dot(gather, h2_ref[rows, :], preferred_element_type=F32).astype(BF16)
            gws_ref[sub] = sum(jnp.dot(gather, p, preferred_element_type=F32) for p in _split3(gw))
        flag_ref[...] = jnp.broadcast_to(over, flag_ref.shape)
        acc_ref[...] = jnp.zeros(acc_ref.shape, F32)

    overflow = jnp.max(flag_ref[0:1, 0:1]) >= MOE_CAP

    @pl.when(jnp.logical_not(overflow))
    def _():
        lo = g * MOE_CAP
        h = jnp.concatenate([hs_ref[sub, pl.ds(lo, MOE_CAP), :] for sub in range(n_sub)], axis=0)
        gwv = jnp.concatenate([gws_ref[sub, pl.ds(lo, MOE_CAP), :] for sub in range(n_sub)], axis=0)
        y = _group_experts(h, gwv, g, wg_ref, wu_ref, wd_ref)
        for sub in range(n_sub):
            ys_ref[sub, pl.ds(lo, MOE_CAP), :] = y[sub * MOE_CAP:(sub + 1) * MOE_CAP]

    @pl.when(overflow)
    def _():
        acc_ref[...] += _group_experts(h2_ref[...], gw_ref[...], g, wg_ref, wu_ref, wd_ref)

    @pl.when(g == N_GROUPS - 1)
    def _():
        @pl.when(jnp.logical_not(overflow))
        def _():
            for sub in range(n_sub):
                rows = slice(sub * MOE_SUB, (sub + 1) * MOE_SUB)
                slot = lax.broadcasted_iota(jnp.int32, (1, seg), 1).astype(F32)
                scatter = jnp.where(dest_ref[rows, 0:1] == slot, 1.0, 0.0).astype(BF16)
                ys = ys_ref[sub]
                y_hi = ys.astype(BF16)
                y_lo = (ys - y_hi.astype(F32)).astype(BF16)
                acc_ref[rows, :] = (jnp.dot(scatter, y_hi, preferred_element_type=F32)
                                    + jnp.dot(scatter, y_lo, preferred_element_type=F32))
        o_ref[...] = _rms_rows(x1_ref[...] + acc_ref[...], gfin_ref[...])


def _moe(h2, gw, x1, w_gate, w_up, w_down, g_final):
    T = h2.shape[0]
    rows = MOE_ROWS
    n_sub = rows // MOE_SUB
    seg = N_GROUPS * MOE_CAP
    row = lambda n: pl.BlockSpec((rows, n), lambda i, g: (i, 0))
    return pl.pallas_call(
        _moe_kernel, grid=(T // rows, N_GROUPS),
        in_specs=[row(D_MODEL), row(ROUTER_LANES),
                  pl.BlockSpec((rows, D_MODEL), lambda i, g: (i, 0), pipeline_mode=pl.Buffered(1)),
                  pl.BlockSpec((EXPERTS_PER_GROUP, D_MODEL, D_EXPERT), lambda i, g: (g, 0, 0)),
                  pl.BlockSpec((EXPERTS_PER_GROUP, D_MODEL, D_EXPERT), lambda i, g: (g, 0, 0)),
                  pl.BlockSpec((EXPERTS_PER_GROUP, D_EXPERT, D_MODEL), lambda i, g: (g, 0, 0)),
                  pl.BlockSpec((1, D_MODEL), lambda i, g: (0, 0))],
        out_specs=row(D_MODEL),
        out_shape=jax.ShapeDtypeStruct((T, D_MODEL), F32),
        scratch_shapes=[pltpu.VMEM((n_sub, seg, D_MODEL), BF16),
                        pltpu.VMEM((n_sub, seg, ROUTER_LANES), F32),
                        pltpu.VMEM((n_sub, seg, D_MODEL), F32),
                        pltpu.VMEM((rows, D_MODEL), F32),
                        pltpu.VMEM((rows, ROUTER_LANES), F32),
                        pltpu.VMEM((SUBLANES, ROUTER_LANES), F32)],
        compiler_params=pltpu.CompilerParams(
            dimension_semantics=("arbitrary", "arbitrary"), vmem_limit_bytes=VMEM_LIMIT_BYTES),
        name="moe_final",
    )(h2, gw, x1, w_gate, w_up, w_down, g_final)


def _rope_inputs(positions):
    half = HEAD_DIM // 2
    inv = 1.0 / (ROPE_THETA ** (jnp.arange(0, HEAD_DIM, 2, dtype=F32) / HEAD_DIM))
    return positions.astype(F32)[..., None], inv[jnp.arange(LANES) % half].reshape(1, LANES)


def _layer(x, pos, inv_freq, g_mix, w_in, w_dw, b_dw, ln_g, ln_b, w_conv_out, w_attn_out, w_o,
           g_ffn, w_rg, b_rg, w_re, b_re, w_gate, w_up, w_down, g_final):
    B, S, D = x.shape
    T = B * S
    offs = [0]
    for n in IN_SIZES:
        offs.append(offs[-1] + n)
    col = lambda j: w_in[:, offs[j]:offs[j + 1]]
    w_ki, w_wi = col(5), col(6)
    ws = (
        col(0), col(1), col(2), col(3), col(4),
        jnp.concatenate([w_ki, w_ki, jnp.pad(w_wi, ((0, 0), (0, LANES - IDX_HEADS)))], axis=1),
        col(7), col(8),
    )
    ws = tuple(w.astype(BF16) for w in ws)
    u, q, k, vt, qi, kk, wi, gc, ga = _inproj(x, g_mix.reshape(1, D), pos, inv_freq, ws)

    mconv = _conv_branch(u, gc, w_dw.reshape(CONV_KERNEL, CONV_WIDTH), b_dw.reshape(1, -1),
                         ln_g.reshape(1, -1), ln_b.reshape(1, -1), w_conv_out.astype(BF16))
    attn = _attention(qi, q, wi, kk, k, vt)

    n_r = N_GROUPS + N_EXPERTS
    w_r = jnp.concatenate([w_rg, w_re.reshape(D, N_EXPERTS)], axis=1)
    w_r = jnp.pad(w_r, ((0, 0), (0, ROUTER_LANES - n_r))).astype(BF16)
    b_r = jnp.pad(jnp.concatenate([b_rg, b_re.reshape(N_EXPERTS)]), (0, ROUTER_LANES - n_r)).reshape(1, -1)
    x1, h2, gw = _mix(x.reshape(T, D), mconv.reshape(T, D), attn.reshape(T, ATTN_WIDTH), ga.reshape(T, D),
                      w_attn_out.astype(BF16), w_o.astype(BF16), g_ffn.reshape(1, D), w_r, b_r)
    out = _moe(h2, gw, x1, w_gate.astype(BF16), w_up.astype(BF16), w_down.astype(BF16), g_final.reshape(1, D))
    return out.reshape(B, S, D)


def kernel(x, positions, g_mix, w_in, w_dw, b_dw, ln_g, ln_b, w_conv_out, w_attn_out, w_o, g_ffn,
           w_rg, b_rg, w_re, b_re, w_gate, w_up, w_down, g_final):
    depth = g_mix.shape[0]
    assert depth == 1, "final norm is fused into the single layer's MoE call"
    pos, inv_freq = _rope_inputs(positions)
    return _layer(x, pos, inv_freq, g_mix[0], w_in[0], w_dw[0], b_dw[0], ln_g[0], ln_b[0],
                  w_conv_out[0], w_attn_out[0], w_o[0], g_ffn[0], w_rg[0], b_rg[0], w_re[0],
                  b_re[0], w_gate[0], w_up[0], w_down[0], g_final)
```

```python
import jax
import jax.numpy as jnp
from jax import lax
from jax.experimental import pallas as pl
from jax.experimental.pallas import tpu as pltpu

F32 = jnp.float32
BF16 = jnp.bfloat16

D_MODEL = 1024
CONV_WIDTH = 512
CONV_KERNEL = 31
N_HEADS = 8
HEAD_DIM = 64
ATTN_WIDTH = N_HEADS * HEAD_DIM
IDX_HEADS = 8
IDX_DIM = 64
TOPK_MAX = 256
ROPE_THETA = 10000.0
N_GROUPS = 4
EXPERTS_PER_GROUP = 4
N_EXPERTS = N_GROUPS * EXPERTS_PER_GROUP
D_EXPERT = 256
EPS = 1e-6
IN_SIZES = (2 * CONV_WIDTH, ATTN_WIDTH, ATTN_WIDTH, ATTN_WIDTH,
            IDX_HEADS * IDX_DIM, IDX_DIM, IDX_HEADS, D_MODEL, D_MODEL)

LANES = 128
SUBLANES = 8
VMEM_LIMIT_BYTES = 56 * 1024 * 1024

PROJ_ROWS = 512
CONV_ROWS = 512
CONV_HALO = 32
CONV_SUB = 64
ATTN_Q = 256
HEAD_GROUP = 4
ONES_ROWS = 16
V_ROWS = HEAD_DIM + ONES_ROWS
KEY_CHUNK = 512
COUNT_ROWS = 256
COUNT_ACC_ROWS = 32
HEAD_ACC_ROWS = 8
MIX_ROWS = 1024
MIX_SUB = 512
MOE_ROWS = 1024
MOE_SUB = 256
GW_BAND = 32
MOE_CAP = 96
ROUTER_LANES = 128
MASK_VALUE = -1e30
KEY16_MIN = -2 ** 15
BF16_MIN_NORMAL_KEY = 1 << 7
PACK_ROWS = 16
STAGE1_BITS = 15
STAGE2_BITS = 17
STAGE2_BELOW = 36000


def _rms_rows(x, g):
    ms = jnp.mean(x * x, axis=-1, keepdims=True)
    return x * lax.rsqrt(ms + EPS) * g


def _sigmoid(x):
    return jax.nn.sigmoid(x)


def _rope128(z, cos, sin, first_half):
    rot = jnp.where(first_half, pltpu.roll(z, LANES - HEAD_DIM // 2, 1),
                    pltpu.roll(z, HEAD_DIM // 2, 1))
    return z * cos + rot * sin


def _inproj_kernel(x_ref, g_ref, pos_ref, inv_ref, wu_ref, wq_ref, wk_ref, wv_ref, wqi_ref,
                   wkw_ref, wgc_ref, wga_ref,
                   u_ref, q_ref, k_ref, vt_ref, qi_ref, kk_ref, wi_ref, gc_ref, ga_ref):
    h = _rms_rows(x_ref[0], g_ref[...]).astype(BF16)
    ang = pos_ref[0] * inv_ref[...]
    lane = lax.broadcasted_iota(jnp.int32, ang.shape, 1)
    first_half = (lane % HEAD_DIM) < (HEAD_DIM // 2)
    cos = jnp.cos(ang)
    sin = jnp.sin(ang)
    sin = jnp.where(first_half, -sin, sin)

    def proj(w_ref):
        return jnp.dot(h, w_ref[...], preferred_element_type=F32)

    u_ref[0] = proj(wu_ref)
    gc_ref[0] = proj(wgc_ref)
    ga_ref[0] = proj(wga_ref)
    for w_ref, o_ref in ((wq_ref, q_ref), (wk_ref, k_ref), (wqi_ref, qi_ref)):
        z = proj(w_ref)
        for c in range(ATTN_WIDTH // LANES):
            sl = slice(c * LANES, (c + 1) * LANES)
            o_ref[0, :, sl] = _rope128(z[:, sl], cos, sin, first_half).astype(BF16)
    kw = proj(wkw_ref)
    kk_ref[0] = _rope128(kw[:, :LANES], cos, sin, first_half).astype(BF16)
    idx_scale = (IDX_HEADS ** -0.5) * (IDX_DIM ** -0.5)
    wi_ref[0] = kw[:, LANES:] * idx_scale
    vt = proj(wv_ref).T.astype(BF16)
    ones = jnp.ones((ONES_ROWS, vt.shape[1]), BF16)
    vt_ref[0] = jnp.concatenate(
        [piece for hd in range(N_HEADS) for piece in (vt[hd * HEAD_DIM:(hd + 1) * HEAD_DIM], ones)], axis=0)


def _inproj(x, g_mix, pos, inv_freq, ws):
    B, S, D = x.shape
    rows = PROJ_ROWS
    grid = (B, S // rows)
    row_spec = lambda n: pl.BlockSpec((1, rows, n), lambda b, i: (b, i, 0))
    w_spec = lambda w: pl.BlockSpec(w.shape, lambda b, i: (0, 0), pipeline_mode=pl.Buffered(1))
    out_shape = (
        jax.ShapeDtypeStruct((B, S, 2 * CONV_WIDTH), F32),
        jax.ShapeDtypeStruct((B, S, ATTN_WIDTH), BF16),
        jax.ShapeDtypeStruct((B, S, ATTN_WIDTH), BF16),
        jax.ShapeDtypeStruct((B, N_HEADS * V_ROWS, S), BF16),
        jax.ShapeDtypeStruct((B, S, ATTN_WIDTH), BF16),
        jax.ShapeDtypeStruct((B, S, LANES), BF16),
        jax.ShapeDtypeStruct((B, S, LANES), F32),
        jax.ShapeDtypeStruct((B, S, D_MODEL), F32),
        jax.ShapeDtypeStruct((B, S, D_MODEL), F32),
    )
    out_specs = (
        row_spec(2 * CONV_WIDTH), row_spec(ATTN_WIDTH), row_spec(ATTN_WIDTH),
        pl.BlockSpec((1, N_HEADS * V_ROWS, rows), lambda b, i: (b, 0, i)),
        row_spec(ATTN_WIDTH), row_spec(LANES), row_spec(LANES), row_spec(D_MODEL), row_spec(D_MODEL),
    )
    in_specs = [row_spec(D), pl.BlockSpec((1, D), lambda b, i: (0, 0)), row_spec(1),
                pl.BlockSpec((1, LANES), lambda b, i: (0, 0))]
    in_specs += [w_spec(w) for w in ws]
    return pl.pallas_call(
        _inproj_kernel, grid=grid, in_specs=in_specs, out_specs=out_specs, out_shape=out_shape,
        compiler_params=pltpu.CompilerParams(
            dimension_semantics=("arbitrary", "arbitrary"), vmem_limit_bytes=VMEM_LIMIT_BYTES),
        name="inproj",
    )(x, g_mix, pos, inv_freq, *ws)


def _conv_kernel(u_ref, uh_ref, gc_ref, wdw_ref, bdw_ref, lng_ref, lnb_ref, wout_ref,
                 o_ref, g_buf, s_buf):
    i = pl.program_id(1)
    rows = u_ref.shape[1]
    uh = uh_ref[0]
    gh = uh[:, :CONV_WIDTH] * _sigmoid(uh[:, CONV_WIDTH:])
    g_buf[0, 0:CONV_HALO, :] = jnp.where(i > 0, gh, 0.0)
    um = u_ref[0]
    g_buf[0, CONV_HALO:CONV_HALO + rows, :] = um[:, :CONV_WIDTH] * _sigmoid(um[:, CONV_WIDTH:])
    span = CONV_HALO + rows - SUBLANES
    for r in range(1, SUBLANES):
        for base in range(0, span, CONV_SUB):
            n = min(CONV_SUB, span - base)
            g_buf[r, base:base + n, :] = g_buf[0, pl.ds(base + r, n), :]
    first = CONV_HALO - (CONV_KERNEL - 1)
    for rr in range(rows // CONV_SUB):
        acc = jnp.zeros((CONV_SUB, CONV_WIDTH), F32)
        for t in range(CONV_KERNEL):
            shift = (first + t) % SUBLANES
            row0 = rr * CONV_SUB + (first + t) - shift
            acc = acc + wdw_ref[t:t + 1, :] * g_buf[shift, row0:row0 + CONV_SUB, :]
        c = acc + bdw_ref[...]
        mu = jnp.mean(c, axis=-1, keepdims=True)
        d = c - mu
        var = jnp.mean(d * d, axis=-1, keepdims=True)
        n = d * lax.rsqrt(var + EPS) * lng_ref[...] + lnb_ref[...]
        s_buf[rr * CONV_SUB:(rr + 1) * CONV_SUB, :] = (n * _sigmoid(n)).astype(BF16)
    y = jnp.dot(s_buf[...], wout_ref[...], preferred_element_type=F32)
    o_ref[0] = _sigmoid(gc_ref[0]) * y


def _conv_branch(u, gc, w_dw, b_dw, ln_g, ln_b, w_out):
    B, S, _ = u.shape
    rows = CONV_ROWS
    halo_per_tile = rows // CONV_HALO
    grid = (B, S // rows)
    vec = lambda n: pl.BlockSpec((1, n), lambda b, i: (0, 0))
    return pl.pallas_call(
        _conv_kernel, grid=grid,
        in_specs=[
            pl.BlockSpec((1, rows, 2 * CONV_WIDTH), lambda b, i: (b, i, 0)),
            pl.BlockSpec((1, CONV_HALO, 2 * CONV_WIDTH),
                         lambda b, i: (b, jnp.maximum(i * halo_per_tile - 1, 0), 0)),
            pl.BlockSpec((1, rows, D_MODEL), lambda b, i: (b, i, 0)),
            pl.BlockSpec((CONV_KERNEL, CONV_WIDTH), lambda b, i: (0, 0)),
            vec(CONV_WIDTH), vec(CONV_WIDTH), vec(CONV_WIDTH),
            pl.BlockSpec((CONV_WIDTH, D_MODEL), lambda b, i: (0, 0)),
        ],
        out_specs=pl.BlockSpec((1, rows, D_MODEL), lambda b, i: (b, i, 0)),
        out_shape=jax.ShapeDtypeStruct((B, S, D_MODEL), F32),
        scratch_shapes=[pltpu.VMEM((SUBLANES, CONV_HALO + rows, CONV_WIDTH), F32),
                        pltpu.VMEM((rows, CONV_WIDTH), BF16)],
        compiler_params=pltpu.CompilerParams(
            dimension_semantics=("arbitrary", "arbitrary"), vmem_limit_bytes=VMEM_LIMIT_BYTES),
        name="conv_branch",
    )(u, u, gc, w_dw, b_dw, ln_g, ln_b, w_out)


def _pair_rhs(x_bf16, scale=None):
    x = x_bf16.astype(F32)
    if scale is not None:
        x = x * scale
    lane = lax.broadcasted_iota(jnp.int32, x.shape, 1)
    lo = jnp.where(lane < HEAD_DIM, x, 0.0)
    hi = jnp.where(lane >= HEAD_DIM, x, 0.0)
    return jnp.concatenate([lo, hi], axis=0).astype(BF16)


def _fold_rows(x, op, rows):
    n = x.shape[0] // rows
    return op(x.reshape(n, rows, x.shape[1]), axis=0)


def _finish_rows(x, op):
    y = op(x.reshape(x.shape[0] // SUBLANES, SUBLANES, x.shape[1]), axis=0)
    return op(y, axis=0, keepdims=True)


def _chunk_loop(n, body, init):
    def quad(i, cr):
        for k in range(4):
            cr = body(4 * i + k, cr)
        return cr
    carry = lax.fori_loop(0, n // 4, quad, init)
    base = (n // 4) * 4
    carry = lax.cond(n % 4 >= 2, lambda cr: body(base + 1, body(base, cr)), lambda cr: cr, carry)
    return lax.cond(n % 2 == 1, lambda cr: body(n - 1, cr), lambda cr: cr, carry)


def _nt_dot(a, b):
    return lax.dot_general(a, b, (((1,), (1,)), ((), ())), preferred_element_type=F32)


def _key16_to_key32(k16):
    bits = lax.shift_left(jnp.where(k16 >= 0, k16, k16 ^ jnp.int32(0x7FFF)), 16)
    return jnp.where(bits >= 0, bits, bits ^ jnp.int32(0x7FFFFFFF))


def _key_to_f32(key):
    bits = jnp.where(key >= 0, key, key ^ jnp.int32(0x7FFFFFFF))
    return lax.bitcast_convert_type(bits, F32)


def _attn_kernel(qi_ref, q_ref, wi_ref, kk_ref, k_ref, vt_ref, o_ref,
                 sc_ref, sb_ref, xs_ref, acc_ref, out_ref):
    i = pl.program_id(1)
    kc = KEY_CHUNK
    n_chunks = ((i + 1) * ATTN_Q + kc - 1) // kc
    topk = float(TOPK_MAX)
    t_idx = i * ATTN_Q + lax.broadcasted_iota(jnp.int32, (1, ATTN_Q), 1)
    s_iota = lax.broadcasted_iota(jnp.int32, (kc, ATTN_Q), 0)
    n_pairs = N_HEADS // 2

    qi = qi_ref[0]
    idx_rhs = [_pair_rhs(qi[:, p * LANES:(p + 1) * LANES]) for p in range(IDX_HEADS // 2)]
    wi_t = wi_ref[0].T

    def score_body(c, carry):
        start = pl.multiple_of(c * kc, kc)
        kk = kk_ref[0, pl.ds(start, kc), :]
        s = jnp.zeros((kc, ATTN_Q), F32)
        for p in range(IDX_HEADS // 2):
            r = _nt_dot(kk, idx_rhs[p])
            s = s + wi_t[2 * p:2 * p + 1, :] * jnp.maximum(r[:, :ATTN_Q], 0.0)
            s = s + wi_t[2 * p + 1:2 * p + 2, :] * jnp.maximum(r[:, ATTN_Q:], 0.0)
        causal = (start + s_iota) <= t_idx
        s = jnp.where(causal, s, -jnp.inf)
        sc_ref[pl.ds(start, kc), :] = s
        sb_ref[pl.ds(start, kc), :] = s.astype(BF16)
        return carry

    _chunk_loop(n_chunks, score_body, 0)

    n_count = n_chunks * (kc // COUNT_ROWS)

    def count_ge(cand):
        def body(c, acc):
            for half in range(kc // COUNT_ROWS):
                start = pl.multiple_of(c * kc + half * COUNT_ROWS, COUNT_ROWS)
                hit = jnp.where(sc_ref[pl.ds(start, COUNT_ROWS), :] >= cand, 1.0, 0.0)
                acc = acc + _fold_rows(hit, jnp.sum, COUNT_ACC_ROWS)
            return acc
        acc = lax.fori_loop(0, n_chunks, body, jnp.zeros((COUNT_ACC_ROWS, ATTN_Q), F32))
        return _finish_rows(acc, jnp.sum)

    def count_ge_bf16(cand):
        def body(c, acc):
            for half in range(kc // COUNT_ROWS):
                start = pl.multiple_of(c * kc + half * COUNT_ROWS, COUNT_ROWS)
                hit = jnp.where(sb_ref[pl.ds(start, COUNT_ROWS), :] >= cand, jnp.ones((), BF16), jnp.zeros((), BF16))
                parts = [hit[r * PACK_ROWS:(r + 1) * PACK_ROWS] for r in range(COUNT_ROWS // PACK_ROWS)]
                while len(parts) > 1:
                    parts = [a + b for a, b in zip(parts[::2], parts[1::2])]
                acc = acc + parts[0].astype(F32)
            return acc
        acc = lax.fori_loop(0, n_chunks, body, jnp.zeros((PACK_ROWS, ATTN_Q), F32))
        return _finish_rows(acc, jnp.sum)

    c0 = count_ge_bf16(jnp.zeros((1, ATTN_Q), BF16))
    k16_0 = jnp.where(c0 >= topk, jnp.int32(0), jnp.int32(KEY16_MIN))

    def bit16_body(j, k16):
        cand = k16 + lax.shift_left(jnp.int32(1), STAGE1_BITS - 1 - j)
        cnt = count_ge_bf16(_key_to_f32(_key16_to_key32(cand)).astype(BF16))
        return jnp.where(cnt >= topk, cand, k16)

    k16 = lax.fori_loop(0, STAGE1_BITS, bit16_body, k16_0)
    k16 = jnp.where((k16 > 0) & (k16 < BF16_MIN_NORMAL_KEY), 0, k16)
    low = _key16_to_key32(k16) - jnp.int32(STAGE2_BELOW)

    def bit_body(j, carry):
        key, cnt_key = carry
        cand = key + lax.shift_left(jnp.int32(1), STAGE2_BITS - 1 - j)
        cnt = count_ge(_key_to_f32(cand))
        take = cnt >= topk
        return jnp.where(take, cand, key), jnp.where(take, cnt, cnt_key)

    key, cnt_ge = lax.fori_loop(0, STAGE2_BITS, bit_body, (low, jnp.full((1, ATTN_Q), jnp.inf, F32)))
    thr = _key_to_f32(key)
    thr = jnp.where((k16 == KEY16_MIN) | (thr != thr), -jnp.inf, thr)
    excess = jnp.where((cnt_ge > topk) & (thr > -jnp.inf), 1.0, 0.0)
    has_excess = jnp.sum(excess) > 0.0

    @pl.when(has_excess)
    def _():
        r_i = lax.broadcasted_iota(jnp.int32, (kc, kc), 0)
        c_i = lax.broadcasted_iota(jnp.int32, (kc, kc), 1)
        strict_lower = jnp.where(c_i < r_i, 1.0, 0.0).astype(BF16)

        def count_gt(c, acc):
            start = pl.multiple_of(c * COUNT_ROWS, COUNT_ROWS)
            hit = jnp.where(sc_ref[pl.ds(start, COUNT_ROWS), :] > thr, 1.0, 0.0)
            return acc + _fold_rows(hit, jnp.sum, COUNT_ACC_ROWS)

        gt_acc = lax.fori_loop(0, n_count, count_gt, jnp.zeros((COUNT_ACC_ROWS, ATTN_Q), F32))
        need = topk - _finish_rows(gt_acc, jnp.sum)

        def body(c, before):
            start = pl.multiple_of(c * kc, kc)
            s = sc_ref[pl.ds(start, kc), :]
            tie = jnp.where(s == thr, 1.0, 0.0)
            rank = jnp.dot(strict_lower, tie.astype(BF16), preferred_element_type=F32) + before
            keep = (s > thr) | ((s == thr) & (rank < need))
            sc_ref[pl.ds(start, kc), :] = jnp.where(keep, jnp.inf, -jnp.inf)
            return before + jnp.sum(tie, axis=0, keepdims=True)
        lax.fori_loop(0, n_chunks, body, jnp.zeros((1, ATTN_Q), F32))

    q = q_ref[0]
    acc_ref[...] = jnp.zeros(acc_ref.shape, F32)
    neg = jnp.full((HEAD_ACC_ROWS, ATTN_Q), -jnp.inf, F32)
    n_groups = N_HEADS // HEAD_GROUP
    att_rhs = [_pair_rhs(q[:, p * LANES:(p + 1) * LANES], HEAD_DIM ** -0.5) for p in range(n_pairs)]

    def qk_part(start, g, mx):
        if g == 0:
            keep = (sc_ref[pl.ds(start, kc), :] >= thr) & ((start + s_iota) <= t_idx)
            bias = jnp.where(keep, 0.0, MASK_VALUE)
            sc_ref[pl.ds(start, kc), :] = bias
        else:
            bias = sc_ref[pl.ds(start, kc), :]
        out = []
        for p in range(g * HEAD_GROUP // 2, (g + 1) * HEAD_GROUP // 2):
            kp = k_ref[0, pl.ds(start, kc), p * LANES:(p + 1) * LANES]
            lg = _nt_dot(kp, att_rhs[p])
            for hh in range(2):
                hl = 2 * p + hh - g * HEAD_GROUP
                x = lg[:, hh * ATTN_Q:(hh + 1) * ATTN_Q] + bias
                xs_ref[g % 2, pl.ds(start, kc), hl * ATTN_Q:(hl + 1) * ATTN_Q] = x
                out.append(jnp.maximum(mx[hl], _fold_rows(x, jnp.max, HEAD_ACC_ROWS)))
        return tuple(out)

    def pv_part(start, g, m):
        for hl in range(HEAD_GROUP):
            h = g * HEAD_GROUP + hl
            pm = jnp.exp(xs_ref[g % 2, pl.ds(start, kc), hl * ATTN_Q:(hl + 1) * ATTN_Q] - m[hl])
            rows = slice(h * V_ROWS, (h + 1) * V_ROWS)
            acc_ref[rows, :] += jnp.dot(vt_ref[0, rows, pl.ds(start, kc)], pm.astype(BF16),
                                        preferred_element_type=F32)

    m = None
    for s in range(n_groups + 1):
        def body(c, mx, s=s, m=m):
            start = pl.multiple_of(c * kc, kc)
            if s < n_groups:
                mx = qk_part(start, s, mx)
            if s > 0:
                pv_part(start, s - 1, m)
            return mx

        mx = _chunk_loop(n_chunks, body, (neg,) * HEAD_GROUP)
        if s < n_groups:
            m = [_finish_rows(mx[hl], jnp.max) for hl in range(HEAD_GROUP)]
    for h in range(N_HEADS):
        pv = acc_ref[h * V_ROWS:h * V_ROWS + HEAD_DIM, :]
        denom = acc_ref[h * V_ROWS + HEAD_DIM:h * V_ROWS + HEAD_DIM + 1, :]
        out_ref[h * HEAD_DIM:(h + 1) * HEAD_DIM, :] = pv / denom
    o_ref[0] = out_ref[...].T.astype(BF16)


def _attention(qi, q, wi, kk, k, vt):
    B, S, _ = q.shape
    grid = (B, S // ATTN_Q)
    blk = lambda n: pl.BlockSpec((1, ATTN_Q, n), lambda b, i: (b, i, 0))
    full = lambda n: pl.BlockSpec((1, S, n), lambda b, i: (b, 0, 0), pipeline_mode=pl.Buffered(1))
    return pl.pallas_call(
        _attn_kernel, grid=grid,
        in_specs=[blk(ATTN_WIDTH), blk(ATTN_WIDTH), blk(LANES), full(LANES), full(ATTN_WIDTH),
                  pl.BlockSpec((1, N_HEADS * V_ROWS, S), lambda b, i: (b, 0, 0), pipeline_mode=pl.Buffered(1))],
        out_specs=blk(ATTN_WIDTH),
        out_shape=jax.ShapeDtypeStruct((B, S, ATTN_WIDTH), BF16),
        scratch_shapes=[pltpu.VMEM((S, ATTN_Q), F32),
                        pltpu.VMEM((S, ATTN_Q), BF16),
                        pltpu.VMEM((2, S, HEAD_GROUP * ATTN_Q), F32),
                        pltpu.VMEM((N_HEADS * V_ROWS, ATTN_Q), F32),
                        pltpu.VMEM((ATTN_WIDTH, ATTN_Q), F32)],
        compiler_params=pltpu.CompilerParams(
            dimension_semantics=("arbitrary", "arbitrary"), vmem_limit_bytes=VMEM_LIMIT_BYTES),
        name="sparse_attn",
    )(qi, q, wi, kk, k, vt)


def _lane_first(mask, lane):
    return jnp.min(jnp.where(mask, lane, ROUTER_LANES), axis=-1, keepdims=True)


def _router(logits):
    lane = lax.broadcasted_iota(jnp.int32, logits.shape, 1)
    neg = -jnp.inf
    gl = jnp.where(lane < N_GROUPS, logits, neg)
    gmax = jnp.max(gl, axis=-1, keepdims=True)
    gsum = jnp.sum(jnp.exp(gl - gmax), axis=-1, keepdims=True)
    p_g = 1.0 / gsum
    gi = _lane_first(gl == gmax, lane)
    e_lo = N_GROUPS + gi * EXPERTS_PER_GROUP
    in_group = (lane >= e_lo) & (lane < e_lo + EXPERTS_PER_GROUP)
    el = jnp.where(in_group, logits, neg)
    emax = jnp.max(el, axis=-1, keepdims=True)
    ee = jnp.exp(el - emax)
    pe = ee / jnp.sum(ee, axis=-1, keepdims=True)
    pe = jnp.where(in_group, pe, -1.0)
    p1 = jnp.max(pe, axis=-1, keepdims=True)
    l1 = _lane_first(pe == p1, lane)
    pe2 = jnp.where(lane == l1, -1.0, pe)
    p2 = jnp.max(pe2, axis=-1, keepdims=True)
    l2 = _lane_first(pe2 == p2, lane)
    tot = p1 + p2
    c1 = p_g * (p1 / tot)
    c2 = p_g * (p2 / tot)
    return (jnp.where(lane == l1, c1, 0.0) + jnp.where(lane == l2, c2, 0.0)
            + jnp.where(lane == 0, gi.astype(F32), 0.0))


def _mix_kernel(x_ref, mconv_ref, attn_ref, ga_ref, wao_ref, wo_ref, gffn_ref, wr_ref, br_ref,
                x1_ref, h2_ref, gw_ref):
    for r0 in range(0, x_ref.shape[0], MIX_SUB):
        sl = slice(r0, r0 + MIX_SUB)
        ya = jnp.dot(attn_ref[sl, :], wao_ref[...], preferred_element_type=F32)
        m = mconv_ref[sl, :] + _sigmoid(ga_ref[sl, :]) * ya
        x1 = x_ref[sl, :] + jnp.dot(m.astype(BF16), wo_ref[...], preferred_element_type=F32)
        x1_ref[sl, :] = x1
        h2 = _rms_rows(x1, gffn_ref[...]).astype(BF16)
        h2_ref[sl, :] = h2
        logits = jnp.dot(h2, wr_ref[...], preferred_element_type=F32) + br_ref[...]
        gw_ref[sl, :] = _router(logits)


def _mix(x, mconv, attn, ga, w_ao, w_o, g_ffn, w_r, b_r):
    T = x.shape[0]
    rows = MIX_ROWS
    row = lambda n: pl.BlockSpec((rows, n), lambda i: (i, 0))
    const = lambda a: pl.BlockSpec(a.shape, lambda i: (0, 0))
    return pl.pallas_call(
        _mix_kernel, grid=(T // rows,),
        in_specs=[row(D_MODEL), row(D_MODEL), row(ATTN_WIDTH), row(D_MODEL),
                  const(w_ao), const(w_o), const(g_ffn), const(w_r), const(b_r)],
        out_specs=(row(D_MODEL), row(D_MODEL), row(ROUTER_LANES)),
        out_shape=(jax.ShapeDtypeStruct((T, D_MODEL), F32),
                   jax.ShapeDtypeStruct((T, D_MODEL), BF16),
                   jax.ShapeDtypeStruct((T, ROUTER_LANES), F32)),
        compiler_params=pltpu.CompilerParams(
            dimension_semantics=("arbitrary",), vmem_limit_bytes=VMEM_LIMIT_BYTES),
        name="mix_router",
    )(x, mconv, attn, ga, w_ao, w_o, g_ffn, w_r, b_r)


def _split3(x):
    a = x.astype(BF16)
    r = x - a.astype(F32)
    b = r.astype(BF16)
    return a, b, (r - b.astype(F32)).astype(BF16)


def _group_experts(h, gwv, g, wg_ref, wu_ref, wd_ref):
    lane = lax.broadcasted_iota(jnp.int32, gwv.shape, 1)
    y = None
    for j in range(EXPERTS_PER_GROUP):
        gate = jnp.dot(h, wg_ref[j], preferred_element_type=F32)
        up = jnp.dot(h, wu_ref[j], preferred_element_type=F32)
        expert_lane = N_GROUPS + g * EXPERTS_PER_GROUP + j
        gwe = jnp.sum(jnp.where(lane == expert_lane, gwv, 0.0), axis=-1, keepdims=True)
        hid = (gate * _sigmoid(gate)) * up * gwe
        yj = jnp.dot(hid.astype(BF16), wd_ref[j], preferred_element_type=F32)
        y = yj if y is None else y + yj
    return y


def _moe_kernel(h2_ref, gw_ref, x1_ref, wg_ref, wu_ref, wd_ref, gfin_ref, o_ref,
                hs_ref, gws_ref, ys_ref, acc_ref, dest_ref, flag_ref):
    g = pl.program_id(1)
    n_sub = MOE_ROWS // MOE_SUB
    seg = N_GROUPS * MOE_CAP

    @pl.when(g == 0)
    def _():
        r_i = lax.broadcasted_iota(jnp.int32, (MOE_SUB, MOE_SUB), 0)
        c_i = lax.broadcasted_iota(jnp.int32, (MOE_SUB, MOE_SUB), 1)
        tri = jnp.where(c_i < r_i, 1.0, 0.0).astype(BF16)
        over = jnp.zeros((1, 1), F32)
        for sub in range(n_sub):
            rows = slice(sub * MOE_SUB, (sub + 1) * MOE_SUB)
            gw = gw_ref[rows, :]
            lane = lax.broadcasted_iota(jnp.int32, gw.shape, 1)
            gi = jnp.sum(jnp.where(lane == 0, gw, 0.0), axis=-1, keepdims=True)
            onehot = jnp.where((lane < N_GROUPS) & (lane.astype(F32) == gi), 1.0, 0.0)
            before = jnp.dot(tri, onehot.astype(BF16), preferred_element_type=F32)
            rank = jnp.sum(onehot * before, axis=-1, keepdims=True)
            over = jnp.maximum(over, jnp.max(rank, axis=0, keepdims=True))
            dest = jnp.where(rank < MOE_CAP, gi * MOE_CAP + rank, -1.0)
            dest_col = jnp.where(lane == 0, dest, 0.0)
            dest_ref[rows, :] = dest_col
            dest_row = dest_col.T[0:1, :]
            slot = lax.broadcasted_iota(jnp.int32, (seg, 1), 0).astype(F32)
            gather = jnp.where(dest_row == slot, 1.0, 0.0).astype(BF16)
            hs_ref[sub] = jnp.dot(gather, h2_ref[rows, :], preferred_element_type=F32).astype(BF16)
            pa, pb, pc = (p.astype(F32) for p in _split3(gw))
            packed = (pa + pltpu.roll(pb, GW_BAND, 1) + pltpu.roll(pc, 2 * GW_BAND, 1)).astype(BF16)
            got = jnp.dot(gather, packed, preferred_element_type=F32)
            gws_ref[sub] = (got + pltpu.roll(got, ROUTER_LANES - GW_BAND, 1)
                            + pltpu.roll(got, ROUTER_LANES - 2 * GW_BAND, 1))
        flag_ref[...] = jnp.broadcast_to(over, flag_ref.shape)
        acc_ref[...] = jnp.zeros(acc_ref.shape, F32)

    overflow = jnp.max(flag_ref[0:1, 0:1]) >= MOE_CAP

    @pl.when(jnp.logical_not(overflow))
    def _():
        lo = g * MOE_CAP
        h = jnp.concatenate([hs_ref[sub, pl.ds(lo, MOE_CAP), :] for sub in range(n_sub)], axis=0)
        gwv = jnp.concatenate([gws_ref[sub, pl.ds(lo, MOE_CAP), :] for sub in range(n_sub)], axis=0)
        y = _group_experts(h, gwv, g, wg_ref, wu_ref, wd_ref)
        for sub in range(n_sub):
            ys_ref[sub, pl.ds(lo, MOE_CAP), :] = y[sub * MOE_CAP:(sub + 1) * MOE_CAP]

    @pl.when(overflow)
    def _():
        acc_ref[...] += _group_experts(h2_ref[...], gw_ref[...], g, wg_ref, wu_ref, wd_ref)

    @pl.when(g == N_GROUPS - 1)
    def _():
        @pl.when(jnp.logical_not(overflow))
        def _():
            for sub in range(n_sub):
                rows = slice(sub * MOE_SUB, (sub + 1) * MOE_SUB)
                slot = lax.broadcasted_iota(jnp.int32, (1, seg), 1).astype(F32)
                scatter = jnp.where(dest_ref[rows, 0:1] == slot, 1.0, 0.0).astype(BF16)
                ys = ys_ref[sub]
                y_hi = ys.astype(BF16)
                y_lo = (ys - y_hi.astype(F32)).astype(BF16)
                acc_ref[rows, :] = (jnp.dot(scatter, y_hi, preferred_element_type=F32)
                                    + jnp.dot(scatter, y_lo, preferred_element_type=F32))
        o_ref[...] = _rms_rows(x1_ref[...] + acc_ref[...], gfin_ref[...])


def _moe(h2, gw, x1, w_gate, w_up, w_down, g_final):
    T = h2.shape[0]
    rows = MOE_ROWS
    n_sub = rows // MOE_SUB
    seg = N_GROUPS * MOE_CAP
    row = lambda n: pl.BlockSpec((rows, n), lambda i, g: (i, 0))
    return pl.pallas_call(
        _moe_kernel, grid=(T // rows, N_GROUPS),
        in_specs=[row(D_MODEL), row(ROUTER_LANES),
                  pl.BlockSpec((rows, D_MODEL), lambda i, g: (i, 0), pipeline_mode=pl.Buffered(1)),
                  pl.BlockSpec((EXPERTS_PER_GROUP, D_MODEL, D_EXPERT), lambda i, g: (g, 0, 0)),
                  pl.BlockSpec((EXPERTS_PER_GROUP, D_MODEL, D_EXPERT), lambda i, g: (g, 0, 0)),
                  pl.BlockSpec((EXPERTS_PER_GROUP, D_EXPERT, D_MODEL), lambda i, g: (g, 0, 0)),
                  pl.BlockSpec((1, D_MODEL), lambda i, g: (0, 0))],
        out_specs=row(D_MODEL),
        out_shape=jax.ShapeDtypeStruct((T, D_MODEL), F32),
        scratch_shapes=[pltpu.VMEM((n_sub, seg, D_MODEL), BF16),
                        pltpu.VMEM((n_sub, seg, ROUTER_LANES), F32),
                        pltpu.VMEM((n_sub, seg, D_MODEL), F32),
                        pltpu.VMEM((rows, D_MODEL), F32),
                        pltpu.VMEM((rows, ROUTER_LANES), F32),
                        pltpu.VMEM((SUBLANES, ROUTER_LANES), F32)],
        compiler_params=pltpu.CompilerParams(
            dimension_semantics=("arbitrary", "arbitrary"), vmem_limit_bytes=VMEM_LIMIT_BYTES),
        name="moe_final",
    )(h2, gw, x1, w_gate, w_up, w_down, g_final)


def _rope_inputs(positions):
    half = HEAD_DIM // 2
    inv = 1.0 / (ROPE_THETA ** (jnp.arange(0, HEAD_DIM, 2, dtype=F32) / HEAD_DIM))
    return positions.astype(F32)[..., None], inv[jnp.arange(LANES) % half].reshape(1, LANES)


def _layer(x, pos, inv_freq, g_mix, w_in, w_dw, b_dw, ln_g, ln_b, w_conv_out, w_attn_out, w_o,
           g_ffn, w_rg, b_rg, w_re, b_re, w_gate, w_up, w_down, g_final):
    B, S, D = x.shape
    T = B * S
    offs = [0]
    for n in IN_SIZES:
        offs.append(offs[-1] + n)
    col = lambda j: w_in[:, offs[j]:offs[j + 1]]
    w_ki, w_wi = col(5), col(6)
    ws = (
        col(0), col(1), col(2), col(3), col(4),
        jnp.concatenate([w_ki, w_ki, jnp.pad(w_wi, ((0, 0), (0, LANES - IDX_HEADS)))], axis=1),
        col(7), col(8),
    )
    ws = tuple(w.astype(BF16) for w in ws)
    u, q, k, vt, qi, kk, wi, gc, ga = _inproj(x, g_mix.reshape(1, D), pos, inv_freq, ws)

    mconv = _conv_branch(u, gc, w_dw.reshape(CONV_KERNEL, CONV_WIDTH), b_dw.reshape(1, -1),
                         ln_g.reshape(1, -1), ln_b.reshape(1, -1), w_conv_out.astype(BF16))
    attn = _attention(qi, q, wi, kk, k, vt)

    n_r = N_GROUPS + N_EXPERTS
    w_r = jnp.concatenate([w_rg, w_re.reshape(D, N_EXPERTS)], axis=1)
    w_r = jnp.pad(w_r, ((0, 0), (0, ROUTER_LANES - n_r))).astype(BF16)
    b_r = jnp.pad(jnp.concatenate([b_rg, b_re.reshape(N_EXPERTS)]), (0, ROUTER_LANES - n_r)).reshape(1, -1)
    x1, h2, gw = _mix(x.reshape(T, D), mconv.reshape(T, D), attn.reshape(T, ATTN_WIDTH), ga.reshape(T, D),
                      w_attn_out.astype(BF16), w_o.astype(BF16), g_ffn.reshape(1, D), w_r, b_r)
    out = _moe(h2, gw, x1, w_gate.astype(BF16), w_up.astype(BF16), w_down.astype(BF16), g_final.reshape(1, D))
    return out.reshape(B, S, D)


def kernel(x, positions, g_mix, w_in, w_dw, b_dw, ln_g, ln_b, w_conv_out, w_attn_out, w_o, g_ffn,
           w_rg, b_rg, w_re, b_re, w_gate, w_up, w_down, g_final):
    depth = g_mix.shape[0]
    assert depth == 1, "final norm is fused into the single layer's MoE call"
    pos, inv_freq = _rope_inputs(positions)
    return _layer(x, pos, inv_freq, g_mix[0], w_in[0], w_dw[0], b_dw[0], ln_g[0], ln_b[0],
                  w_conv_out[0], w_attn_out[0], w_o[0], g_ffn[0], w_rg[0], b_rg[0], w_re[0],
                  b_re[0], w_gate[0], w_up[0], w_down[0], g_final)
```
